```python
import math
import jax, jax.numpy as jnp
from jax import lax
import numpy as np

D_MODEL = 1024
BATCH = 8
SEQ = 2048
DEPTH = 2

D_MIX = D_MODEL
D_A = D_MIX // 2
N_HEADS_A = 8
HEAD_DIM_A = D_A // N_HEADS_A
CHUNK = 128
D_B = D_MIX - D_A
N_GROUPS_B = 8
HYENA_ORDER = 2
N_PROJ_B = HYENA_ORDER + 1
SHORT_K = 3
IN_COLS = 2 * D_A + N_PROJ_B * D_B
FILTER_EMB = 33
FILTER_BANDS = (FILTER_EMB - 1) // 2
FILTER_WIDTH = 64
FILTER_OUT_SCALE = 0.1
DECAY_TARGET = 1e-2
FAST_DECAY_PCT = 0.3
SLOW_DECAY_PCT = 1.5
MAX_DECAY = math.log(DECAY_TARGET) / FAST_DECAY_PCT
MIN_DECAY = math.log(DECAY_TARGET) / SLOW_DECAY_PCT
D_FF = 2816
HALF_STEP = 0.5
RMS_EPS = 1e-6
LN_EPS = 1e-5

kernel_name = 'hybrid_gmlp_hyena_macaron_encoder'


def _rms_norm(x, g):
    xf = x.astype(jnp.float32)
    y = xf * lax.rsqrt(jnp.mean(jnp.square(xf), axis=-1, keepdims=True) + RMS_EPS)
    return (y * g.astype(jnp.float32)).astype(x.dtype)


def _layer_norm(x, g, b):
    xf = x.astype(jnp.float32)
    mu = jnp.mean(xf, axis=-1, keepdims=True)
    var = jnp.mean(jnp.square(xf - mu), axis=-1, keepdims=True)
    y = (xf - mu) * lax.rsqrt(var + LN_EPS)
    return (y * g.astype(jnp.float32) + b.astype(jnp.float32)).astype(x.dtype)


def _swiglu(h, w_gate, w_up, w_down):
    return (jax.nn.silu(h @ w_gate) * (h @ w_up)) @ w_down


def _spatial_gating(u, v, ln_g, ln_b, w_s, b_s):
    bsz, seq, _ = v.shape
    vn = _layer_norm(v, ln_g, ln_b).reshape(bsz, seq // CHUNK, CHUNK, N_HEADS_A, HEAD_DIM_A)
    mixed = jnp.einsum('hmc,bnchd->bnmhd', w_s, vn) + b_s.T[:, :, None]
    return u * mixed.reshape(bsz, seq, D_A)


def _short_conv(z, w, b):
    seq = z.shape[1]
    pad = SHORT_K // 2
    zp = jnp.pad(z, ((0, 0), (pad, pad), (0, 0)))
    out = b
    for k in range(SHORT_K):
        out = out + zp[:, k:k + seq] * w[k]
    return out


def _hyena_filter_spectrum(seq, w1, b1, w2, b2, w3, b3, freq, w_o):
    t = jnp.linspace(0.0, 1.0, seq, dtype=jnp.float32)[:, None]
    bands = jnp.linspace(1e-4, FILTER_BANDS - 1, FILTER_BANDS, dtype=jnp.float32)
    ang = (2.0 * math.pi / seq) * jnp.arange(seq, dtype=jnp.float32)[:, None] * bands[None, :]
    z = jnp.concatenate([t, jnp.cos(ang), -jnp.sin(ang)], axis=-1)
    h = jnp.sin(freq * (z @ w1 + b1))
    h = jnp.sin(freq * (h @ w2 + b2))
    h = jnp.sin(freq * (h @ w3 + b3))
    h = (h @ w_o).astype(jnp.float32).reshape(seq, HYENA_ORDER, 2, D_B)
    deltas = jnp.abs(jnp.linspace(MIN_DECAY, MAX_DECAY, D_B, dtype=jnp.float32))
    h = h * jnp.exp(-t * deltas)[:, None, None, :]
    fwd, bwd = h[:, :, 0], h[:, :, 1]
    k = jnp.concatenate([fwd, jnp.zeros_like(fwd[:1]), bwd[:0:-1]], axis=0)
    return jnp.fft.rfft(k, axis=0)


def _long_conv(u, k_f, skip):
    seq = u.shape[1]
    uf32 = u.astype(jnp.float32)
    y = jnp.fft.irfft(jnp.fft.rfft(uf32, n=2 * seq, axis=1) * k_f[None], n=2 * seq, axis=1)[:, :seq]
    return (y + uf32 * skip.astype(jnp.float32)).astype(u.dtype)


def _hyena(p, k_f, skip):
    parts = jnp.split(p, N_PROJ_B, axis=-1)
    z = parts[0]
    for n in range(HYENA_ORDER):
        z = parts[n + 1] * _long_conv(z, k_f[:, n], skip[n])
    return z


def _token_mix(h, w_in, ln_g, ln_b, w_s, b_s, conv_w, conv_b, fw1, fb1, fw2, fb2, fw3, fb3,
               ffreq, fwo, skip, w_out):
    p = h @ w_in
    u_a, v_a = jnp.split(jax.nn.gelu(p[..., :2 * D_A], approximate=False), 2, axis=-1)
    y_a = _spatial_gating(u_a, v_a, ln_g, ln_b, w_s, b_s)
    p_b = _short_conv(p[..., 2 * D_A:], conv_w, conv_b)
    k_f = _hyena_filter_spectrum(h.shape[1], fw1, fb1, fw2, fb2, fw3, fb3, ffreq, fwo)
    y_b = _hyena(p_b, k_f, skip)
    return jnp.concatenate([y_a, y_b], axis=-1) @ w_out


def setup_inputs(seed: int = 0) -> dict:
    key = jax.random.key(seed)
    ks = iter(jax.random.split(key, 40))

    def nrm(shape, scale):
        return jax.random.normal(next(ks), shape, jnp.float32) * scale

    def gain(shape):
        return 1.0 + nrm(shape, 0.05)

    L = DEPTH
    return {
        'x': nrm((BATCH, SEQ, D_MODEL), 1.0),
        'ffn1_pre_g': gain((L, D_MODEL)),
        'ffn1_w_gate': nrm((L, D_MODEL, D_FF), D_MODEL ** -0.5),
        'ffn1_w_up': nrm((L, D_MODEL, D_FF), D_MODEL ** -0.5),
        'ffn1_w_down': nrm((L, D_FF, D_MODEL), D_FF ** -0.5),
        'ffn1_post_g': gain((L, D_MODEL)),
        'mix_pre_g': gain((L, D_MODEL)),
        'mix_w_in': nrm((L, D_MODEL, IN_COLS), D_MODEL ** -0.5),
        'gmlp_ln_g': gain((L, D_A)),
        'gmlp_ln_b': nrm((L, D_A), 0.02),
        'gmlp_w_s': nrm((L, N_HEADS_A, CHUNK, CHUNK), CHUNK ** -0.5),
        'gmlp_b_s': gain((L, N_HEADS_A, CHUNK)),
        'hy_conv_w': nrm((L, SHORT_K, N_PROJ_B * D_B), SHORT_K ** -0.5),
        'hy_conv_b': nrm((L, N_PROJ_B * D_B), 0.02),
        'hy_filt_w1': nrm((L, FILTER_EMB, FILTER_WIDTH), FILTER_EMB ** -0.5),
        'hy_filt_b1': nrm((L, FILTER_WIDTH), 0.1),
        'hy_filt_w2': nrm((L, FILTER_WIDTH, FILTER_WIDTH), FILTER_WIDTH ** -0.5),
        'hy_filt_b2': nrm((L, FILTER_WIDTH), 0.1),
        'hy_filt_w3': nrm((L, FILTER_WIDTH, FILTER_WIDTH), FILTER_WIDTH ** -0.5),
        'hy_filt_b3': nrm((L, FILTER_WIDTH), 0.1),
        'hy_filt_freq': gain((L, FILTER_WIDTH)),
        'hy_filt_w_out': nrm((L, FILTER_WIDTH, HYENA_ORDER * 2 * D_B), FILTER_OUT_SCALE * FILTER_WIDTH ** -0.5),
        'hy_skip': nrm((L, HYENA_ORDER, D_B), 0.5),
        'mix_w_out': nrm((L, D_MIX, D_MODEL), D_MIX ** -0.5),
        'mix_post_g': gain((L, D_MODEL)),
        'ffn2_pre_g': gain((L, D_MODEL)),
        'ffn2_w_gate': nrm((L, D_MODEL, D_FF), D_MODEL ** -0.5),
        'ffn2_w_up': nrm((L, D_MODEL, D_FF), D_MODEL ** -0.5),
        'ffn2_w_down': nrm((L, D_FF, D_MODEL), D_FF ** -0.5),
        'ffn2_post_g': gain((L, D_MODEL)),
    }


def reference(x, ffn1_pre_g, ffn1_w_gate, ffn1_w_up, ffn1_w_down, ffn1_post_g,
              mix_pre_g, mix_w_in, gmlp_ln_g, gmlp_ln_b, gmlp_w_s, gmlp_b_s,
              hy_conv_w, hy_conv_b, hy_filt_w1, hy_filt_b1, hy_filt_w2, hy_filt_b2,
              hy_filt_w3, hy_filt_b3, hy_filt_freq, hy_filt_w_out, hy_skip,
              mix_w_out, mix_post_g,
              ffn2_pre_g, ffn2_w_gate, ffn2_w_up, ffn2_w_down, ffn2_post_g):
    for l in range(DEPTH):
        h = _rms_norm(x, ffn1_pre_g[l])
        x = x + HALF_STEP * _rms_norm(_swiglu(h, ffn1_w_gate[l], ffn1_w_up[l], ffn1_w_down[l]), ffn1_post_g[l])
        h = _rms_norm(x, mix_pre_g[l])
        y = _token_mix(h, mix_w_in[l], gmlp_ln_g[l], gmlp_ln_b[l], gmlp_w_s[l], gmlp_b_s[l],
                       hy_conv_w[l], hy_conv_b[l], hy_filt_w1[l], hy_filt_b1[l],
                       hy_filt_w2[l], hy_filt_b2[l], hy_filt_w3[l], hy_filt_b3[l],
                       hy_filt_freq[l], hy_filt_w_out[l], hy_skip[l], mix_w_out[l])
        x = x + _rms_norm(y, mix_post_g[l])
        h = _rms_norm(x, ffn2_pre_g[l])
        x = x + HALF_STEP * _rms_norm(_swiglu(h, ffn2_w_gate[l], ffn2_w_up[l], ffn2_w_down[l]), ffn2_post_g[l])
    return x
```

```python
import functools
import math

import jax
import jax.numpy as jnp
from jax import lax
from jax.experimental import pallas as pl
from jax.experimental.pallas import tpu as pltpu

_CHUNK = 128
_HEADS_A = 8
_HEAD_DIM_A = 64
_ORDER = 2
_SHORT_K = 3
_FILTER_BANDS = 16
_FILTER_WIDTH = 64
_DECAY_TARGET = 1e-2
_MAX_DECAY = math.log(_DECAY_TARGET) / 0.3
_MIN_DECAY = math.log(_DECAY_TARGET) / 1.5
_HALF_STEP = 0.5
_RMS_EPS = 1e-6
_LN_EPS = 1e-5

_V7X_LANES = 128
_V7X_BF16_SUBLANES = 16
_V7X_MXU_DIM = 256
_V7X_VMEM_LIMIT_BYTES = 56 * 1024 * 1024

_NP = 8
_HALF = 272
_SQRT_HALF = 0.7071067811865476


def _cparams(semantics):
    return pltpu.CompilerParams(dimension_semantics=semantics, vmem_limit_bytes=_V7X_VMEM_LIMIT_BYTES)


def _rms_norm(x, g):
    return x * lax.rsqrt(jnp.mean(x * x, axis=-1, keepdims=True) + _RMS_EPS) * g


def _ffn_kernel(x_ref, gpre_ref, wgu_ref, wd_ref, gpost_ref, o_ref, *, n_chunks, chunk):
    x = x_ref[...]
    h = _rms_norm(x, gpre_ref[...]).astype(jnp.bfloat16)
    acc = jnp.zeros(x.shape, jnp.float32)
    for c in range(n_chunks):
        gu = jnp.dot(h, wgu_ref[c], preferred_element_type=jnp.float32)
        g = gu[:, :chunk]
        a = (g * jax.nn.sigmoid(g) * gu[:, chunk:]).astype(jnp.bfloat16)
        acc = acc + jnp.dot(a, wd_ref[c], preferred_element_type=jnp.float32)
    o_ref[...] = x + _HALF_STEP * _rms_norm(acc, gpost_ref[...])


def _ffn(x, g_pre, w_gate, w_up, w_down, g_post, *, tm, chunk):
    m, d = x.shape
    dff = w_gate.shape[1]
    n_chunks = dff // chunk
    wgu = jnp.concatenate(
        [w_gate.reshape(d, n_chunks, chunk), w_up.reshape(d, n_chunks, chunk)], axis=2
    ).transpose(1, 0, 2).astype(jnp.bfloat16)
    wd = w_down.reshape(n_chunks, chunk, d).astype(jnp.bfloat16)
    const = lambda i: (0, 0, 0)
    return pl.pallas_call(
        functools.partial(_ffn_kernel, n_chunks=n_chunks, chunk=chunk),
        grid=(m // tm,),
        in_specs=[
            pl.BlockSpec((tm, d), lambda i: (i, 0)),
            pl.BlockSpec((1, d), lambda i: (0, 0)),
            pl.BlockSpec((n_chunks, d, 2 * chunk), const, pipeline_mode=pl.Buffered(1)),
            pl.BlockSpec((n_chunks, chunk, d), const, pipeline_mode=pl.Buffered(1)),
            pl.BlockSpec((1, d), lambda i: (0, 0)),
        ],
        out_specs=pl.BlockSpec((tm, d), lambda i: (i, 0)),
        out_shape=jax.ShapeDtypeStruct((m, d), jnp.float32),
        compiler_params=_cparams(("arbitrary",)),
        name="ffn",
    )(x, g_pre.reshape(1, d), wgu, wd, g_post.reshape(1, d))


def _gmlp_kernel(x_ref, gpre_ref, w_ref, lng_ref, lnb_ref, ws_ref, bs_ref, o_ref, *, jb, da):
    rows = _CHUNK // _NP
    d = x_ref.shape[-1]
    x = x_ref[0].reshape(_NP * jb * rows, d)
    h = _rms_norm(x, gpre_ref[...]).astype(jnp.bfloat16)
    p = jnp.dot(h, w_ref[...], preferred_element_type=jnp.float32)
    p = 0.5 * p * (1.0 + lax.erf(p * _SQRT_HALF))
    u = p[:, :da]
    v = p[:, da:]
    mu = jnp.mean(v, axis=-1, keepdims=True)
    vc = v - mu
    var = jnp.mean(vc * vc, axis=-1, keepdims=True)
    vn = (vc * lax.rsqrt(var + _LN_EPS) * lng_ref[...] + lnb_ref[...]).astype(jnp.bfloat16)
    lane = lax.broadcasted_iota(jnp.int32, (_CHUNK, _V7X_LANES), 1)
    first_head = lane < _HEAD_DIM_A
    n_pairs = da // _V7X_LANES
    for j in range(jb):
        starts = [(n2 * jb + j) * rows for n2 in range(_NP)]
        vchunk = jnp.concatenate([vn[s:s + rows] for s in starts], axis=0)
        mixed = []
        for q in range(n_pairs):
            r = jnp.dot(ws_ref[q], vchunk[:, q * _V7X_LANES:(q + 1) * _V7X_LANES],
                        preferred_element_type=jnp.float32)
            mixed.append(jnp.where(first_head, r[:_CHUNK], r[_CHUNK:]))
        mixed = jnp.concatenate(mixed, axis=1) + bs_ref[...]
        for n2 in range(_NP):
            s = starts[n2]
            o_ref[0, n2, j] = u[s:s + rows] * mixed[n2 * rows:(n2 + 1) * rows]


def _gmlp(xp, g_pre, w_a, ln_g, ln_b, w_s, b_s, *, batch, seq, jb):
    m, d = xp.shape
    da = ln_g.shape[0]
    rows = _CHUNK // _NP
    n_chunks = seq // _CHUNK
    x5 = xp.reshape(batch, _NP, n_chunks, rows, d)
    ws_p = w_s.reshape(_HEADS_A, rows, _NP, rows, _NP).transpose(0, 2, 1, 4, 3).reshape(_HEADS_A, _CHUNK, _CHUNK)
    ws_pairs = ws_p.reshape(_HEADS_A // 2, 2 * _CHUNK, _CHUNK).astype(jnp.bfloat16)
    bs_p = b_s.reshape(_HEADS_A, rows, _NP).transpose(0, 2, 1).reshape(_HEADS_A, _CHUNK)
    bs_full = jnp.repeat(bs_p.T, _HEAD_DIM_A, axis=1)
    out = pl.pallas_call(
        functools.partial(_gmlp_kernel, jb=jb, da=da),
        grid=(batch, n_chunks // jb),
        in_specs=[
            pl.BlockSpec((1, _NP, jb, rows, d), lambda b, j: (b, 0, j, 0, 0)),
            pl.BlockSpec((1, d), lambda b, j: (0, 0)),
            pl.BlockSpec((d, 2 * da), lambda b, j: (0, 0)),
            pl.BlockSpec((1, da), lambda b, j: (0, 0)),
            pl.BlockSpec((1, da), lambda b, j: (0, 0)),
            pl.BlockSpec((_HEADS_A // 2, 2 * _CHUNK, _CHUNK), lambda b, j: (0, 0, 0)),
            pl.BlockSpec((_CHUNK, da), lambda b, j: (0, 0)),
        ],
        out_specs=pl.BlockSpec((1, _NP, jb, rows, da), lambda b, j: (b, 0, j, 0, 0)),
        out_shape=jax.ShapeDtypeStruct((batch, _NP, n_chunks, rows, da), jnp.float32),
        compiler_params=_cparams(("arbitrary", "arbitrary")),
        name="gmlp",
    )(x5, g_pre.reshape(1, d), w_a.astype(jnp.bfloat16), ln_g.reshape(1, da), ln_b.reshape(1, da),
      ws_pairs, bs_full)
    return out.reshape(m, da)


def _hyproj_kernel(x_ref, gpre_ref, w_ref, cw_ref, cb_ref, o_ref, h_scr, *, n1):
    @pl.when(pl.program_id(1) == 0)
    def _():
        h_scr[...] = _rms_norm(x_ref[0], gpre_ref[...]).astype(jnp.bfloat16)

    p = jnp.dot(h_scr[...], w_ref[0], preferred_element_type=jnp.float32)
    blocks = [p[k * n1:(k + 1) * n1] for k in range(_NP)]
    row = lax.broadcasted_iota(jnp.int32, blocks[0].shape, 0)
    before_first = jnp.where(row == 0, 0.0, pltpu.roll(blocks[_NP - 1], 1, axis=0))
    after_last = jnp.where(row == n1 - 1, 0.0, pltpu.roll(blocks[0], n1 - 1, axis=0))
    w0 = cw_ref[0, 0:1]
    w1 = cw_ref[0, 1:2]
    w2 = cw_ref[0, 2:3]
    bias = cb_ref[0]
    for k in range(_NP):
        prev = blocks[k - 1] if k > 0 else before_first
        nxt = blocks[k + 1] if k < _NP - 1 else after_last
        o_ref[0, k * n1:(k + 1) * n1] = ((bias + prev * w0) + blocks[k] * w1) + nxt * w2


def _hyproj(xp, g_pre, w_b, conv_w, conv_b, *, batch, seq):
    m, d = xp.shape
    n_proj = _ORDER + 1
    c = w_b.shape[1] // n_proj
    w3 = w_b.reshape(d, n_proj, c).transpose(1, 0, 2).astype(jnp.bfloat16)
    cw = conv_w.reshape(_SHORT_K, n_proj, c).transpose(1, 0, 2)
    cb = conv_b.reshape(n_proj, 1, c)
    return pl.pallas_call(
        functools.partial(_hyproj_kernel, n1=seq // _NP),
        grid=(batch, n_proj),
        in_specs=[
            pl.BlockSpec((1, seq, d), lambda b, j: (b, 0, 0)),
            pl.BlockSpec((1, d), lambda b, j: (0, 0)),
            pl.BlockSpec((1, d, c), lambda b, j: (j, 0, 0)),
            pl.BlockSpec((1, _SHORT_K, c), lambda b, j: (j, 0, 0)),
            pl.BlockSpec((1, 1, c), lambda b, j: (j, 0, 0)),
        ],
        out_specs=pl.BlockSpec((1, seq, c), lambda b, j: (j, b, 0)),
        out_shape=jax.ShapeDtypeStruct((n_proj, m, c), jnp.float32),
        scratch_shapes=[pltpu.VMEM((seq, d), jnp.bfloat16)],
        compiler_params=_cparams(("arbitrary", "arbitrary")),
        name="hyproj",
    )(xp.reshape(batch, seq, d), g_pre.reshape(1, d), w3, cw, cb)


def _dft_tables(seq):
    n1 = seq // _NP
    n_fft = 2 * seq
    k1 = lax.broadcasted_iota(jnp.int32, (_NP, _HALF, n1), 1)
    t = _NP * lax.broadcasted_iota(jnp.int32, (_NP, _HALF, n1), 2) + lax.broadcasted_iota(jnp.int32, (_NP, _HALF, n1), 0)
    theta = ((k1 * t) % n_fft).astype(jnp.float32) * (2.0 * math.pi / n_fft)
    valid = k1 <= n1
    cos = jnp.where(valid, jnp.cos(theta), 0.0)
    sin = jnp.where(valid, jnp.sin(theta), 0.0)
    fwd = jnp.concatenate([cos, -sin], axis=1)
    weight = jnp.where((k1 == 0) | (k1 == n1), 1.0, 2.0) / n_fft
    inv = jnp.concatenate([weight * cos, -weight * sin], axis=1).transpose(0, 2, 1)
    return fwd, inv


def _fft4(cr, ci):
    d0r, d0i = cr[0] + cr[2], ci[0] + ci[2]
    d1r, d1i = cr[1] + cr[3], ci[1] + ci[3]
    d2r, d2i = cr[0] - cr[2], ci[0] - ci[2]
    er, ei = cr[1] - cr[3], ci[1] - ci[3]
    d3r, d3i = ei, -er
    return [(d0r + d1r, d0i + d1i), (d2r + d3r, d2i + d3i), (d0r - d1r, d0i - d1i), (d2r - d3r, d2i - d3i)]


def _fft8(re, im):
    br, bi = [None] * 8, [None] * 8
    for j in range(4):
        br[j], bi[j] = re[j] + re[j + 4], im[j] + im[j + 4]
        dr, di = re[j] - re[j + 4], im[j] - im[j + 4]
        if j == 0:
            br[4], bi[4] = dr, di
        elif j == 1:
            br[5], bi[5] = (dr + di) * _SQRT_HALF, (di - dr) * _SQRT_HALF
        elif j == 2:
            br[6], bi[6] = di, -dr
        else:
            br[7], bi[7] = (di - dr) * _SQRT_HALF, -(dr + di) * _SQRT_HALF
    even = _fft4(br[:4], bi[:4])
    odd = _fft4(br[4:], bi[4:])
    out = [None] * 8
    for k in range(4):
        out[2 * k], out[2 * k + 1] = even[k], odd[k]
    return [o[0] for o in out], [o[1] for o in out]


def _ifft8(re, im):
    o_im, o_re = _fft8(im, re)
    return o_re, o_im


def _filter_kernel(w1_ref, b1_ref, w2_ref, b2_ref, w3_ref, b3_ref, fr_ref, wof_ref, wob_ref, dl_ref,
                   fwd_ref, o_ref, af_scr, ab_scr, *, seq):
    n1 = seq // _NP
    hp = lax.Precision.HIGHEST
    r = lax.broadcasted_iota(jnp.int32, (seq, _V7X_LANES), 0)
    pos = (_NP * (r % n1) + r // n1).astype(jnp.float32)
    tn = pos * (1.0 / (seq - 1))
    lane = lax.broadcasted_iota(jnp.int32, (seq, _V7X_LANES), 1)
    is_cos = (lane >= 1) & (lane <= _FILTER_BANDS)
    is_sin = (lane > _FILTER_BANDS) & (lane <= 2 * _FILTER_BANDS)
    band_idx = jnp.where(is_cos, lane - 1, lane - 1 - _FILTER_BANDS).astype(jnp.float32)
    band = 1e-4 + band_idx * ((_FILTER_BANDS - 1 - 1e-4) / (_FILTER_BANDS - 1))
    ang = (2.0 * math.pi / seq) * pos * band
    z = jnp.where(lane == 0, tn, jnp.where(is_cos, jnp.cos(ang), jnp.where(is_sin, -jnp.sin(ang), 0.0)))
    freq = fr_ref[0]
    h = jnp.sin(freq * (jnp.dot(z, w1_ref[0], precision=hp, preferred_element_type=jnp.float32) + b1_ref[0]))
    h = jnp.sin(freq * (jnp.dot(h, w2_ref[0], precision=hp, preferred_element_type=jnp.float32) + b2_ref[0]))
    h = jnp.sin(freq * (jnp.dot(h, w3_ref[0], precision=hp, preferred_element_type=jnp.float32) + b3_ref[0]))
    decay = jnp.exp(-tn[:, :1] * dl_ref[...])
    hf = jnp.dot(h, wof_ref[0, 0, 0], precision=hp, preferred_element_type=jnp.float32) * decay
    hb = jnp.dot(h, wob_ref[0, 0, 0], precision=hp, preferred_element_type=jnp.float32) * decay
    hb = jnp.where(pos[:, :1] == 0.0, 0.0, hb)
    for n2 in range(_NP):
        af_scr[n2] = jnp.dot(fwd_ref[n2], hf[n2 * n1:(n2 + 1) * n1], precision=hp, preferred_element_type=jnp.float32)
        ab_scr[n2] = jnp.dot(fwd_ref[n2], hb[n2 * n1:(n2 + 1) * n1], precision=hp, preferred_element_type=jnp.float32)

    rc = _V7X_BF16_SUBLANES

    def body(i, carry):
        r0 = pl.multiple_of(i * rc, rc)
        fr, fi = _fft8([af_scr[n2, pl.ds(r0, rc)] for n2 in range(_NP)],
                       [af_scr[n2, pl.ds(_HALF + r0, rc)] for n2 in range(_NP)])
        gr, gi = _fft8([ab_scr[n2, pl.ds(r0, rc)] for n2 in range(_NP)],
                       [ab_scr[n2, pl.ds(_HALF + r0, rc)] for n2 in range(_NP)])
        for k2 in range(_NP):
            o_ref[0, 0, k2, 0, pl.ds(r0, rc)] = fr[k2] + gr[k2]
            o_ref[0, 0, k2, 1, pl.ds(r0, rc)] = fi[k2] - gi[k2]
        return carry

    lax.fori_loop(0, _HALF // rc, body, 0)


def _filter_spectra(w1, b1, w2, b2, w3, b3, freq, w_out, fwd_tab, *, seq, ct):
    n_layers = w1.shape[0]
    c = w_out.shape[2] // (2 * _ORDER)
    fw = _FILTER_WIDTH
    w1p = jnp.pad(w1, ((0, 0), (0, _V7X_LANES - w1.shape[1]), (0, 0)))
    wo = w_out.reshape(n_layers, fw, _ORDER, 2, c).transpose(0, 2, 3, 1, 4)
    deltas = jnp.abs(jnp.linspace(_MIN_DECAY, _MAX_DECAY, c, dtype=jnp.float32)).reshape(1, c)
    vec = lambda a: a.reshape(n_layers, 1, fw)
    lmap3 = lambda l, o, j: (l, 0, 0)
    return pl.pallas_call(
        functools.partial(_filter_kernel, seq=seq),
        grid=(n_layers, _ORDER, c // ct),
        in_specs=[
            pl.BlockSpec((1, _V7X_LANES, fw), lmap3), pl.BlockSpec((1, 1, fw), lmap3),
            pl.BlockSpec((1, fw, fw), lmap3), pl.BlockSpec((1, 1, fw), lmap3),
            pl.BlockSpec((1, fw, fw), lmap3), pl.BlockSpec((1, 1, fw), lmap3),
            pl.BlockSpec((1, 1, fw), lmap3),
            pl.BlockSpec((1, 1, 1, fw, ct), lambda l, o, j: (l, o, 0, 0, j)),
            pl.BlockSpec((1, 1, 1, fw, ct), lambda l, o, j: (l, o, 1, 0, j)),
            pl.BlockSpec((1, ct), lambda l, o, j: (0, j)),
            pl.BlockSpec((_NP, 2 * _HALF, seq // _NP), lambda l, o, j: (0, 0, 0)),
        ],
        out_specs=pl.BlockSpec((1, 1, _NP, 2, _HALF, ct), lambda l, o, j: (l, o, 0, 0, 0, j)),
        out_shape=jax.ShapeDtypeStruct((n_layers, _ORDER, _NP, 2, _HALF, c), jnp.float32),
        scratch_shapes=[pltpu.VMEM((_NP, 2 * _HALF, ct), jnp.float32),
                        pltpu.VMEM((_NP, 2 * _HALF, ct), jnp.float32)],
        compiler_params=_cparams(("arbitrary", "arbitrary", "arbitrary")),
        name="filter_spectra",
    )(w1p, vec(b1), w2, vec(b2), w3, vec(b3), vec(freq), wo, wo, deltas, fwd_tab)


def _hyena_kernel(v_ref, x1_ref, x2_ref, kf_ref, skip_ref, fwd_ref, inv_ref, o_ref, a_scr, c_scr, z_scr, *, seq):
    n1 = seq // _NP
    rc = _V7X_BF16_SUBLANES
    ct = o_ref.shape[-1]

    def conv_order(order, z_in_ref, gate_ref, z_out_ref):
        for n2 in range(_NP):
            zb = z_in_ref[0, n2 * n1:(n2 + 1) * n1].astype(jnp.bfloat16)
            a_scr[n2] = jnp.dot(fwd_ref[n2], zb, preferred_element_type=jnp.float32)

        def body(i, carry):
            r0 = pl.multiple_of(i * rc, rc)
            for l0 in range(0, ct, _V7X_LANES):
                ls = slice(l0, l0 + _V7X_LANES)
                xr, xi = _fft8([a_scr[n2, pl.ds(r0, rc), ls] for n2 in range(_NP)],
                               [a_scr[n2, pl.ds(_HALF + r0, rc), ls] for n2 in range(_NP)])
                yr, yi = [], []
                for k2 in range(_NP):
                    kr = kf_ref[order, k2, 0, pl.ds(r0, rc), ls]
                    ki = kf_ref[order, k2, 1, pl.ds(r0, rc), ls]
                    yr.append(xr[k2] * kr - xi[k2] * ki)
                    yi.append(xr[k2] * ki + xi[k2] * kr)
                cr, ci = _ifft8(yr, yi)
                for t2 in range(_NP):
                    c_scr[t2, pl.ds(r0, rc), ls] = cr[t2].astype(jnp.bfloat16)
                    c_scr[t2, pl.ds(_HALF + r0, rc), ls] = ci[t2].astype(jnp.bfloat16)
            return carry

        lax.fori_loop(0, _HALF // rc, body, 0)

        skip = skip_ref[order:order + 1]
        for t2 in range(_NP):
            rows = slice(t2 * n1, (t2 + 1) * n1)
            y = jnp.dot(inv_ref[t2], c_scr[t2], preferred_element_type=jnp.float32)
            z = z_in_ref[0, rows]
            z_out_ref[0, rows] = gate_ref[0, rows] * (y + z * skip)

    conv_order(0, v_ref, x1_ref, z_scr)
    conv_order(1, z_scr, x2_ref, o_ref)


def _hyena(proj, kf, skip, fwd_tab, inv_tab, *, batch, seq, ct):
    n_proj, m, c = proj.shape
    n1 = seq // _NP
    proj4 = proj.reshape(n_proj, batch, seq, c)
    pspec = lambda j: pl.BlockSpec((None, 1, seq, ct), lambda t, b, j=j: (j, b, 0, t))
    resident = dict(pipeline_mode=pl.Buffered(1))
    out = pl.pallas_call(
        functools.partial(_hyena_kernel, seq=seq),
        grid=(c // ct, batch),
        in_specs=[
            pspec(0), pspec(1), pspec(2),
            pl.BlockSpec((_ORDER, _NP, 2, _HALF, ct), lambda t, b: (0, 0, 0, 0, t)),
            pl.BlockSpec((_ORDER, ct), lambda t, b: (0, t)),
            pl.BlockSpec((_NP, 2 * _HALF, n1), lambda t, b: (0, 0, 0), **resident),
            pl.BlockSpec((_NP, n1, 2 * _HALF), lambda t, b: (0, 0, 0), **resident),
        ],
        out_specs=pl.BlockSpec((1, seq, ct), lambda t, b: (b, 0, t)),
        out_shape=jax.ShapeDtypeStruct((batch, seq, c), jnp.float32),
        scratch_shapes=[pltpu.VMEM((_NP, 2 * _HALF, ct), jnp.float32),
                        pltpu.VMEM((_NP, 2 * _HALF, ct), jnp.bfloat16),
                        pltpu.VMEM((1, seq, ct), jnp.float32)],
        compiler_params=_cparams(("arbitrary", "arbitrary")),
        name="hyena",
    )(proj4, proj4, proj4, kf, skip, fwd_tab.astype(jnp.bfloat16), inv_tab.astype(jnp.bfloat16))
    return out.reshape(m, c)


def _mixout_kernel(x_ref, ya_ref, yb_ref, wa_ref, wb_ref, g_ref, o_ref):
    y = jnp.dot(ya_ref[...].astype(jnp.bfloat16), wa_ref[...], preferred_element_type=jnp.float32)
    y = y + jnp.dot(yb_ref[...].astype(jnp.bfloat16), wb_ref[...], preferred_element_type=jnp.float32)
    o_ref[...] = x_ref[...] + _rms_norm(y, g_ref[...])


def _mixout(x, ya, yb, w_out, g_post, *, tm):
    m, d = x.shape
    da = ya.shape[1]
    db = yb.shape[1]
    wa = w_out[:da].astype(jnp.bfloat16)
    wb = w_out[da:].astype(jnp.bfloat16)
    return pl.pallas_call(
        _mixout_kernel,
        grid=(m // tm,),
        in_specs=[
            pl.BlockSpec((tm, d), lambda i: (i, 0)),
            pl.BlockSpec((tm, da), lambda i: (i, 0)),
            pl.BlockSpec((tm, db), lambda i: (i, 0)),
            pl.BlockSpec((da, d), lambda i: (0, 0)),
            pl.BlockSpec((db, d), lambda i: (0, 0)),
            pl.BlockSpec((1, d), lambda i: (0, 0)),
        ],
        out_specs=pl.BlockSpec((tm, d), lambda i: (i, 0)),
        out_shape=jax.ShapeDtypeStruct((m, d), jnp.float32),
        compiler_params=_cparams(("arbitrary",)),
        name="mixout",
    )(x, ya, yb, wa, wb, g_post.reshape(1, d))


def kernel(x, ffn1_pre_g, ffn1_w_gate, ffn1_w_up, ffn1_w_down, ffn1_post_g, mix_pre_g, mix_w_in, gmlp_ln_g, gmlp_ln_b, gmlp_w_s, gmlp_b_s, hy_conv_w, hy_conv_b, hy_filt_w1, hy_filt_b1, hy_filt_w2, hy_filt_b2, hy_filt_w3, hy_filt_b3, hy_filt_freq, hy_filt_w_out, hy_skip, mix_w_out, mix_post_g, ffn2_pre_g, ffn2_w_gate, ffn2_w_up, ffn2_w_down, ffn2_post_g):
    batch, seq, d = x.shape
    depth = ffn1_pre_g.shape[0]
    da = gmlp_ln_g.shape[1]
    assert seq % (_NP * _CHUNK) == 0 and seq // _NP + 1 <= _HALF
    m = batch * seq
    ffn_tiles = dict(tm=512, chunk=_V7X_MXU_DIM)
    ct = _V7X_MXU_DIM

    xp = x.reshape(batch, seq // _NP, _NP, d).transpose(0, 2, 1, 3).reshape(m, d)

    fwd_tab, inv_tab = _dft_tables(seq)
    kf_all = _filter_spectra(hy_filt_w1, hy_filt_b1, hy_filt_w2, hy_filt_b2, hy_filt_w3, hy_filt_b3,
                             hy_filt_freq, hy_filt_w_out, fwd_tab, seq=seq, ct=ct)

    for l in range(depth):
        xp = _ffn(xp, ffn1_pre_g[l], ffn1_w_gate[l], ffn1_w_up[l], ffn1_w_down[l], ffn1_post_g[l], **ffn_tiles)
        ya = _gmlp(xp, mix_pre_g[l], mix_w_in[l][:, :2 * da], gmlp_ln_g[l], gmlp_ln_b[l], gmlp_w_s[l], gmlp_b_s[l],
                   batch=batch, seq=seq, jb=4)
        proj = _hyproj(xp, mix_pre_g[l], mix_w_in[l][:, 2 * da:], hy_conv_w[l], hy_conv_b[l], batch=batch, seq=seq)
        yb = _hyena(proj, kf_all[l], hy_skip[l], fwd_tab, inv_tab, batch=batch, seq=seq, ct=ct)
        xp = _mixout(xp, ya, yb, mix_w_out[l], mix_post_g[l], tm=1024)
        xp = _ffn(xp, ffn2_pre_g[l], ffn2_w_gate[l], ffn2_w_up[l], ffn2_w_down[l], ffn2_post_g[l], **ffn_tiles)

    return xp.reshape(batch, _NP, seq // _NP, d).transpose(0, 2, 1, 3).reshape(batch, seq, d)
```

```python
import functools
import math

import jax
import jax.numpy as jnp
import numpy as np
from jax import lax
from jax.experimental import pallas as pl
from jax.experimental.pallas import tpu as pltpu

_CHUNK = 128
_HEADS_A = 8
_HEAD_DIM_A = 64
_ORDER = 2
_SHORT_K = 3
_FILTER_BANDS = 16
_FILTER_WIDTH = 64
_DECAY_TARGET = 1e-2
_MAX_DECAY = math.log(_DECAY_TARGET) / 0.3
_MIN_DECAY = math.log(_DECAY_TARGET) / 1.5
_HALF_STEP = 0.5
_RMS_EPS = 1e-6
_LN_EPS = 1e-5

_V7X_LANES = 128
_V7X_BF16_SUBLANES = 16
_V7X_MXU_DIM = 256
_V7X_VMEM_LIMIT_BYTES = 56 * 1024 * 1024

_NP = 8
_HALF = 272
_SQRT_HALF = 0.7071067811865476


def _cparams(semantics):
    return pltpu.CompilerParams(dimension_semantics=semantics, vmem_limit_bytes=_V7X_VMEM_LIMIT_BYTES)


def _rms_norm(x, g):
    return x * lax.rsqrt(jnp.mean(x * x, axis=-1, keepdims=True) + _RMS_EPS) * g


def _ffn_kernel(x_ref, gpre_ref, wg_ref, wu_ref, wd_ref, gpost_ref, o_ref, *, chunk):
    x = x_ref[...]
    h = _rms_norm(x, gpre_ref[...]).astype(jnp.bfloat16)
    acc = jnp.zeros(x.shape, jnp.float32)
    for c0 in range(0, wg_ref.shape[1], chunk):
        g = jnp.dot(h, wg_ref[:, c0:c0 + chunk], preferred_element_type=jnp.float32)
        u = jnp.dot(h, wu_ref[:, c0:c0 + chunk], preferred_element_type=jnp.float32)
        a = (g * jax.nn.sigmoid(g) * u).astype(jnp.bfloat16)
        acc = acc + jnp.dot(a, wd_ref[c0:c0 + chunk, :], preferred_element_type=jnp.float32)
    o_ref[...] = x + _HALF_STEP * _rms_norm(acc, gpost_ref[...])


def _ffn(x, g_pre, w_gate, w_up, w_down, g_post, *, tm, chunk):
    m, d = x.shape
    dff = w_gate.shape[1]
    assert dff % chunk == 0
    const = lambda i: (0, 0)
    resident = dict(pipeline_mode=pl.Buffered(1))
    return pl.pallas_call(
        functools.partial(_ffn_kernel, chunk=chunk),
        grid=(m // tm,),
        in_specs=[
            pl.BlockSpec((tm, d), lambda i: (i, 0)),
            pl.BlockSpec((1, d), const),
            pl.BlockSpec((d, dff), const, **resident),
            pl.BlockSpec((d, dff), const, **resident),
            pl.BlockSpec((dff, d), const, **resident),
            pl.BlockSpec((1, d), const),
        ],
        out_specs=pl.BlockSpec((tm, d), lambda i: (i, 0)),
        out_shape=jax.ShapeDtypeStruct((m, d), jnp.float32),
        compiler_params=_cparams(("arbitrary",)),
        name="ffn",
    )(x, g_pre.reshape(1, d), w_gate.astype(jnp.bfloat16), w_up.astype(jnp.bfloat16),
      w_down.astype(jnp.bfloat16), g_post.reshape(1, d))


def _gmlp_kernel(x_ref, gpre_ref, w_ref, lng_ref, lnb_ref, ws_ref, bs_ref, o_ref, *, jb, da):
    rows = _CHUNK // _NP
    d = x_ref.shape[-1]
    x = x_ref[0].reshape(_NP * jb * rows, d)
    h = _rms_norm(x, gpre_ref[...]).astype(jnp.bfloat16)
    p = jnp.dot(h, w_ref[...], preferred_element_type=jnp.float32)
    p = 0.5 * p * (1.0 + lax.erf(p * _SQRT_HALF))
    u = p[:, :da]
    v = p[:, da:]
    mu = jnp.mean(v, axis=-1, keepdims=True)
    vc = v - mu
    var = jnp.mean(vc * vc, axis=-1, keepdims=True)
    vn = (vc * lax.rsqrt(var + _LN_EPS) * lng_ref[...] + lnb_ref[...]).astype(jnp.bfloat16)
    lane = lax.broadcasted_iota(jnp.int32, (_CHUNK, _V7X_LANES), 1)
    first_head = lane < _HEAD_DIM_A
    n_pairs = da // _V7X_LANES
    for j in range(jb):
        starts = [(n2 * jb + j) * rows for n2 in range(_NP)]
        vchunk = jnp.concatenate([vn[s:s + rows] for s in starts], axis=0)
        mixed = []
        for q in range(n_pairs):
            r = jnp.dot(ws_ref[q], vchunk[:, q * _V7X_LANES:(q + 1) * _V7X_LANES],
                        preferred_element_type=jnp.float32)
            mixed.append(jnp.where(first_head, r[:_CHUNK], r[_CHUNK:]))
        mixed = jnp.concatenate(mixed, axis=1) + bs_ref[...]
        for n2 in range(_NP):
            s = starts[n2]
            o_ref[0, n2, j] = u[s:s + rows] * mixed[n2 * rows:(n2 + 1) * rows]


def _gmlp(xp, g_pre, w_in, ln_g, ln_b, w_s, b_s, *, batch, seq, jb):
    m, d = xp.shape
    da = ln_g.shape[0]
    rows = _CHUNK // _NP
    n_chunks = seq // _CHUNK
    x5 = xp.reshape(batch, _NP, n_chunks, rows, d)
    ws_p = w_s.reshape(_HEADS_A, rows, _NP, rows, _NP).transpose(0, 2, 1, 4, 3).reshape(_HEADS_A, _CHUNK, _CHUNK)
    ws_pairs = ws_p.reshape(_HEADS_A // 2, 2 * _CHUNK, _CHUNK).astype(jnp.bfloat16)
    bs_p = b_s.reshape(_HEADS_A, rows, _NP).transpose(0, 2, 1).reshape(_HEADS_A, _CHUNK)
    bs_full = jnp.repeat(bs_p.T, _HEAD_DIM_A, axis=1)
    out = pl.pallas_call(
        functools.partial(_gmlp_kernel, jb=jb, da=da),
        grid=(batch, n_chunks // jb),
        in_specs=[
            pl.BlockSpec((1, _NP, jb, rows, d), lambda b, j: (b, 0, j, 0, 0)),
            pl.BlockSpec((1, d), lambda b, j: (0, 0)),
            pl.BlockSpec((d, 2 * da), lambda b, j: (0, 0)),
            pl.BlockSpec((1, da), lambda b, j: (0, 0)),
            pl.BlockSpec((1, da), lambda b, j: (0, 0)),
            pl.BlockSpec((_HEADS_A // 2, 2 * _CHUNK, _CHUNK), lambda b, j: (0, 0, 0)),
            pl.BlockSpec((_CHUNK, da), lambda b, j: (0, 0)),
        ],
        out_specs=pl.BlockSpec((1, _NP, jb, rows, da), lambda b, j: (b, 0, j, 0, 0)),
        out_shape=jax.ShapeDtypeStruct((batch, _NP, n_chunks, rows, da), jnp.float32),
        compiler_params=_cparams(("arbitrary", "arbitrary")),
        name="gmlp",
    )(x5, g_pre.reshape(1, d), w_in, ln_g.reshape(1, da), ln_b.reshape(1, da), ws_pairs, bs_full)
    return out.reshape(m, da)


def _hyproj_kernel(x_ref, gpre_ref, w_ref, cw_ref, cb_ref, o_ref, h_scr, *, n1):
    @pl.when(pl.program_id(1) == 0)
    def _():
        h_scr[...] = _rms_norm(x_ref[0], gpre_ref[...]).astype(jnp.bfloat16)

    p = jnp.dot(h_scr[...], w_ref[...], preferred_element_type=jnp.float32)
    blocks = [p[k * n1:(k + 1) * n1] for k in range(_NP)]
    row = lax.broadcasted_iota(jnp.int32, blocks[0].shape, 0)
    before_first = jnp.where(row == 0, 0.0, pltpu.roll(blocks[_NP - 1], 1, axis=0))
    after_last = jnp.where(row == n1 - 1, 0.0, pltpu.roll(blocks[0], n1 - 1, axis=0))
    w0 = cw_ref[0:1]
    w1 = cw_ref[1:2]
    w2 = cw_ref[2:3]
    bias = cb_ref[...]
    for k in range(_NP):
        prev = blocks[k - 1] if k > 0 else before_first
        nxt = blocks[k + 1] if k < _NP - 1 else after_last
        o_ref[0, k * n1:(k + 1) * n1] = ((bias + prev * w0) + blocks[k] * w1) + nxt * w2


def _hyproj(xp, g_pre, w_in, conv_w, conv_b, *, batch, seq):
    m, d = xp.shape
    n_proj = _ORDER + 1
    c = conv_w.shape[1] // n_proj
    first = (w_in.shape[1] - n_proj * c) // c
    return pl.pallas_call(
        functools.partial(_hyproj_kernel, n1=seq // _NP),
        grid=(batch, n_proj),
        in_specs=[
            pl.BlockSpec((1, seq, d), lambda b, j: (b, 0, 0)),
            pl.BlockSpec((1, d), lambda b, j: (0, 0)),
            pl.BlockSpec((d, c), lambda b, j: (0, first + j)),
            pl.BlockSpec((_SHORT_K, c), lambda b, j: (0, j)),
            pl.BlockSpec((1, c), lambda b, j: (0, j)),
        ],
        out_specs=pl.BlockSpec((1, seq, c), lambda b, j: (j, b, 0)),
        out_shape=jax.ShapeDtypeStruct((n_proj, m, c), jnp.float32),
        scratch_shapes=[pltpu.VMEM((seq, d), jnp.bfloat16)],
        compiler_params=_cparams(("arbitrary", "arbitrary")),
        name="hyproj",
    )(xp.reshape(batch, seq, d), g_pre.reshape(1, d), w_in, conv_w, conv_b.reshape(1, n_proj * c))


@functools.lru_cache(maxsize=None)
def _dft_tables(seq):
    n1 = seq // _NP
    n_fft = 2 * seq
    k1 = np.arange(_HALF)[None, :, None]
    t = _NP * np.arange(n1)[None, None, :] + np.arange(_NP)[:, None, None]
    theta = ((k1 * t) % n_fft) * (2.0 * math.pi / n_fft)
    valid = k1 <= n1
    cos = np.where(valid, np.cos(theta), 0.0)
    sin = np.where(valid, np.sin(theta), 0.0)
    fwd = np.concatenate([cos, -sin], axis=1)
    weight = np.where((k1 == 0) | (k1 == n1), 1.0, 2.0) / n_fft
    inv = np.concatenate([weight * cos, -weight * sin], axis=1).transpose(0, 2, 1)
    return fwd.astype(np.float32), np.ascontiguousarray(inv).astype(np.float32)


def _fft4(cr, ci):
    d0r, d0i = cr[0] + cr[2], ci[0] + ci[2]
    d1r, d1i = cr[1] + cr[3], ci[1] + ci[3]
    d2r, d2i = cr[0] - cr[2], ci[0] - ci[2]
    er, ei = cr[1] - cr[3], ci[1] - ci[3]
    d3r, d3i = ei, -er
    return [(d0r + d1r, d0i + d1i), (d2r + d3r, d2i + d3i), (d0r - d1r, d0i - d1i), (d2r - d3r, d2i - d3i)]


def _fft8(re, im):
    br, bi = [None] * 8, [None] * 8
    for j in range(4):
        br[j], bi[j] = re[j] + re[j + 4], im[j] + im[j + 4]
        dr, di = re[j] - re[j + 4], im[j] - im[j + 4]
        if j == 0:
            br[4], bi[4] = dr, di
        elif j == 1:
            br[5], bi[5] = (dr + di) * _SQRT_HALF, (di - dr) * _SQRT_HALF
        elif j == 2:
            br[6], bi[6] = di, -dr
        else:
            br[7], bi[7] = (di - dr) * _SQRT_HALF, -(dr + di) * _SQRT_HALF
    even = _fft4(br[:4], bi[:4])
    odd = _fft4(br[4:], bi[4:])
    out = [None] * 8
    for k in range(4):
        out[2 * k], out[2 * k + 1] = even[k], odd[k]
    return [o[0] for o in out], [o[1] for o in out]


def _ifft8(re, im):
    o_im, o_re = _fft8(im, re)
    return o_re, o_im


def _filter_kernel(w1_ref, b1_ref, w2_ref, b2_ref, w3_ref, b3_ref, fr_ref, wof_ref, wob_ref, dl_ref,
                   fwd_ref, o_ref, h_scr, af_scr, ab_scr, *, seq):
    n1 = seq // _NP
    fw = _FILTER_WIDTH
    hp = lax.Precision.HIGHEST

    @pl.when(pl.program_id(1) == 0)
    def _():
        r = lax.broadcasted_iota(jnp.int32, (fw, seq), 1)
        pos = (_NP * (r % n1) + r // n1).astype(jnp.float32)
        row = lax.broadcasted_iota(jnp.int32, (fw, seq), 0)
        is_cos = (row >= 1) & (row <= _FILTER_BANDS)
        is_sin = (row > _FILTER_BANDS) & (row <= 2 * _FILTER_BANDS)
        band_idx = jnp.where(is_cos, row - 1, row - 1 - _FILTER_BANDS).astype(jnp.float32)
        band = 1e-4 + band_idx * ((_FILTER_BANDS - 1 - 1e-4) / (_FILTER_BANDS - 1))
        ang = (2.0 * math.pi / seq) * pos * band
        z = jnp.where(row == 0, pos * (1.0 / (seq - 1)),
                      jnp.where(is_cos, jnp.cos(ang), jnp.where(is_sin, -jnp.sin(ang), 0.0)))
        freq = fr_ref[0]
        h = jnp.sin(freq * (jnp.dot(w1_ref[0], z, precision=hp, preferred_element_type=jnp.float32) + b1_ref[0]))
        h = jnp.sin(freq * (jnp.dot(w2_ref[0], h, precision=hp, preferred_element_type=jnp.float32) + b2_ref[0]))
        h = jnp.sin(freq * (jnp.dot(w3_ref[0], h, precision=hp, preferred_element_type=jnp.float32) + b3_ref[0]))
        h_scr[...] = h.T

    r = lax.broadcasted_iota(jnp.int32, (seq, 1), 0)
    pos = (_NP * (r % n1) + r // n1).astype(jnp.float32)
    h = h_scr[...]
    decay = jnp.exp(-(pos * (1.0 / (seq - 1))) * dl_ref[...])
    hf = jnp.dot(h, wof_ref[0], precision=hp, preferred_element_type=jnp.float32) * decay
    hb = jnp.dot(h, wob_ref[0], precision=hp, preferred_element_type=jnp.float32) * decay
    hb = jnp.where(pos == 0.0, 0.0, hb)
    for n2 in range(_NP):
        af_scr[n2] = jnp.dot(fwd_ref[n2], hf[n2 * n1:(n2 + 1) * n1], precision=hp, preferred_element_type=jnp.float32)
        ab_scr[n2] = jnp.dot(fwd_ref[n2], hb[n2 * n1:(n2 + 1) * n1], precision=hp, preferred_element_type=jnp.float32)

    rc = _V7X_BF16_SUBLANES

    def body(i, carry):
        r0 = pl.multiple_of(i * rc, rc)
        fr, fi = _fft8([af_scr[n2, pl.ds(r0, rc)] for n2 in range(_NP)],
                       [af_scr[n2, pl.ds(_HALF + r0, rc)] for n2 in range(_NP)])
        gr, gi = _fft8([ab_scr[n2, pl.ds(r0, rc)] for n2 in range(_NP)],
                       [ab_scr[n2, pl.ds(_HALF + r0, rc)] for n2 in range(_NP)])
        for k2 in range(_NP):
            o_ref[0, 0, k2, 0, pl.ds(r0, rc)] = fr[k2] + gr[k2]
            o_ref[0, 0, k2, 1, pl.ds(r0, rc)] = fi[k2] - gi[k2]
        return carry

    lax.fori_loop(0, _HALF // rc, body, 0)


def _filter_spectra(w1, b1, w2, b2, w3, b3, freq, w_out, fwd_tab, *, seq, ct):
    n_layers = w1.shape[0]
    c = w_out.shape[2] // (2 * _ORDER)
    n_ct = c // ct
    fw = _FILTER_WIDTH
    w1t = jnp.pad(w1, ((0, 0), (0, fw - w1.shape[1]), (0, 0))).transpose(0, 2, 1)
    col = lambda a: a.reshape(n_layers, fw, 1)
    deltas = jnp.abs(jnp.linspace(_MIN_DECAY, _MAX_DECAY, c, dtype=jnp.float32)).reshape(1, c)
    lmap3 = lambda l, s: (l, 0, 0)
    wo_spec = lambda direction: pl.BlockSpec(
        (1, fw, ct), lambda l, s: (l, 0, (2 * (s // n_ct) + direction) * n_ct + s % n_ct))
    return pl.pallas_call(
        functools.partial(_filter_kernel, seq=seq),
        grid=(n_layers, _ORDER * n_ct),
        in_specs=[
            pl.BlockSpec((1, fw, fw), lmap3), pl.BlockSpec((1, fw, 1), lmap3),
            pl.BlockSpec((1, fw, fw), lmap3), pl.BlockSpec((1, fw, 1), lmap3),
            pl.BlockSpec((1, fw, fw), lmap3), pl.BlockSpec((1, fw, 1), lmap3),
            pl.BlockSpec((1, fw, 1), lmap3),
            wo_spec(0), wo_spec(1),
            pl.BlockSpec((1, ct), lambda l, s: (0, s % n_ct)),
            pl.BlockSpec((_NP, 2 * _HALF, seq // _NP), lambda l, s: (0, 0, 0)),
        ],
        out_specs=pl.BlockSpec((1, 1, _NP, 2, _HALF, ct), lambda l, s: (l, s // n_ct, 0, 0, 0, s % n_ct)),
        out_shape=jax.ShapeDtypeStruct((n_layers, _ORDER, _NP, 2, _HALF, c), jnp.float32),
        scratch_shapes=[pltpu.VMEM((seq, fw), jnp.float32),
                        pltpu.VMEM((_NP, 2 * _HALF, ct), jnp.float32),
                        pltpu.VMEM((_NP, 2 * _HALF, ct), jnp.float32)],
        compiler_params=_cparams(("arbitrary", "arbitrary")),
        name="filter_spectra",
    )(w1t, col(b1), w2.transpose(0, 2, 1), col(b2), w3.transpose(0, 2, 1), col(b3), col(freq),
      w_out, w_out, deltas, fwd_tab)


def _hyena_kernel(v_ref, x1_ref, x2_ref, kf_ref, skip_ref, fwd_ref, inv_ref, o_ref, a_scr, c_scr, z_scr, *, seq):
    n1 = seq // _NP
    rc = _V7X_BF16_SUBLANES
    ct = o_ref.shape[-1]

    def conv_order(order, z_in_ref, gate_ref, z_out_ref):
        for n2 in range(_NP):
            zb = z_in_ref[0, n2 * n1:(n2 + 1) * n1].astype(jnp.bfloat16)
            a_scr[n2] = jnp.dot(fwd_ref[n2], zb, preferred_element_type=jnp.float32)

        def body(i, carry):
            r0 = pl.multiple_of(i * rc, rc)
            for l0 in range(0, ct, _V7X_LANES):
                ls = slice(l0, l0 + _V7X_LANES)
                xr, xi = _fft8([a_scr[n2, pl.ds(r0, rc), ls] for n2 in range(_NP)],
                               [a_scr[n2, pl.ds(_HALF + r0, rc), ls] for n2 in range(_NP)])
                yr, yi = [], []
                for k2 in range(_NP):
                    kr = kf_ref[order, k2, 0, pl.ds(r0, rc), ls]
                    ki = kf_ref[order, k2, 1, pl.ds(r0, rc), ls]
                    yr.append(xr[k2] * kr - xi[k2] * ki)
                    yi.append(xr[k2] * ki + xi[k2] * kr)
                cr, ci = _ifft8(yr, yi)
                for t2 in range(_NP):
                    c_scr[t2, pl.ds(r0, rc), ls] = cr[t2].astype(jnp.bfloat16)
                    c_scr[t2, pl.ds(_HALF + r0, rc), ls] = ci[t2].astype(jnp.bfloat16)
            return carry

        lax.fori_loop(0, _HALF // rc, body, 0)

        skip = skip_ref[order:order + 1]
        for t2 in range(_NP):
            rows = slice(t2 * n1, (t2 + 1) * n1)
            y = jnp.dot(inv_ref[t2], c_scr[t2], preferred_element_type=jnp.float32)
            z = z_in_ref[0, rows]
            z_out_ref[0, rows] = gate_ref[0, rows] * (y + z * skip)

    conv_order(0, v_ref, x1_ref, z_scr)
    conv_order(1, z_scr, x2_ref, o_ref)


def _hyena(proj, kf_all, layer, skip, fwd_tab, inv_tab, *, batch, seq, ct):
    n_proj, m, c = proj.shape
    n1 = seq // _NP
    proj4 = proj.reshape(n_proj, batch, seq, c)
    pspec = lambda j: pl.BlockSpec((None, 1, seq, ct), lambda t, b, j=j: (j, b, 0, t))
    resident = dict(pipeline_mode=pl.Buffered(1))
    out = pl.pallas_call(
        functools.partial(_hyena_kernel, seq=seq),
        grid=(c // ct, batch),
        in_specs=[
            pspec(0), pspec(1), pspec(2),
            pl.BlockSpec((None, _ORDER, _NP, 2, _HALF, ct), lambda t, b: (layer, 0, 0, 0, 0, t)),
            pl.BlockSpec((_ORDER, ct), lambda t, b: (0, t)),
            pl.BlockSpec((_NP, 2 * _HALF, n1), lambda t, b: (0, 0, 0), **resident),
            pl.BlockSpec((_NP, n1, 2 * _HALF), lambda t, b: (0, 0, 0), **resident),
        ],
        out_specs=pl.BlockSpec((1, seq, ct), lambda t, b: (b, 0, t)),
        out_shape=jax.ShapeDtypeStruct((batch, seq, c), jnp.float32),
        scratch_shapes=[pltpu.VMEM((_NP, 2 * _HALF, ct), jnp.float32),
                        pltpu.VMEM((_NP, 2 * _HALF, ct), jnp.bfloat16),
                        pltpu.VMEM((1, seq, ct), jnp.float32)],
        compiler_params=_cparams(("arbitrary", "arbitrary")),
        name="hyena",
    )(proj4, proj4, proj4, kf_all, skip, fwd_tab, inv_tab)
    return out.reshape(m, c)


def _mixout_kernel(x_ref, ya_ref, yb_ref, wa_ref, wb_ref, g_ref, o_ref):
    y = jnp.dot(ya_ref[...].astype(jnp.bfloat16), wa_ref[...], preferred_element_type=jnp.float32)
    y = y + jnp.dot(yb_ref[...].astype(jnp.bfloat16), wb_ref[...], preferred_element_type=jnp.float32)
    o_ref[...] = x_ref[...] + _rms_norm(y, g_ref[...])


def _mixout(x, ya, yb, w_out, g_post, *, tm):
    m, d = x.shape
    da = ya.shape[1]
    db = yb.shape[1]
    assert da == db and w_out.shape == (da + db, d)
    return pl.pallas_call(
        _mixout_kernel,
        grid=(m // tm,),
        in_specs=[
            pl.BlockSpec((tm, d), lambda i: (i, 0)),
            pl.BlockSpec((tm, da), lambda i: (i, 0)),
            pl.BlockSpec((tm, db), lambda i: (i, 0)),
            pl.BlockSpec((da, d), lambda i: (0, 0)),
            pl.BlockSpec((db, d), lambda i: (1, 0)),
            pl.BlockSpec((1, d), lambda i: (0, 0)),
        ],
        out_specs=pl.BlockSpec((tm, d), lambda i: (i, 0)),
        out_shape=jax.ShapeDtypeStruct((m, d), jnp.float32),
        compiler_params=_cparams(("arbitrary",)),
        name="mixout",
    )(x, ya, yb, w_out, w_out, g_post.reshape(1, d))


def kernel(x, ffn1_pre_g, ffn1_w_gate, ffn1_w_up, ffn1_w_down, ffn1_post_g, mix_pre_g, mix_w_in, gmlp_ln_g, gmlp_ln_b, gmlp_w_s, gmlp_b_s, hy_conv_w, hy_conv_b, hy_filt_w1, hy_filt_b1, hy_filt_w2, hy_filt_b2, hy_filt_w3, hy_filt_b3, hy_filt_freq, hy_filt_w_out, hy_skip, mix_w_out, mix_post_g, ffn2_pre_g, ffn2_w_gate, ffn2_w_up, ffn2_w_down, ffn2_post_g):
    batch, seq, d = x.shape
    depth = ffn1_pre_g.shape[0]
    da = gmlp_ln_g.shape[1]
    assert seq % (_NP * _CHUNK) == 0 and seq // _NP + 1 <= _HALF
    m = batch * seq
    ffn_tiles = dict(tm=512, chunk=_V7X_MXU_DIM)
    ct = _V7X_MXU_DIM

    xp = x.reshape(batch, seq // _NP, _NP, d).transpose(0, 2, 1, 3).reshape(m, d)

    fwd_np, inv_np = _dft_tables(seq)
    fwd_tab = jnp.asarray(fwd_np)
    fwd_bf16 = fwd_tab.astype(jnp.bfloat16)
    inv_bf16 = jnp.asarray(inv_np).astype(jnp.bfloat16)
    kf_all = _filter_spectra(hy_filt_w1, hy_filt_b1, hy_filt_w2, hy_filt_b2, hy_filt_w3, hy_filt_b3,
                             hy_filt_freq, hy_filt_w_out, fwd_tab, seq=seq, ct=ct)

    for l in range(depth):
        w_in = mix_w_in[l].astype(jnp.bfloat16)
        xp = _ffn(xp, ffn1_pre_g[l], ffn1_w_gate[l], ffn1_w_up[l], ffn1_w_down[l], ffn1_post_g[l], **ffn_tiles)
        ya = _gmlp(xp, mix_pre_g[l], w_in, gmlp_ln_g[l], gmlp_ln_b[l], gmlp_w_s[l], gmlp_b_s[l],
                   batch=batch, seq=seq, jb=4)
        proj = _hyproj(xp, mix_pre_g[l], w_in, hy_conv_w[l], hy_conv_b[l], batch=batch, seq=seq)
        yb = _hyena(proj, kf_all, l, hy_skip[l], fwd_bf16, inv_bf16, batch=batch, seq=seq, ct=ct)
        xp = _mixout(xp, ya, yb, mix_w_out[l].astype(jnp.bfloat16), mix_post_g[l], tm=1024)
        xp = _ffn(xp, ffn2_pre_g[l], ffn2_w_gate[l], ffn2_w_up[l], ffn2_w_down[l], ffn2_post_g[l], **ffn_tiles)

    return xp.reshape(batch, _NP, seq // _NP, d).transpose(0, 2, 1, 3).reshape(batch, seq, d)
```

```python
import functools
import math

import jax
import jax.numpy as jnp
import numpy as np
from jax import lax
from jax.experimental import pallas as pl
from jax.experimental.pallas import tpu as pltpu

_CHUNK = 128
_HEADS_A = 8
_HEAD_DIM_A = 64
_ORDER = 2
_SHORT_K = 3
_FILTER_BANDS = 16
_FILTER_WIDTH = 64
_DECAY_TARGET = 1e-2
_MAX_DECAY = math.log(_DECAY_TARGET) / 0.3
_MIN_DECAY = math.log(_DECAY_TARGET) / 1.5
_HALF_STEP = 0.5
_RMS_EPS = 1e-6
_LN_EPS = 1e-5

_V7X_LANES = 128
_V7X_BF16_SUBLANES = 16
_V7X_MXU_DIM = 256
_V7X_VMEM_LIMIT_BYTES = 56 * 1024 * 1024

_NP = 8
_HALF = 272
_SQRT_HALF = 0.7071067811865476


def _cparams(semantics):
    return pltpu.CompilerParams(dimension_semantics=semantics, vmem_limit_bytes=_V7X_VMEM_LIMIT_BYTES)


def _rms_norm(x, g):
    return x * lax.rsqrt(jnp.mean(x * x, axis=-1, keepdims=True) + _RMS_EPS) * g


def _ffn_kernel(*refs, chunk, mix_in, norm_out):
    refs = list(refs)
    x_ref = refs.pop(0)
    x = x_ref[...]
    if mix_in:
        ya_ref, yb_ref, wa_ref, wb_ref, gmix_ref = refs[:5]
        refs = refs[5:]
        y = jnp.dot(ya_ref[...], wa_ref[...], preferred_element_type=jnp.float32)
        y = y + jnp.dot(yb_ref[...], wb_ref[...], preferred_element_type=jnp.float32)
        x = x + _rms_norm(y, gmix_ref[...])
    gpre_ref, wg_ref, wu_ref, wd_ref, gpost_ref = refs[:5]
    refs = refs[5:]
    h = _rms_norm(x, gpre_ref[...]).astype(jnp.bfloat16)
    acc = jnp.zeros(x.shape, jnp.float32)
    for c0 in range(0, wg_ref.shape[1], chunk):
        g = jnp.dot(h, wg_ref[:, c0:c0 + chunk], preferred_element_type=jnp.float32)
        u = jnp.dot(h, wu_ref[:, c0:c0 + chunk], preferred_element_type=jnp.float32)
        a = (g * jax.nn.sigmoid(g) * u).astype(jnp.bfloat16)
        acc = acc + jnp.dot(a, wd_ref[c0:c0 + chunk, :], preferred_element_type=jnp.float32)
    out = x + _HALF_STEP * _rms_norm(acc, gpost_ref[...])
    if norm_out:
        gnext_ref, o_ref, hn_ref = refs
        hn_ref[...] = _rms_norm(out, gnext_ref[...]).astype(jnp.bfloat16)
    else:
        (o_ref,) = refs
    o_ref[...] = out


def _ffn(x, layer, g_pre, w_gate, w_up, w_down, g_post, *, tm, chunk, mix=None, g_next=None):
    m, d = x.shape
    dff = w_gate.shape[2]
    assert dff % chunk == 0
    row = lambda a: a[layer].reshape(1, d)
    tile = lambda width: pl.BlockSpec((tm, width), lambda i: (i, 0))
    vec = pl.BlockSpec((1, d), lambda i: (0, 0))
    resident = lambda r, c, blk=0: pl.BlockSpec((None, r, c), lambda i: (layer, blk, 0), pipeline_mode=pl.Buffered(1))
    args, specs = [x], [tile(d)]
    if mix is not None:
        ya, yb, w_out, g_mix = mix
        da = ya.shape[1]
        assert yb.shape[1] == da and w_out.shape[1:] == (2 * da, d)
        args += [ya, yb, w_out, w_out, row(g_mix)]
        specs += [tile(da), tile(da), resident(da, d, 0), resident(da, d, 1), vec]
    args += [row(g_pre), w_gate, w_up, w_down, row(g_post)]
    specs += [vec, resident(d, dff), resident(d, dff), resident(dff, d), vec]
    out_shape = jax.ShapeDtypeStruct((m, d), jnp.float32)
    out_specs = tile(d)
    if g_next is not None:
        args.append(row(g_next))
        specs.append(vec)
        out_shape = (out_shape, jax.ShapeDtypeStruct((m, d), jnp.bfloat16))
        out_specs = (out_specs, tile(d))
    return pl.pallas_call(
        functools.partial(_ffn_kernel, chunk=chunk, mix_in=mix is not None, norm_out=g_next is not None),
        grid=(m // tm,),
        in_specs=specs,
        out_specs=out_specs,
        out_shape=out_shape,
        compiler_params=_cparams(("arbitrary",)),
        name="ffn",
    )(*args)


def _gmlp_kernel(h_ref, w_ref, lng_ref, lnb_ref, ws_ref, bs_ref, o_ref, *, jb, da):
    rows = _CHUNK // _NP
    d = h_ref.shape[-1]
    h = h_ref[0].reshape(_NP * jb * rows, d)
    p = jnp.dot(h, w_ref[...], preferred_element_type=jnp.float32)
    p = 0.5 * p * (1.0 + lax.erf(p * _SQRT_HALF))
    u = p[:, :da]
    v = p[:, da:]
    mu = jnp.mean(v, axis=-1, keepdims=True)
    vc = v - mu
    var = jnp.mean(vc * vc, axis=-1, keepdims=True)
    vn = (vc * lax.rsqrt(var + _LN_EPS) * lng_ref[...] + lnb_ref[...]).astype(jnp.bfloat16)
    lane = lax.broadcasted_iota(jnp.int32, (_CHUNK, _V7X_LANES), 1)
    first_head = lane < _HEAD_DIM_A
    n_pairs = da // _V7X_LANES
    for j in range(jb):
        starts = [(n2 * jb + j) * rows for n2 in range(_NP)]
        vchunk = jnp.concatenate([vn[s:s + rows] for s in starts], axis=0)
        mixed = []
        for q in range(n_pairs):
            r = jnp.dot(ws_ref[q], vchunk[:, q * _V7X_LANES:(q + 1) * _V7X_LANES],
                        preferred_element_type=jnp.float32)
            mixed.append(jnp.where(first_head, r[:_CHUNK], r[_CHUNK:]))
        mixed = jnp.concatenate(mixed, axis=1) + bs_ref[...]
        for n2 in range(_NP):
            s = starts[n2]
            o_ref[0, n2, j] = (u[s:s + rows] * mixed[n2 * rows:(n2 + 1) * rows]).astype(o_ref.dtype)


def _gmlp(h, w_in, layer, ln_g, ln_b, w_s, b_s, *, batch, seq, jb):
    m, d = h.shape
    da = ln_g.shape[0]
    rows = _CHUNK // _NP
    n_chunks = seq // _CHUNK
    assert rows == _V7X_BF16_SUBLANES
    h5 = h.reshape(batch, _NP, n_chunks, rows, d)
    ws_p = w_s.reshape(_HEADS_A, rows, _NP, rows, _NP).transpose(0, 2, 1, 4, 3).reshape(_HEADS_A, _CHUNK, _CHUNK)
    ws_pairs = ws_p.reshape(_HEADS_A // 2, 2 * _CHUNK, _CHUNK).astype(jnp.bfloat16)
    bs_p = b_s.reshape(_HEADS_A, rows, _NP).transpose(0, 2, 1).reshape(_HEADS_A, _CHUNK)
    bs_full = jnp.repeat(bs_p.T, _HEAD_DIM_A, axis=1)
    out = pl.pallas_call(
        functools.partial(_gmlp_kernel, jb=jb, da=da),
        grid=(batch, n_chunks // jb),
        in_specs=[
            pl.BlockSpec((1, _NP, jb, rows, d), lambda b, j: (b, 0, j, 0, 0)),
            pl.BlockSpec((None, d, 2 * da), lambda b, j: (layer, 0, 0)),
            pl.BlockSpec((1, da), lambda b, j: (0, 0)),
            pl.BlockSpec((1, da), lambda b, j: (0, 0)),
            pl.BlockSpec((_HEADS_A // 2, 2 * _CHUNK, _CHUNK), lambda b, j: (0, 0, 0)),
            pl.BlockSpec((_CHUNK, da), lambda b, j: (0, 0)),
        ],
        out_specs=pl.BlockSpec((1, _NP, jb, rows, da), lambda b, j: (b, 0, j, 0, 0)),
        out_shape=jax.ShapeDtypeStruct((batch, _NP, n_chunks, rows, da), jnp.bfloat16),
        compiler_params=_cparams(("arbitrary", "arbitrary")),
        name="gmlp",
    )(h5, w_in, ln_g.reshape(1, da), ln_b.reshape(1, da), ws_pairs, bs_full)
    return out.reshape(m, da)


def _hyproj_kernel(h_ref, w_ref, cw_ref, cb_ref, o_ref, *, n1):
    p = jnp.dot(h_ref[0], w_ref[...], preferred_element_type=jnp.float32)
    blocks = [p[k * n1:(k + 1) * n1] for k in range(_NP)]
    row = lax.broadcasted_iota(jnp.int32, blocks[0].shape, 0)
    before_first = jnp.where(row == 0, 0.0, pltpu.roll(blocks[_NP - 1], 1, axis=0))
    after_last = jnp.where(row == n1 - 1, 0.0, pltpu.roll(blocks[0], n1 - 1, axis=0))
    w0 = cw_ref[0:1]
    w1 = cw_ref[1:2]
    w2 = cw_ref[2:3]
    bias = cb_ref[...]
    for k in range(_NP):
        prev = blocks[k - 1] if k > 0 else before_first
        nxt = blocks[k + 1] if k < _NP - 1 else after_last
        o_ref[0, k * n1:(k + 1) * n1] = ((bias + prev * w0) + blocks[k] * w1) + nxt * w2


def _hyproj(h, w_in, layer, conv_w, conv_b, *, batch, seq):
    m, d = h.shape
    n_proj = _ORDER + 1
    c = conv_w.shape[1] // n_proj
    first = (w_in.shape[2] - n_proj * c) // c
    return pl.pallas_call(
        functools.partial(_hyproj_kernel, n1=seq // _NP),
        grid=(batch, n_proj),
        in_specs=[
            pl.BlockSpec((1, seq, d), lambda b, j: (b, 0, 0)),
            pl.BlockSpec((None, d, c), lambda b, j: (layer, 0, first + j)),
            pl.BlockSpec((_SHORT_K, c), lambda b, j: (0, j)),
            pl.BlockSpec((1, c), lambda b, j: (0, j)),
        ],
        out_specs=pl.BlockSpec((1, seq, c), lambda b, j: (j, b, 0)),
        out_shape=jax.ShapeDtypeStruct((n_proj, m, c), jnp.float32),
        compiler_params=_cparams(("arbitrary", "arbitrary")),
        name="hyproj",
    )(h.reshape(batch, seq, d), w_in, conv_w, conv_b.reshape(1, n_proj * c))


@functools.lru_cache(maxsize=None)
def _dft_tables(seq):
    n1 = seq // _NP
    n_fft = 2 * seq
    k1 = np.arange(_HALF)[None, :, None]
    t = _NP * np.arange(n1)[None, None, :] + np.arange(_NP)[:, None, None]
    theta = ((k1 * t) % n_fft) * (2.0 * math.pi / n_fft)
    valid = k1 <= n1
    cos = np.where(valid, np.cos(theta), 0.0)
    sin = np.where(valid, np.sin(theta), 0.0)
    fwd = np.concatenate([cos, -sin], axis=1)
    weight = np.where((k1 == 0) | (k1 == n1), 1.0, 2.0) / n_fft
    inv = np.concatenate([weight * cos, -weight * sin], axis=1).transpose(0, 2, 1)
    return fwd.astype(np.float32), np.ascontiguousarray(inv).astype(np.float32)


def _fft4(cr, ci):
    d0r, d0i = cr[0] + cr[2], ci[0] + ci[2]
    d1r, d1i = cr[1] + cr[3], ci[1] + ci[3]
    d2r, d2i = cr[0] - cr[2], ci[0] - ci[2]
    er, ei = cr[1] - cr[3], ci[1] - ci[3]
    d3r, d3i = ei, -er
    return [(d0r + d1r, d0i + d1i), (d2r + d3r, d2i + d3i), (d0r - d1r, d0i - d1i), (d2r - d3r, d2i - d3i)]


def _fft8(re, im):
    br, bi = [None] * 8, [None] * 8
    for j in range(4):
        br[j], bi[j] = re[j] + re[j + 4], im[j] + im[j + 4]
        dr, di = re[j] - re[j + 4], im[j] - im[j + 4]
        if j == 0:
            br[4], bi[4] = dr, di
        elif j == 1:
            br[5], bi[5] = (dr + di) * _SQRT_HALF, (di - dr) * _SQRT_HALF
        elif j == 2:
            br[6], bi[6] = di, -dr
        else:
            br[7], bi[7] = (di - dr) * _SQRT_HALF, -(dr + di) * _SQRT_HALF
    even = _fft4(br[:4], bi[:4])
    odd = _fft4(br[4:], bi[4:])
    out = [None] * 8
    for k in range(4):
        out[2 * k], out[2 * k + 1] = even[k], odd[k]
    return [o[0] for o in out], [o[1] for o in out]


def _ifft8(re, im):
    o_im, o_re = _fft8(im, re)
    return o_re, o_im


def _filter_kernel(w1_ref, b1_ref, w2_ref, b2_ref, w3_ref, b3_ref, fr_ref, wof_ref, wob_ref, dl_ref,
                   fwd_ref, o_ref, h_scr, af_scr, ab_scr, *, seq):
    n1 = seq // _NP
    fw = _FILTER_WIDTH
    hp = lax.Precision.HIGHEST

    @pl.when(pl.program_id(1) == 0)
    def _():
        r = lax.broadcasted_iota(jnp.int32, (fw, seq), 1)
        pos = (_NP * (r % n1) + r // n1).astype(jnp.float32)
        row = lax.broadcasted_iota(jnp.int32, (fw, seq), 0)
        is_cos = (row >= 1) & (row <= _FILTER_BANDS)
        is_sin = (row > _FILTER_BANDS) & (row <= 2 * _FILTER_BANDS)
        band_idx = jnp.where(is_cos, row - 1, row - 1 - _FILTER_BANDS).astype(jnp.float32)
        band = 1e-4 + band_idx * ((_FILTER_BANDS - 1 - 1e-4) / (_FILTER_BANDS - 1))
        ang = (2.0 * math.pi / seq) * pos * band
        z = jnp.where(row == 0, pos * (1.0 / (seq - 1)),
                      jnp.where(is_cos, jnp.cos(ang), jnp.where(is_sin, -jnp.sin(ang), 0.0)))
        freq = fr_ref[0]
        h = jnp.sin(freq * (jnp.dot(w1_ref[0], z, precision=hp, preferred_element_type=jnp.float32) + b1_ref[0]))
        h = jnp.sin(freq * (jnp.dot(w2_ref[0], h, precision=hp, preferred_element_type=jnp.float32) + b2_ref[0]))
        h = jnp.sin(freq * (jnp.dot(w3_ref[0], h, precision=hp, preferred_element_type=jnp.float32) + b3_ref[0]))
        h_scr[...] = h.T

    r = lax.broadcasted_iota(jnp.int32, (seq, 1), 0)
    pos = (_NP * (r % n1) + r // n1).astype(jnp.float32)
    h = h_scr[...].astype(jnp.bfloat16)
    decay = jnp.exp(-(pos * (1.0 / (seq - 1))) * dl_ref[...])
    hf = jnp.dot(h, wof_ref[0].astype(jnp.bfloat16), preferred_element_type=jnp.float32) * decay
    hb = jnp.dot(h, wob_ref[0].astype(jnp.bfloat16), preferred_element_type=jnp.float32) * decay
    hb = jnp.where(pos == 0.0, 0.0, hb)
    hf = hf.astype(jnp.bfloat16)
    hb = hb.astype(jnp.bfloat16)
    for n2 in range(_NP):
        af_scr[n2] = jnp.dot(fwd_ref[n2], hf[n2 * n1:(n2 + 1) * n1], preferred_element_type=jnp.float32)
        ab_scr[n2] = jnp.dot(fwd_ref[n2], hb[n2 * n1:(n2 + 1) * n1], preferred_element_type=jnp.float32)

    rc = _V7X_BF16_SUBLANES

    def body(i, carry):
        r0 = pl.multiple_of(i * rc, rc)
        fr, fi = _fft8([af_scr[n2, pl.ds(r0, rc)] for n2 in range(_NP)],
                       [af_scr[n2, pl.ds(_HALF + r0, rc)] for n2 in range(_NP)])
        gr, gi = _fft8([ab_scr[n2, pl.ds(r0, rc)] for n2 in range(_NP)],
                       [ab_scr[n2, pl.ds(_HALF + r0, rc)] for n2 in range(_NP)])
        for k2 in range(_NP):
            o_ref[0, 0, k2, 0, pl.ds(r0, rc)] = fr[k2] + gr[k2]
            o_ref[0, 0, k2, 1, pl.ds(r0, rc)] = fi[k2] - gi[k2]
        return carry

    lax.fori_loop(0, _HALF // rc, body, 0)


def _filter_spectra(w1, b1, w2, b2, w3, b3, freq, w_out, fwd_tab, *, seq, ct):
    n_layers = w1.shape[0]
    c = w_out.shape[2] // (2 * _ORDER)
    n_ct = c // ct
    fw = _FILTER_WIDTH
    w1t = jnp.pad(w1, ((0, 0), (0, fw - w1.shape[1]), (0, 0))).transpose(0, 2, 1)
    col = lambda a: a.reshape(n_layers, fw, 1)
    deltas = jnp.abs(jnp.linspace(_MIN_DECAY, _MAX_DECAY, c, dtype=jnp.float32)).reshape(1, c)
    lmap3 = lambda l, s: (l, 0, 0)
    wo_spec = lambda direction: pl.BlockSpec(
        (1, fw, ct), lambda l, s: (l, 0, (2 * (s // n_ct) + direction) * n_ct + s % n_ct))
    return pl.pallas_call(
        functools.partial(_filter_kernel, seq=seq),
        grid=(n_layers, _ORDER * n_ct),
        in_specs=[
            pl.BlockSpec((1, fw, fw), lmap3), pl.BlockSpec((1, fw, 1), lmap3),
            pl.BlockSpec((1, fw, fw), lmap3), pl.BlockSpec((1, fw, 1), lmap3),
            pl.BlockSpec((1, fw, fw), lmap3), pl.BlockSpec((1, fw, 1), lmap3),
            pl.BlockSpec((1, fw, 1), lmap3),
            wo_spec(0), wo_spec(1),
            pl.BlockSpec((1, ct), lambda l, s: (0, s % n_ct)),
            pl.BlockSpec((_NP, 2 * _HALF, seq // _NP), lambda l, s: (0, 0, 0)),
        ],
        out_specs=pl.BlockSpec((1, 1, _NP, 2, _HALF, ct), lambda l, s: (l, s // n_ct, 0, 0, 0, s % n_ct)),
        out_shape=jax.ShapeDtypeStruct((n_layers, _ORDER, _NP, 2, _HALF, c), jnp.float32),
        scratch_shapes=[pltpu.VMEM((seq, fw), jnp.float32),
                        pltpu.VMEM((_NP, 2 * _HALF, ct), jnp.float32),
                        pltpu.VMEM((_NP, 2 * _HALF, ct), jnp.float32)],
        compiler_params=_cparams(("arbitrary", "arbitrary")),
        name="filter_spectra",
    )(w1t, col(b1), w2.transpose(0, 2, 1), col(b2), w3.transpose(0, 2, 1), col(b3), col(freq),
      w_out, w_out, deltas, fwd_tab)


def _hyena_kernel(v_ref, x1_ref, x2_ref, kf_ref, skip_ref, fwd_ref, inv_ref, o_ref, a_scr, c_scr, z_scr, *, seq):
    n1 = seq // _NP
    rc = _V7X_BF16_SUBLANES
    ct = o_ref.shape[-1]

    def conv_order(order, z_in_ref, gate_ref, z_out_ref):
        for n2 in range(_NP):
            zb = z_in_ref[0, n2 * n1:(n2 + 1) * n1].astype(jnp.bfloat16)
            a_scr[n2] = jnp.dot(fwd_ref[n2], zb, preferred_element_type=jnp.float32)

        def body(i, carry):
            r0 = pl.multiple_of(i * rc, rc)
            for l0 in range(0, ct, _V7X_LANES):
                ls = slice(l0, l0 + _V7X_LANES)
                xr, xi = _fft8([a_scr[n2, pl.ds(r0, rc), ls] for n2 in range(_NP)],
                               [a_scr[n2, pl.ds(_HALF + r0, rc), ls] for n2 in range(_NP)])
                yr, yi = [], []
                for k2 in range(_NP):
                    kr = kf_ref[order, k2, 0, pl.ds(r0, rc), ls]
                    ki = kf_ref[order, k2, 1, pl.ds(r0, rc), ls]
                    yr.append(xr[k2] * kr - xi[k2] * ki)
                    yi.append(xr[k2] * ki + xi[k2] * kr)
                cr, ci = _ifft8(yr, yi)
                for t2 in range(_NP):
                    c_scr[t2, pl.ds(r0, rc), ls] = cr[t2].astype(jnp.bfloat16)
                    c_scr[t2, pl.ds(_HALF + r0, rc), ls] = ci[t2].astype(jnp.bfloat16)
            return carry

        lax.fori_loop(0, _HALF // rc, body, 0)

        skip = skip_ref[order:order + 1]
        for t2 in range(_NP):
            rows = slice(t2 * n1, (t2 + 1) * n1)
            y = jnp.dot(inv_ref[t2], c_scr[t2], preferred_element_type=jnp.float32)
            z = z_in_ref[0, rows]
            z_out_ref[0, rows] = (gate_ref[0, rows] * (y + z * skip)).astype(z_out_ref.dtype)

    conv_order(0, v_ref, x1_ref, z_scr)
    conv_order(1, z_scr, x2_ref, o_ref)


def _hyena(proj, kf_all, layer, skip, fwd_tab, inv_tab, *, batch, seq, ct):
    n_proj, m, c = proj.shape
    n1 = seq // _NP
    proj4 = proj.reshape(n_proj, batch, seq, c)
    pspec = lambda j: pl.BlockSpec((None, 1, seq, ct), lambda t, b, j=j: (j, b, 0, t))
    resident = dict(pipeline_mode=pl.Buffered(1))
    out = pl.pallas_call(
        functools.partial(_hyena_kernel, seq=seq),
        grid=(c // ct, batch),
        in_specs=[
            pspec(0), pspec(1), pspec(2),
            pl.BlockSpec((None, _ORDER, _NP, 2, _HALF, ct), lambda t, b: (layer, 0, 0, 0, 0, t)),
            pl.BlockSpec((_ORDER, ct), lambda t, b: (0, t)),
            pl.BlockSpec((_NP, 2 * _HALF, n1), lambda t, b: (0, 0, 0), **resident),
            pl.BlockSpec((_NP, n1, 2 * _HALF), lambda t, b: (0, 0, 0), **resident),
        ],
        out_specs=pl.BlockSpec((1, seq, ct), lambda t, b: (b, 0, t)),
        out_shape=jax.ShapeDtypeStruct((batch, seq, c), jnp.bfloat16),
        scratch_shapes=[pltpu.VMEM((_NP, 2 * _HALF, ct), jnp.float32),
                        pltpu.VMEM((_NP, 2 * _HALF, ct), jnp.bfloat16),
                        pltpu.VMEM((1, seq, ct), jnp.float32)],
        compiler_params=_cparams(("arbitrary", "arbitrary")),
        name="hyena",
    )(proj4, proj4, proj4, kf_all, skip, fwd_tab, inv_tab)
    return out.reshape(m, c)


def kernel(x, ffn1_pre_g, ffn1_w_gate, ffn1_w_up, ffn1_w_down, ffn1_post_g, mix_pre_g, mix_w_in, gmlp_ln_g, gmlp_ln_b, gmlp_w_s, gmlp_b_s, hy_conv_w, hy_conv_b, hy_filt_w1, hy_filt_b1, hy_filt_w2, hy_filt_b2, hy_filt_w3, hy_filt_b3, hy_filt_freq, hy_filt_w_out, hy_skip, mix_w_out, mix_post_g, ffn2_pre_g, ffn2_w_gate, ffn2_w_up, ffn2_w_down, ffn2_post_g):
    batch, seq, d = x.shape
    depth = ffn1_pre_g.shape[0]
    da = gmlp_ln_g.shape[1]
    assert seq % (_NP * _CHUNK) == 0 and seq // _NP + 1 <= _HALF
    m = batch * seq
    ffn_tiles = dict(tm=512, chunk=_V7X_MXU_DIM)
    ct = _V7X_MXU_DIM

    xp = x.reshape(batch, seq // _NP, _NP, d).transpose(0, 2, 1, 3).reshape(m, d)

    fwd_np, inv_np = _dft_tables(seq)
    fwd_bf16 = jnp.asarray(fwd_np).astype(jnp.bfloat16)
    inv_bf16 = jnp.asarray(inv_np).astype(jnp.bfloat16)
    kf_all = _filter_spectra(hy_filt_w1, hy_filt_b1, hy_filt_w2, hy_filt_b2, hy_filt_w3, hy_filt_b3,
                             hy_filt_freq, hy_filt_w_out, fwd_bf16, seq=seq, ct=ct)

    bf16 = lambda w: w.astype(jnp.bfloat16)
    ffn1_w = (bf16(ffn1_w_gate), bf16(ffn1_w_up), bf16(ffn1_w_down))
    ffn2_w = (bf16(ffn2_w_gate), bf16(ffn2_w_up), bf16(ffn2_w_down))
    w_in = bf16(mix_w_in)
    w_out = bf16(mix_w_out)

    for l in range(depth):
        xp, h = _ffn(xp, l, ffn1_pre_g, *ffn1_w, ffn1_post_g, g_next=mix_pre_g, **ffn_tiles)
        ya = _gmlp(h, w_in, l, gmlp_ln_g[l], gmlp_ln_b[l], gmlp_w_s[l], gmlp_b_s[l], batch=batch, seq=seq, jb=4)
        proj = _hyproj(h, w_in, l, hy_conv_w[l], hy_conv_b[l], batch=batch, seq=seq)
        yb = _hyena(proj, kf_all, l, hy_skip[l], fwd_bf16, inv_bf16, batch=batch, seq=seq, ct=ct)
        xp = _ffn(xp, l, ffn2_pre_g, *ffn2_w, ffn2_post_g, mix=(ya, yb, w_out, mix_post_g), **ffn_tiles)

    return xp.reshape(batch, _NP, seq // _NP, d).transpose(0, 2, 1, 3).reshape(batch, seq, d)
```

```python
import functools
import math

import jax
import jax.numpy as jnp
import numpy as np
from jax import lax
from jax.experimental import pallas as pl
from jax.experimental.pallas import tpu as pltpu

_CHUNK = 128
_HEADS_A = 8
_HEAD_DIM_A = 64
_ORDER = 2
_SHORT_K = 3
_FILTER_BANDS = 16
_FILTER_WIDTH = 64
_DECAY_TARGET = 1e-2
_MAX_DECAY = math.log(_DECAY_TARGET) / 0.3
_MIN_DECAY = math.log(_DECAY_TARGET) / 1.5
_HALF_STEP = 0.5
_RMS_EPS = 1e-6
_LN_EPS = 1e-5

_V7X_LANES = 128
_V7X_BF16_SUBLANES = 16
_V7X_MXU_DIM = 256
_V7X_VMEM_LIMIT_BYTES = 60 * 1024 * 1024

_NP = 8
_HALF = 272
_SQRT_HALF = 0.7071067811865476


def _cparams(semantics):
    return pltpu.CompilerParams(dimension_semantics=semantics, vmem_limit_bytes=_V7X_VMEM_LIMIT_BYTES)


def _rms_norm(x, g):
    return x * lax.rsqrt(jnp.mean(x * x, axis=-1, keepdims=True) + _RMS_EPS) * g


def _ffn_kernel(*refs, chunk, mix_in, norm_out):
    refs = list(refs)
    x_ref = refs.pop(0)
    x = x_ref[...]
    if mix_in:
        ya_ref, yb_ref, wa_ref, wb_ref, gmix_ref = refs[:5]
        refs = refs[5:]
        y = jnp.dot(ya_ref[...], wa_ref[...], preferred_element_type=jnp.float32)
        y = y + jnp.dot(yb_ref[...], wb_ref[...], preferred_element_type=jnp.float32)
        x = x + _rms_norm(y, gmix_ref[...])
    gpre_ref, wg_ref, wu_ref, wd_ref, gpost_ref = refs[:5]
    refs = refs[5:]
    h = _rms_norm(x, gpre_ref[...]).astype(jnp.bfloat16)
    acc = jnp.zeros(x.shape, jnp.float32)
    dff = wg_ref.shape[1]
    for c0 in range(0, dff, chunk):
        c1 = min(c0 + chunk, dff)
        g = jnp.dot(h, wg_ref[:, c0:c1], preferred_element_type=jnp.float32)
        u = jnp.dot(h, wu_ref[:, c0:c1], preferred_element_type=jnp.float32)
        a = (g * jax.nn.sigmoid(g) * u).astype(jnp.bfloat16)
        acc = acc + jnp.dot(a, wd_ref[c0:c1, :], preferred_element_type=jnp.float32)
    out = x + _HALF_STEP * _rms_norm(acc, gpost_ref[...])
    if norm_out:
        gnext_ref, o_ref, hn_ref = refs
        hn_ref[...] = _rms_norm(out, gnext_ref[...]).astype(jnp.bfloat16)
    else:
        (o_ref,) = refs
    o_ref[...] = out


def _ffn(x, layer, g_pre, w_gate, w_up, w_down, g_post, *, tm, chunk, mix=None, g_next=None):
    m, d = x.shape
    dff = w_gate.shape[2]
    assert dff % _V7X_MXU_DIM == 0 and chunk % _V7X_MXU_DIM == 0
    row = lambda a: a[layer].reshape(1, d)
    tile = lambda width: pl.BlockSpec((tm, width), lambda i: (i, 0))
    vec = pl.BlockSpec((1, d), lambda i: (0, 0))
    resident = lambda r, c, blk=0: pl.BlockSpec((None, r, c), lambda i: (layer, blk, 0), pipeline_mode=pl.Buffered(1))
    args, specs = [x], [tile(d)]
    if mix is not None:
        ya, yb, w_out, g_mix = mix
        da = ya.shape[1]
        assert yb.shape[1] == da and w_out.shape[1:] == (2 * da, d)
        args += [ya, yb, w_out, w_out, row(g_mix)]
        specs += [tile(da), tile(da), resident(da, d, 0), resident(da, d, 1), vec]
    args += [row(g_pre), w_gate, w_up, w_down, row(g_post)]
    specs += [vec, resident(d, dff), resident(d, dff), resident(dff, d), vec]
    out_shape = jax.ShapeDtypeStruct((m, d), jnp.float32)
    out_specs = tile(d)
    if g_next is not None:
        args.append(row(g_next))
        specs.append(vec)
        out_shape = (out_shape, jax.ShapeDtypeStruct((m, d), jnp.bfloat16))
        out_specs = (out_specs, tile(d))
    return pl.pallas_call(
        functools.partial(_ffn_kernel, chunk=chunk, mix_in=mix is not None, norm_out=g_next is not None),
        grid=(m // tm,),
        in_specs=specs,
        out_specs=out_specs,
        out_shape=out_shape,
        compiler_params=_cparams(("arbitrary",)),
        name="ffn",
    )(*args)


def _gmlp_kernel(h_ref, w_ref, lng_ref, lnb_ref, ws_ref, bs_ref, o_ref, *, jb, da):
    rows = _CHUNK // _NP
    d = h_ref.shape[-1]
    h = h_ref[0].reshape(_NP * jb * rows, d)
    p = jnp.dot(h, w_ref[...], preferred_element_type=jnp.float32)
    p = 0.5 * p * (1.0 + lax.erf(p * _SQRT_HALF))
    u = p[:, :da]
    v = p[:, da:]
    mu = jnp.mean(v, axis=-1, keepdims=True)
    vc = v - mu
    var = jnp.mean(vc * vc, axis=-1, keepdims=True)
    vn = (vc * lax.rsqrt(var + _LN_EPS) * lng_ref[...] + lnb_ref[...]).astype(jnp.bfloat16)
    lane = lax.broadcasted_iota(jnp.int32, (_CHUNK, _V7X_LANES), 1)
    first_head = lane < _HEAD_DIM_A
    n_pairs = da // _V7X_LANES
    for j in range(jb):
        starts = [(n2 * jb + j) * rows for n2 in range(_NP)]
        vchunk = jnp.concatenate([vn[s:s + rows] for s in starts], axis=0)
        mixed = []
        for q in range(n_pairs):
            r = jnp.dot(ws_ref[q], vchunk[:, q * _V7X_LANES:(q + 1) * _V7X_LANES],
                        preferred_element_type=jnp.float32)
            mixed.append(jnp.where(first_head, r[:_CHUNK], r[_CHUNK:]))
        mixed = jnp.concatenate(mixed, axis=1) + bs_ref[...]
        for n2 in range(_NP):
            s = starts[n2]
            o_ref[0, n2, j] = (u[s:s + rows] * mixed[n2 * rows:(n2 + 1) * rows]).astype(o_ref.dtype)


def _gmlp(h, w_in, layer, ln_g, ln_b, w_s, b_s, *, batch, seq, jb):
    m, d = h.shape
    da = ln_g.shape[0]
    rows = _CHUNK // _NP
    n_chunks = seq // _CHUNK
    assert rows == _V7X_BF16_SUBLANES
    h5 = h.reshape(batch, _NP, n_chunks, rows, d)
    ws_p = w_s.reshape(_HEADS_A, rows, _NP, rows, _NP).transpose(0, 2, 1, 4, 3).reshape(_HEADS_A, _CHUNK, _CHUNK)
    ws_pairs = ws_p.reshape(_HEADS_A // 2, 2 * _CHUNK, _CHUNK).astype(jnp.bfloat16)
    bs_p = b_s.reshape(_HEADS_A, rows, _NP).transpose(0, 2, 1).reshape(_HEADS_A, _CHUNK)
    bs_full = jnp.repeat(bs_p.T, _HEAD_DIM_A, axis=1)
    out = pl.pallas_call(
        functools.partial(_gmlp_kernel, jb=jb, da=da),
        grid=(batch, n_chunks // jb),
        in_specs=[
            pl.BlockSpec((1, _NP, jb, rows, d), lambda b, j: (b, 0, j, 0, 0)),
            pl.BlockSpec((None, d, 2 * da), lambda b, j: (layer, 0, 0)),
            pl.BlockSpec((1, da), lambda b, j: (0, 0)),
            pl.BlockSpec((1, da), lambda b, j: (0, 0)),
            pl.BlockSpec((_HEADS_A // 2, 2 * _CHUNK, _CHUNK), lambda b, j: (0, 0, 0)),
            pl.BlockSpec((_CHUNK, da), lambda b, j: (0, 0)),
        ],
        out_specs=pl.BlockSpec((1, _NP, jb, rows, da), lambda b, j: (b, 0, j, 0, 0)),
        out_shape=jax.ShapeDtypeStruct((batch, _NP, n_chunks, rows, da), jnp.bfloat16),
        compiler_params=_cparams(("arbitrary", "arbitrary")),
        name="gmlp",
    )(h5, w_in, ln_g.reshape(1, da), ln_b.reshape(1, da), ws_pairs, bs_full)
    return out.reshape(m, da)


def _hyproj_kernel(h_ref, w_ref, cw_ref, cb_ref, o_ref, *, n1):
    p = jnp.dot(h_ref[0], w_ref[...], preferred_element_type=jnp.float32)
    blocks = [p[k * n1:(k + 1) * n1] for k in range(_NP)]
    row = lax.broadcasted_iota(jnp.int32, blocks[0].shape, 0)
    before_first = jnp.where(row == 0, 0.0, pltpu.roll(blocks[_NP - 1], 1, axis=0))
    after_last = jnp.where(row == n1 - 1, 0.0, pltpu.roll(blocks[0], n1 - 1, axis=0))
    w0 = cw_ref[0:1]
    w1 = cw_ref[1:2]
    w2 = cw_ref[2:3]
    bias = cb_ref[...]
    for k in range(_NP):
        prev = blocks[k - 1] if k > 0 else before_first
        nxt = blocks[k + 1] if k < _NP - 1 else after_last
        o_ref[0, k * n1:(k + 1) * n1] = (((bias + prev * w0) + blocks[k] * w1) + nxt * w2).astype(o_ref.dtype)


def _hyproj(h, w_in, layer, conv_w, conv_b, *, batch, seq):
    m, d = h.shape
    n_proj = _ORDER + 1
    c = conv_w.shape[1] // n_proj
    first = (w_in.shape[2] - n_proj * c) // c
    return pl.pallas_call(
        functools.partial(_hyproj_kernel, n1=seq // _NP),
        grid=(batch, n_proj),
        in_specs=[
            pl.BlockSpec((1, seq, d), lambda b, j: (b, 0, 0)),
            pl.BlockSpec((None, d, c), lambda b, j: (layer, 0, first + j)),
            pl.BlockSpec((_SHORT_K, c), lambda b, j: (0, j)),
            pl.BlockSpec((1, c), lambda b, j: (0, j)),
        ],
        out_specs=pl.BlockSpec((1, seq, c), lambda b, j: (j, b, 0)),
        out_shape=jax.ShapeDtypeStruct((n_proj, m, c), jnp.bfloat16),
        compiler_params=_cparams(("arbitrary", "arbitrary")),
        name="hyproj",
    )(h.reshape(batch, seq, d), w_in, conv_w, conv_b.reshape(1, n_proj * c))


@functools.lru_cache(maxsize=None)
def _dft_tables(seq):
    n1 = seq // _NP
    n_fft = 2 * seq
    k1 = np.arange(_HALF)[None, :, None]
    t = _NP * np.arange(n1)[None, None, :] + np.arange(_NP)[:, None, None]
    theta = ((k1 * t) % n_fft) * (2.0 * math.pi / n_fft)
    valid = k1 <= n1
    cos = np.where(valid, np.cos(theta), 0.0)
    sin = np.where(valid, np.sin(theta), 0.0)
    fwd = np.concatenate([cos, -sin], axis=1)
    weight = np.where((k1 == 0) | (k1 == n1), 1.0, 2.0) / n_fft
    inv = np.concatenate([weight * cos, -weight * sin], axis=1).transpose(0, 2, 1)
    return fwd.astype(np.float32), np.ascontiguousarray(inv).astype(np.float32)


def _fft4(cr, ci):
    d0r, d0i = cr[0] + cr[2], ci[0] + ci[2]
    d1r, d1i = cr[1] + cr[3], ci[1] + ci[3]
    d2r, d2i = cr[0] - cr[2], ci[0] - ci[2]
    er, ei = cr[1] - cr[3], ci[1] - ci[3]
    d3r, d3i = ei, -er
    return [(d0r + d1r, d0i + d1i), (d2r + d3r, d2i + d3i), (d0r - d1r, d0i - d1i), (d2r - d3r, d2i - d3i)]


def _fft8(re, im):
    br, bi = [None] * 8, [None] * 8
    for j in range(4):
        br[j], bi[j] = re[j] + re[j + 4], im[j] + im[j + 4]
        dr, di = re[j] - re[j + 4], im[j] - im[j + 4]
        if j == 0:
            br[4], bi[4] = dr, di
        elif j == 1:
            br[5], bi[5] = (dr + di) * _SQRT_HALF, (di - dr) * _SQRT_HALF
        elif j == 2:
            br[6], bi[6] = di, -dr
        else:
            br[7], bi[7] = (di - dr) * _SQRT_HALF, -(dr + di) * _SQRT_HALF
    even = _fft4(br[:4], bi[:4])
    odd = _fft4(br[4:], bi[4:])
    out = [None] * 8
    for k in range(4):
        out[2 * k], out[2 * k + 1] = even[k], odd[k]
    return [o[0] for o in out], [o[1] for o in out]


def _ifft8(re, im):
    o_im, o_re = _fft8(im, re)
    return o_re, o_im


def _filter_kernel(w1_ref, b1_ref, w2_ref, b2_ref, w3_ref, b3_ref, fr_ref, wof_ref, wob_ref, dl_ref,
                   fwd_ref, o_ref, h_scr, af_scr, ab_scr, *, seq):
    n1 = seq // _NP
    fw = _FILTER_WIDTH
    hp = lax.Precision.HIGHEST

    @pl.when(pl.program_id(1) == 0)
    def _():
        r = lax.broadcasted_iota(jnp.int32, (fw, seq), 1)
        pos = (_NP * (r % n1) + r // n1).astype(jnp.float32)
        row = lax.broadcasted_iota(jnp.int32, (fw, seq), 0)
        is_cos = (row >= 1) & (row <= _FILTER_BANDS)
        is_sin = (row > _FILTER_BANDS) & (row <= 2 * _FILTER_BANDS)
        band_idx = jnp.where(is_cos, row - 1, row - 1 - _FILTER_BANDS).astype(jnp.float32)
        band = 1e-4 + band_idx * ((_FILTER_BANDS - 1 - 1e-4) / (_FILTER_BANDS - 1))
        ang = (2.0 * math.pi / seq) * pos * band
        z = jnp.where(row == 0, pos * (1.0 / (seq - 1)),
                      jnp.where(is_cos, jnp.cos(ang), jnp.where(is_sin, -jnp.sin(ang), 0.0)))
        freq = fr_ref[0]
        h = jnp.sin(freq * (jnp.dot(w1_ref[0], z, precision=hp, preferred_element_type=jnp.float32) + b1_ref[0]))
        h = jnp.sin(freq * (jnp.dot(w2_ref[0], h, precision=hp, preferred_element_type=jnp.float32) + b2_ref[0]))
        h = jnp.sin(freq * (jnp.dot(w3_ref[0], h, precision=hp, preferred_element_type=jnp.float32) + b3_ref[0]))
        h_scr[...] = h.T

    r = lax.broadcasted_iota(jnp.int32, (seq, 1), 0)
    pos = (_NP * (r % n1) + r // n1).astype(jnp.float32)
    h = h_scr[...].astype(jnp.bfloat16)
    decay = jnp.exp(-(pos * (1.0 / (seq - 1))) * dl_ref[...])
    hf = jnp.dot(h, wof_ref[0].astype(jnp.bfloat16), preferred_element_type=jnp.float32) * decay
    hb = jnp.dot(h, wob_ref[0].astype(jnp.bfloat16), preferred_element_type=jnp.float32) * decay
    hb = jnp.where(pos == 0.0, 0.0, hb)
    hf = hf.astype(jnp.bfloat16)
    hb = hb.astype(jnp.bfloat16)
    for n2 in range(_NP):
        af_scr[n2] = jnp.dot(fwd_ref[n2], hf[n2 * n1:(n2 + 1) * n1], preferred_element_type=jnp.float32)
        ab_scr[n2] = jnp.dot(fwd_ref[n2], hb[n2 * n1:(n2 + 1) * n1], preferred_element_type=jnp.float32)

    rc = _V7X_BF16_SUBLANES

    def body(i, carry):
        r0 = pl.multiple_of(i * rc, rc)
        fr, fi = _fft8([af_scr[n2, pl.ds(r0, rc)] for n2 in range(_NP)],
                       [af_scr[n2, pl.ds(_HALF + r0, rc)] for n2 in range(_NP)])
        gr, gi = _fft8([ab_scr[n2, pl.ds(r0, rc)] for n2 in range(_NP)],
                       [ab_scr[n2, pl.ds(_HALF + r0, rc)] for n2 in range(_NP)])
        for k2 in range(_NP):
            o_ref[0, 0, k2, 0, pl.ds(r0, rc)] = fr[k2] + gr[k2]
            o_ref[0, 0, k2, 1, pl.ds(r0, rc)] = fi[k2] - gi[k2]
        return carry

    lax.fori_loop(0, _HALF // rc, body, 0)


def _filter_spectra(w1, b1, w2, b2, w3, b3, freq, w_out, fwd_tab, *, seq, ct):
    n_layers = w1.shape[0]
    c = w_out.shape[2] // (2 * _ORDER)
    n_ct = c // ct
    fw = _FILTER_WIDTH
    w1t = jnp.pad(w1, ((0, 0), (0, fw - w1.shape[1]), (0, 0))).transpose(0, 2, 1)
    col = lambda a: a.reshape(n_layers, fw, 1)
    deltas = jnp.abs(jnp.linspace(_MIN_DECAY, _MAX_DECAY, c, dtype=jnp.float32)).reshape(1, c)
    lmap3 = lambda l, s: (l, 0, 0)
    wo_spec = lambda direction: pl.BlockSpec(
        (1, fw, ct), lambda l, s: (l, 0, (2 * (s // n_ct) + direction) * n_ct + s % n_ct))
    return pl.pallas_call(
        functools.partial(_filter_kernel, seq=seq),
        grid=(n_layers, _ORDER * n_ct),
        in_specs=[
            pl.BlockSpec((1, fw, fw), lmap3), pl.BlockSpec((1, fw, 1), lmap3),
            pl.BlockSpec((1, fw, fw), lmap3), pl.BlockSpec((1, fw, 1), lmap3),
            pl.BlockSpec((1, fw, fw), lmap3), pl.BlockSpec((1, fw, 1), lmap3),
            pl.BlockSpec((1, fw, 1), lmap3),
            wo_spec(0), wo_spec(1),
            pl.BlockSpec((1, ct), lambda l, s: (0, s % n_ct)),
            pl.BlockSpec((_NP, 2 * _HALF, seq // _NP), lambda l, s: (0, 0, 0)),
        ],
        out_specs=pl.BlockSpec((1, 1, _NP, 2, _HALF, ct), lambda l, s: (l, s // n_ct, 0, 0, 0, s % n_ct)),
        out_shape=jax.ShapeDtypeStruct((n_layers, _ORDER, _NP, 2, _HALF, c), jnp.float32),
        scratch_shapes=[pltpu.VMEM((seq, fw), jnp.float32),
                        pltpu.VMEM((_NP, 2 * _HALF, ct), jnp.float32),
                        pltpu.VMEM((_NP, 2 * _HALF, ct), jnp.float32)],
        compiler_params=_cparams(("arbitrary", "arbitrary")),
        name="filter_spectra",
    )(w1t, col(b1), w2.transpose(0, 2, 1), col(b2), w3.transpose(0, 2, 1), col(b3), col(freq),
      w_out, w_out, deltas, fwd_tab)


def _hyena_kernel(v_ref, x1_ref, x2_ref, kf_ref, skip_ref, fwd_ref, inv_ref, o_ref, a_scr, c_scr, z_scr, *, seq):
    n1 = seq // _NP
    rc = _V7X_BF16_SUBLANES
    ct = o_ref.shape[-1]

    def conv_order(order, z_in_ref, gate_ref, z_out_ref):
        for n2 in range(_NP):
            zb = z_in_ref[0, n2 * n1:(n2 + 1) * n1].astype(jnp.bfloat16)
            a_scr[n2] = jnp.dot(fwd_ref[n2], zb, preferred_element_type=jnp.float32)

        def body(i, carry):
            r0 = pl.multiple_of(i * rc, rc)
            for l0 in range(0, ct, _V7X_LANES):
                ls = slice(l0, l0 + _V7X_LANES)
                xr, xi = _fft8([a_scr[n2, pl.ds(r0, rc), ls] for n2 in range(_NP)],
                               [a_scr[n2, pl.ds(_HALF + r0, rc), ls] for n2 in range(_NP)])
                yr, yi = [], []
                for k2 in range(_NP):
                    kr = kf_ref[order, k2, 0, pl.ds(r0, rc), ls]
                    ki = kf_ref[order, k2, 1, pl.ds(r0, rc), ls]
                    yr.append(xr[k2] * kr - xi[k2] * ki)
                    yi.append(xr[k2] * ki + xi[k2] * kr)
                cr, ci = _ifft8(yr, yi)
                for t2 in range(_NP):
                    c_scr[t2, pl.ds(r0, rc), ls] = cr[t2].astype(jnp.bfloat16)
                    c_scr[t2, pl.ds(_HALF + r0, rc), ls] = ci[t2].astype(jnp.bfloat16)
            return carry

        lax.fori_loop(0, _HALF // rc, body, 0)

        skip = skip_ref[order:order + 1]
        for t2 in range(_NP):
            rows = slice(t2 * n1, (t2 + 1) * n1)
            y = jnp.dot(inv_ref[t2], c_scr[t2], preferred_element_type=jnp.float32)
            z = z_in_ref[0, rows].astype(jnp.float32)
            gate = gate_ref[0, rows].astype(jnp.float32)
            z_out_ref[0, rows] = (gate * (y + z * skip)).astype(z_out_ref.dtype)

    conv_order(0, v_ref, x1_ref, z_scr)
    conv_order(1, z_scr, x2_ref, o_ref)


def _hyena(proj, kf_all, layer, skip, fwd_tab, inv_tab, *, batch, seq, ct):
    n_proj, m, c = proj.shape
    n1 = seq // _NP
    proj4 = proj.reshape(n_proj, batch, seq, c)
    pspec = lambda j: pl.BlockSpec((None, 1, seq, ct), lambda t, b, j=j: (j, b, 0, t))
    resident = dict(pipeline_mode=pl.Buffered(1))
    out = pl.pallas_call(
        functools.partial(_hyena_kernel, seq=seq),
        grid=(c // ct, batch),
        in_specs=[
            pspec(0), pspec(1), pspec(2),
            pl.BlockSpec((None, _ORDER, _NP, 2, _HALF, ct), lambda t, b: (layer, 0, 0, 0, 0, t)),
            pl.BlockSpec((_ORDER, ct), lambda t, b: (0, t)),
            pl.BlockSpec((_NP, 2 * _HALF, n1), lambda t, b: (0, 0, 0), **resident),
            pl.BlockSpec((_NP, n1, 2 * _HALF), lambda t, b: (0, 0, 0), **resident),
        ],
        out_specs=pl.BlockSpec((1, seq, ct), lambda t, b: (b, 0, t)),
        out_shape=jax.ShapeDtypeStruct((batch, seq, c), jnp.bfloat16),
        scratch_shapes=[pltpu.VMEM((_NP, 2 * _HALF, ct), jnp.float32),
                        pltpu.VMEM((_NP, 2 * _HALF, ct), jnp.bfloat16),
                        pltpu.VMEM((1, seq, ct), jnp.float32)],
        compiler_params=_cparams(("arbitrary", "arbitrary")),
        name="hyena",
    )(proj4, proj4, proj4, kf_all, skip, fwd_tab, inv_tab)
    return out.reshape(m, c)


def kernel(x, ffn1_pre_g, ffn1_w_gate, ffn1_w_up, ffn1_w_down, ffn1_post_g, mix_pre_g, mix_w_in, gmlp_ln_g, gmlp_ln_b, gmlp_w_s, gmlp_b_s, hy_conv_w, hy_conv_b, hy_filt_w1, hy_filt_b1, hy_filt_w2, hy_filt_b2, hy_filt_w3, hy_filt_b3, hy_filt_freq, hy_filt_w_out, hy_skip, mix_w_out, mix_post_g, ffn2_pre_g, ffn2_w_gate, ffn2_w_up, ffn2_w_down, ffn2_post_g):
    batch, seq, d = x.shape
    depth = ffn1_pre_g.shape[0]
    da = gmlp_ln_g.shape[1]
    assert seq % (_NP * _CHUNK) == 0 and seq // _NP + 1 <= _HALF
    m = batch * seq
    ffn_tiles = dict(tm=512, chunk=_V7X_MXU_DIM)
    ct = _V7X_MXU_DIM

    xp = x.reshape(batch, seq // _NP, _NP, d).transpose(0, 2, 1, 3).reshape(m, d)

    fwd_np, inv_np = _dft_tables(seq)
    fwd_bf16 = jnp.asarray(fwd_np).astype(jnp.bfloat16)
    inv_bf16 = jnp.asarray(inv_np).astype(jnp.bfloat16)
    kf_all = _filter_spectra(hy_filt_w1, hy_filt_b1, hy_filt_w2, hy_filt_b2, hy_filt_w3, hy_filt_b3,
                             hy_filt_freq, hy_filt_w_out, fwd_bf16, seq=seq, ct=ct)

    ffn1_w = (ffn1_w_gate, ffn1_w_up, ffn1_w_down)
    ffn2_w = (ffn2_w_gate, ffn2_w_up, ffn2_w_down)
    w_in = mix_w_in
    w_out = mix_w_out.astype(jnp.bfloat16)

    for l in range(depth):
        xp, h = _ffn(xp, l, ffn1_pre_g, *ffn1_w, ffn1_post_g, g_next=mix_pre_g, **ffn_tiles)
        ya = _gmlp(h, w_in, l, gmlp_ln_g[l], gmlp_ln_b[l], gmlp_w_s[l], gmlp_b_s[l], batch=batch, seq=seq, jb=4)
        proj = _hyproj(h, w_in, l, hy_conv_w[l], hy_conv_b[l], batch=batch, seq=seq)
        yb = _hyena(proj, kf_all, l, hy_skip[l], fwd_bf16, inv_bf16, batch=batch, seq=seq, ct=ct)
        xp = _ffn(xp, l, ffn2_pre_g, *ffn2_w, ffn2_post_g, mix=(ya, yb, w_out, mix_post_g), **ffn_tiles)

    return xp.reshape(batch, _NP, seq // _NP, d).transpose(0, 2, 1, 3).reshape(batch, seq, d)
```

```python
import functools
import math

import jax
import jax.numpy as jnp
import numpy as np
from jax import lax
from jax.experimental import pallas as pl
from jax.experimental.pallas import tpu as pltpu

_CHUNK = 128
_HEADS_A = 8
_HEAD_DIM_A = 64
_ORDER = 2
_SHORT_K = 3
_FILTER_BANDS = 16
_FILTER_WIDTH = 64
_DECAY_TARGET = 1e-2
_MAX_DECAY = math.log(_DECAY_TARGET) / 0.3
_MIN_DECAY = math.log(_DECAY_TARGET) / 1.5
_HALF_STEP = 0.5
_RMS_EPS = 1e-6
_LN_EPS = 1e-5

_V7X_LANES = 128
_V7X_BF16_SUBLANES = 16
_V7X_MXU_DIM = 256
_V7X_VMEM_LIMIT_BYTES = 60 * 1024 * 1024

_NP = 8
_HALF = 272
_SQRT_HALF = 0.7071067811865476


def _cparams(semantics):
    return pltpu.CompilerParams(dimension_semantics=semantics, vmem_limit_bytes=_V7X_VMEM_LIMIT_BYTES)


def _rms_norm(x, g):
    return x * lax.rsqrt(jnp.mean(x * x, axis=-1, keepdims=True) + _RMS_EPS) * g


def _deinterleave_rows(val, slab_scr):
    rows, d = val.shape
    n_slabs = d // _V7X_LANES
    for k in range(n_slabs):
        slab_scr[k] = val[:, k * _V7X_LANES:(k + 1) * _V7X_LANES]
    return jnp.concatenate(
        [jnp.concatenate([slab_scr[k, pl.ds(n2, rows // _NP, stride=_NP), :] for k in range(n_slabs)], axis=1)
         for n2 in range(_NP)], axis=0)


def _interleave_rows(val, slab_scr):
    rows, d = val.shape
    per = rows // _NP
    n_slabs = d // _V7X_LANES
    for n2 in range(_NP):
        for k in range(n_slabs):
            slab_scr[k, pl.ds(n2, per, stride=_NP), :] = val[n2 * per:(n2 + 1) * per, k * _V7X_LANES:(k + 1) * _V7X_LANES]
    return jnp.concatenate([slab_scr[k] for k in range(n_slabs)], axis=1)


def _ffn_kernel(*refs, chunk, mix_in, norm_out, relayout):
    refs = list(refs)
    slab_scr = refs.pop() if relayout else None
    x_ref = refs.pop(0)
    blocked_in = relayout == "interleave"
    load = (lambda r: r[0].reshape(-1, r.shape[-1])) if blocked_in else (lambda r: r[...])
    x = load(x_ref)
    if mix_in:
        ya_ref, yb_ref, wa_ref, wb_ref, gmix_ref = refs[:5]
        refs = refs[5:]
        y = jnp.dot(load(ya_ref), wa_ref[...], preferred_element_type=jnp.float32)
        y = y + jnp.dot(load(yb_ref), wb_ref[...], preferred_element_type=jnp.float32)
        x = x + _rms_norm(y, gmix_ref[...])
    gpre_ref, wg_ref, wu_ref, wd_ref, gpost_ref = refs[:5]
    refs = refs[5:]
    h = _rms_norm(x, gpre_ref[...]).astype(jnp.bfloat16)
    acc = jnp.zeros(x.shape, jnp.float32)
    dff = wg_ref.shape[1]
    for c0 in range(0, dff, chunk):
        c1 = min(c0 + chunk, dff)
        g = jnp.dot(h, wg_ref[:, c0:c1], preferred_element_type=jnp.float32)
        u = jnp.dot(h, wu_ref[:, c0:c1], preferred_element_type=jnp.float32)
        a = (g * jax.nn.sigmoid(g) * u).astype(jnp.bfloat16)
        acc = acc + jnp.dot(a, wd_ref[c0:c1, :], preferred_element_type=jnp.float32)
    out = x + _HALF_STEP * _rms_norm(acc, gpost_ref[...])
    if relayout == "deinterleave":
        out = _deinterleave_rows(out, slab_scr)
    elif relayout == "interleave":
        out = _interleave_rows(out, slab_scr)
    blocked_out = relayout == "deinterleave"
    if norm_out:
        gnext_ref, o_ref, hn_ref = refs
        hn = _rms_norm(out, gnext_ref[...]).astype(jnp.bfloat16)
        if blocked_out:
            hn_ref[0] = hn.reshape(hn_ref.shape[1:])
        else:
            hn_ref[...] = hn
    else:
        (o_ref,) = refs
    if blocked_out:
        o_ref[0] = out.reshape(o_ref.shape[1:])
    else:
        o_ref[...] = out


def _ffn(x, layer, g_pre, w_gate, w_up, w_down, g_post, *, batch, tm, chunk, mix=None, g_next=None, relayout=None):
    m, d = x.shape
    dff = w_gate.shape[2]
    seq = m // batch
    steps_per_seq = seq // tm
    assert dff % _V7X_MXU_DIM == 0 and chunk % _V7X_MXU_DIM == 0 and seq % tm == 0 and tm % (_NP * _V7X_BF16_SUBLANES) == 0
    row = lambda a: a[layer].reshape(1, d)
    flat_tile = lambda width: pl.BlockSpec((tm, width), lambda i: (i, 0))
    blocked_tile = lambda width: pl.BlockSpec((1, _NP, tm // _NP, width),
                                              lambda i: (i // steps_per_seq, 0, i % steps_per_seq, 0))
    blocked = lambda a: a.reshape(batch, _NP, seq // _NP, a.shape[-1])
    if relayout == "interleave":
        in_tile, put = blocked_tile, blocked
    else:
        in_tile, put = flat_tile, lambda a: a
    vec = pl.BlockSpec((1, d), lambda i: (0, 0))
    resident = lambda r, c, blk=0: pl.BlockSpec((None, r, c), lambda i: (layer, blk, 0), pipeline_mode=pl.Buffered(1))
    args, specs = [put(x)], [in_tile(d)]
    if mix is not None:
        ya, yb, w_out, g_mix = mix
        da = ya.shape[1]
        assert yb.shape[1] == da and w_out.shape[1:] == (2 * da, d)
        args += [put(ya), put(yb), w_out, w_out, row(g_mix)]
        specs += [in_tile(da), in_tile(da), resident(da, d, 0), resident(da, d, 1), vec]
    args += [row(g_pre), w_gate, w_up, w_down, row(g_post)]
    specs += [vec, resident(d, dff), resident(d, dff), resident(dff, d), vec]
    if relayout == "deinterleave":
        out_tile, out_dims = blocked_tile(d), (batch, _NP, seq // _NP, d)
    else:
        out_tile, out_dims = flat_tile(d), (m, d)
    out_shape = [jax.ShapeDtypeStruct(out_dims, jnp.float32)]
    out_specs = [out_tile]
    if g_next is not None:
        args.append(row(g_next))
        specs.append(vec)
        out_shape.append(jax.ShapeDtypeStruct(out_dims, jnp.bfloat16))
        out_specs.append(out_tile)
    outs = pl.pallas_call(
        functools.partial(_ffn_kernel, chunk=chunk, mix_in=mix is not None, norm_out=g_next is not None,
                          relayout=relayout),
        grid=(m // tm,),
        in_specs=specs,
        out_specs=out_specs,
        out_shape=out_shape,
        scratch_shapes=[pltpu.VMEM((d // _V7X_LANES, tm, _V7X_LANES), jnp.float32)] if relayout else [],
        compiler_params=_cparams(("arbitrary",)),
        name="ffn",
    )(*args)
    outs = [o.reshape(m, d) for o in outs]
    return outs if g_next is not None else outs[0]


def _gmlp_kernel(h_ref, w_ref, lng_ref, lnb_ref, ws_ref, bs_ref, o_ref, *, jb, da):
    rows = _CHUNK // _NP
    d = h_ref.shape[-1]
    h = h_ref[0].reshape(_NP * jb * rows, d)
    p = jnp.dot(h, w_ref[...], preferred_element_type=jnp.float32)
    p = 0.5 * p * (1.0 + lax.erf(p * _SQRT_HALF))
    u = p[:, :da]
    v = p[:, da:]
    mu = jnp.mean(v, axis=-1, keepdims=True)
    vc = v - mu
    var = jnp.mean(vc * vc, axis=-1, keepdims=True)
    vn = (vc * lax.rsqrt(var + _LN_EPS) * lng_ref[...] + lnb_ref[...]).astype(jnp.bfloat16)
    lane = lax.broadcasted_iota(jnp.int32, (_CHUNK, _V7X_LANES), 1)
    first_head = lane < _HEAD_DIM_A
    n_pairs = da // _V7X_LANES
    for j in range(jb):
        starts = [(n2 * jb + j) * rows for n2 in range(_NP)]
        vchunk = jnp.concatenate([vn[s:s + rows] for s in starts], axis=0)
        mixed = []
        for q in range(n_pairs):
            r = jnp.dot(ws_ref[q], vchunk[:, q * _V7X_LANES:(q + 1) * _V7X_LANES],
                        preferred_element_type=jnp.float32)
            mixed.append(jnp.where(first_head, r[:_CHUNK], r[_CHUNK:]))
        mixed = jnp.concatenate(mixed, axis=1) + bs_ref[...]
        for n2 in range(_NP):
            s = starts[n2]
            o_ref[0, n2, j] = (u[s:s + rows] * mixed[n2 * rows:(n2 + 1) * rows]).astype(o_ref.dtype)


def _gmlp(h, w_in, layer, ln_g, ln_b, w_s, b_s, *, batch, seq, jb):
    m, d = h.shape
    da = ln_g.shape[0]
    rows = _CHUNK // _NP
    n_chunks = seq // _CHUNK
    assert rows == _V7X_BF16_SUBLANES
    h5 = h.reshape(batch, _NP, n_chunks, rows, d)
    ws_p = w_s.reshape(_HEADS_A, rows, _NP, rows, _NP).transpose(0, 2, 1, 4, 3).reshape(_HEADS_A, _CHUNK, _CHUNK)
    ws_pairs = ws_p.reshape(_HEADS_A // 2, 2 * _CHUNK, _CHUNK).astype(jnp.bfloat16)
    bs_p = b_s.reshape(_HEADS_A, rows, _NP).transpose(0, 2, 1).reshape(_HEADS_A, _CHUNK)
    bs_full = jnp.repeat(bs_p.T, _HEAD_DIM_A, axis=1)
    out = pl.pallas_call(
        functools.partial(_gmlp_kernel, jb=jb, da=da),
        grid=(batch, n_chunks // jb),
        in_specs=[
            pl.BlockSpec((1, _NP, jb, rows, d), lambda b, j: (b, 0, j, 0, 0)),
            pl.BlockSpec((None, d, 2 * da), lambda b, j: (layer, 0, 0)),
            pl.BlockSpec((1, da), lambda b, j: (0, 0)),
            pl.BlockSpec((1, da), lambda b, j: (0, 0)),
            pl.BlockSpec((_HEADS_A // 2, 2 * _CHUNK, _CHUNK), lambda b, j: (0, 0, 0)),
            pl.BlockSpec((_CHUNK, da), lambda b, j: (0, 0)),
        ],
        out_specs=pl.BlockSpec((1, _NP, jb, rows, da), lambda b, j: (b, 0, j, 0, 0)),
        out_shape=jax.ShapeDtypeStruct((batch, _NP, n_chunks, rows, da), jnp.bfloat16),
        compiler_params=_cparams(("arbitrary", "arbitrary")),
        name="gmlp",
    )(h5, w_in, ln_g.reshape(1, da), ln_b.reshape(1, da), ws_pairs, bs_full)
    return out.reshape(m, da)


def _hyproj_kernel(h_ref, *refs, n1):
    n_proj = _ORDER + 1
    w_refs, (cw_ref, cb_ref, o_ref) = refs[:n_proj], refs[n_proj:]
    c = o_ref.shape[-1]
    h = h_ref[0]
    for j in range(n_proj):
        p = jnp.dot(h, w_refs[j][...], preferred_element_type=jnp.float32)
        blocks = [p[k * n1:(k + 1) * n1] for k in range(_NP)]
        row = lax.broadcasted_iota(jnp.int32, blocks[0].shape, 0)
        before_first = jnp.where(row == 0, 0.0, pltpu.roll(blocks[_NP - 1], 1, axis=0))
        after_last = jnp.where(row == n1 - 1, 0.0, pltpu.roll(blocks[0], n1 - 1, axis=0))
        cols = slice(j * c, (j + 1) * c)
        w0 = cw_ref[0:1, cols]
        w1 = cw_ref[1:2, cols]
        w2 = cw_ref[2:3, cols]
        bias = cb_ref[:, cols]
        for k in range(_NP):
            prev = blocks[k - 1] if k > 0 else before_first
            nxt = blocks[k + 1] if k < _NP - 1 else after_last
            o_ref[j, k * n1:(k + 1) * n1] = (((bias + prev * w0) + blocks[k] * w1) + nxt * w2).astype(o_ref.dtype)


def _hyproj(h, w_in, layer, conv_w, conv_b, *, batch, seq):
    m, d = h.shape
    n_proj = _ORDER + 1
    c = conv_w.shape[1] // n_proj
    first = (w_in.shape[2] - n_proj * c) // c
    w_spec = lambda j: pl.BlockSpec((None, d, c), lambda b: (layer, 0, first + j), pipeline_mode=pl.Buffered(1))
    return pl.pallas_call(
        functools.partial(_hyproj_kernel, n1=seq // _NP),
        grid=(batch,),
        in_specs=[pl.BlockSpec((1, seq, d), lambda b: (b, 0, 0))] + [w_spec(j) for j in range(n_proj)] + [
            pl.BlockSpec((_SHORT_K, n_proj * c), lambda b: (0, 0)),
            pl.BlockSpec((1, n_proj * c), lambda b: (0, 0)),
        ],
        out_specs=pl.BlockSpec((n_proj, seq, c), lambda b: (0, b, 0)),
        out_shape=jax.ShapeDtypeStruct((n_proj, m, c), jnp.bfloat16),
        compiler_params=_cparams(("arbitrary",)),
        name="hyproj",
    )(h.reshape(batch, seq, d), *([w_in] * n_proj), conv_w, conv_b.reshape(1, n_proj * c))


@functools.lru_cache(maxsize=None)
def _dft_tables(seq):
    n1 = seq // _NP
    n_fft = 2 * seq
    k1 = np.arange(_HALF)[None, :, None]
    t = _NP * np.arange(n1)[None, None, :] + np.arange(_NP)[:, None, None]
    theta = ((k1 * t) % n_fft) * (2.0 * math.pi / n_fft)
    valid = k1 <= n1
    cos = np.where(valid, np.cos(theta), 0.0)
    sin = np.where(valid, np.sin(theta), 0.0)
    fwd = np.concatenate([cos, -sin], axis=1)
    weight = np.where((k1 == 0) | (k1 == n1), 1.0, 2.0) / n_fft
    inv = np.concatenate([weight * cos, -weight * sin], axis=1).transpose(0, 2, 1)
    return fwd.astype(np.float32), np.ascontiguousarray(inv).astype(np.float32)


def _fft4(cr, ci):
    d0r, d0i = cr[0] + cr[2], ci[0] + ci[2]
    d1r, d1i = cr[1] + cr[3], ci[1] + ci[3]
    d2r, d2i = cr[0] - cr[2], ci[0] - ci[2]
    er, ei = cr[1] - cr[3], ci[1] - ci[3]
    d3r, d3i = ei, -er
    return [(d0r + d1r, d0i + d1i), (d2r + d3r, d2i + d3i), (d0r - d1r, d0i - d1i), (d2r - d3r, d2i - d3i)]


def _fft8(re, im):
    br, bi = [None] * 8, [None] * 8
    for j in range(4):
        br[j], bi[j] = re[j] + re[j + 4], im[j] + im[j + 4]
        dr, di = re[j] - re[j + 4], im[j] - im[j + 4]
        if j == 0:
            br[4], bi[4] = dr, di
        elif j == 1:
            br[5], bi[5] = (dr + di) * _SQRT_HALF, (di - dr) * _SQRT_HALF
        elif j == 2:
            br[6], bi[6] = di, -dr
        else:
            br[7], bi[7] = (di - dr) * _SQRT_HALF, -(dr + di) * _SQRT_HALF
    even = _fft4(br[:4], bi[:4])
    odd = _fft4(br[4:], bi[4:])
    out = [None] * 8
    for k in range(4):
        out[2 * k], out[2 * k + 1] = even[k], odd[k]
    return [o[0] for o in out], [o[1] for o in out]


def _ifft8(re, im):
    o_im, o_re = _fft8(im, re)
    return o_re, o_im


def _filter_kernel(w1_ref, b1_ref, w2_ref, b2_ref, w3_ref, b3_ref, fr_ref, wof_ref, wob_ref, dl_ref,
                   fwd_ref, o_ref, h_scr, af_scr, ab_scr, *, seq):
    n1 = seq // _NP
    fw = _FILTER_WIDTH
    hp = lax.Precision.HIGHEST

    @pl.when(pl.program_id(1) == 0)
    def _():
        r = lax.broadcasted_iota(jnp.int32, (fw, seq), 1)
        pos = (_NP * (r % n1) + r // n1).astype(jnp.float32)
        row = lax.broadcasted_iota(jnp.int32, (fw, seq), 0)
        is_cos = (row >= 1) & (row <= _FILTER_BANDS)
        is_sin = (row > _FILTER_BANDS) & (row <= 2 * _FILTER_BANDS)
        band_idx = jnp.where(is_cos, row - 1, row - 1 - _FILTER_BANDS).astype(jnp.float32)
        band = 1e-4 + band_idx * ((_FILTER_BANDS - 1 - 1e-4) / (_FILTER_BANDS - 1))
        ang = (2.0 * math.pi / seq) * pos * band
        z = jnp.where(row == 0, pos * (1.0 / (seq - 1)),
                      jnp.where(is_cos, jnp.cos(ang), jnp.where(is_sin, -jnp.sin(ang), 0.0)))
        freq = fr_ref[0]
        h = jnp.sin(freq * (jnp.dot(w1_ref[0], z, precision=hp, preferred_element_type=jnp.float32) + b1_ref[0]))
        h = jnp.sin(freq * (jnp.dot(w2_ref[0], h, precision=hp, preferred_element_type=jnp.float32) + b2_ref[0]))
        h = jnp.sin(freq * (jnp.dot(w3_ref[0], h, precision=hp, preferred_element_type=jnp.float32) + b3_ref[0]))
        h_scr[...] = h.T

    r = lax.broadcasted_iota(jnp.int32, (seq, 1), 0)
    pos = (_NP * (r % n1) + r // n1).astype(jnp.float32)
    h = h_scr[...].astype(jnp.bfloat16)
    decay = jnp.exp(-(pos * (1.0 / (seq - 1))) * dl_ref[...])
    hf = jnp.dot(h, wof_ref[0].astype(jnp.bfloat16), preferred_element_type=jnp.float32) * decay
    hb = jnp.dot(h, wob_ref[0].astype(jnp.bfloat16), preferred_element_type=jnp.float32) * decay
    hb = jnp.where(pos == 0.0, 0.0, hb)
    hf = hf.astype(jnp.bfloat16)
    hb = hb.astype(jnp.bfloat16)
    for n2 in range(_NP):
        af_scr[n2] = jnp.dot(fwd_ref[n2], hf[n2 * n1:(n2 + 1) * n1], preferred_element_type=jnp.float32)
        ab_scr[n2] = jnp.dot(fwd_ref[n2], hb[n2 * n1:(n2 + 1) * n1], preferred_element_type=jnp.float32)

    rc = _V7X_BF16_SUBLANES

    def body(i, carry):
        r0 = pl.multiple_of(i * rc, rc)
        fr, fi = _fft8([af_scr[n2, pl.ds(r0, rc)] for n2 in range(_NP)],
                       [af_scr[n2, pl.ds(_HALF + r0, rc)] for n2 in range(_NP)])
        gr, gi = _fft8([ab_scr[n2, pl.ds(r0, rc)] for n2 in range(_NP)],
                       [ab_scr[n2, pl.ds(_HALF + r0, rc)] for n2 in range(_NP)])
        for k2 in range(_NP):
            o_ref[0, 0, k2, 0, pl.ds(r0, rc)] = fr[k2] + gr[k2]
            o_ref[0, 0, k2, 1, pl.ds(r0, rc)] = fi[k2] - gi[k2]
        return carry

    lax.fori_loop(0, _HALF // rc, body, 0)


def _filter_spectra(w1, b1, w2, b2, w3, b3, freq, w_out, fwd_tab, *, seq, ct):
    n_layers = w1.shape[0]
    c = w_out.shape[2] // (2 * _ORDER)
    n_ct = c // ct
    fw = _FILTER_WIDTH
    w1t = jnp.pad(w1, ((0, 0), (0, fw - w1.shape[1]), (0, 0))).transpose(0, 2, 1)
    col = lambda a: a.reshape(n_layers, fw, 1)
    deltas = jnp.abs(jnp.linspace(_MIN_DECAY, _MAX_DECAY, c, dtype=jnp.float32)).reshape(1, c)
    lmap3 = lambda l, s: (l, 0, 0)
    wo_spec = lambda direction: pl.BlockSpec(
        (1, fw, ct), lambda l, s: (l, 0, (2 * (s // n_ct) + direction) * n_ct + s % n_ct))
    return pl.pallas_call(
        functools.partial(_filter_kernel, seq=seq),
        grid=(n_layers, _ORDER * n_ct),
        in_specs=[
            pl.BlockSpec((1, fw, fw), lmap3), pl.BlockSpec((1, fw, 1), lmap3),
            pl.BlockSpec((1, fw, fw), lmap3), pl.BlockSpec((1, fw, 1), lmap3),
            pl.BlockSpec((1, fw, fw), lmap3), pl.BlockSpec((1, fw, 1), lmap3),
            pl.BlockSpec((1, fw, 1), lmap3),
            wo_spec(0), wo_spec(1),
            pl.BlockSpec((1, ct), lambda l, s: (0, s % n_ct)),
            pl.BlockSpec((_NP, 2 * _HALF, seq // _NP), lambda l, s: (0, 0, 0)),
        ],
        out_specs=pl.BlockSpec((1, 1, _NP, 2, _HALF, ct), lambda l, s: (l, s // n_ct, 0, 0, 0, s % n_ct)),
        out_shape=jax.ShapeDtypeStruct((n_layers, _ORDER, _NP, 2, _HALF, c), jnp.float32),
        scratch_shapes=[pltpu.VMEM((seq, fw), jnp.float32),
                        pltpu.VMEM((_NP, 2 * _HALF, ct), jnp.float32),
                        pltpu.VMEM((_NP, 2 * _HALF, ct), jnp.float32)],
        compiler_params=_cparams(("arbitrary", "arbitrary")),
        name="filter_spectra",
    )(w1t, col(b1), w2.transpose(0, 2, 1), col(b2), w3.transpose(0, 2, 1), col(b3), col(freq),
      w_out, w_out, deltas, fwd_tab)


def _hyena_kernel(v_ref, x1_ref, x2_ref, kf_ref, skip_ref, fwd_ref, inv_ref, o_ref, a_scr, c_scr, z_scr, *, seq):
    n1 = seq // _NP
    rc = _V7X_BF16_SUBLANES
    ct = o_ref.shape[-1]

    def conv_order(order, z_in_ref, gate_ref, z_out_ref):
        for n2 in range(_NP):
            zb = z_in_ref[0, n2 * n1:(n2 + 1) * n1].astype(jnp.bfloat16)
            a_scr[n2] = jnp.dot(fwd_ref[n2], zb, preferred_element_type=jnp.float32)

        def body(i, carry):
            r0 = pl.multiple_of(i * rc, rc)
            for l0 in range(0, ct, _V7X_LANES):
                ls = slice(l0, l0 + _V7X_LANES)
                xr, xi = _fft8([a_scr[n2, pl.ds(r0, rc), ls] for n2 in range(_NP)],
                               [a_scr[n2, pl.ds(_HALF + r0, rc), ls] for n2 in range(_NP)])
                yr, yi = [], []
                for k2 in range(_NP):
                    kr = kf_ref[order, k2, 0, pl.ds(r0, rc), ls]
                    ki = kf_ref[order, k2, 1, pl.ds(r0, rc), ls]
                    yr.append(xr[k2] * kr - xi[k2] * ki)
                    yi.append(xr[k2] * ki + xi[k2] * kr)
                cr, ci = _ifft8(yr, yi)
                for t2 in range(_NP):
                    c_scr[t2, pl.ds(r0, rc), ls] = cr[t2].astype(jnp.bfloat16)
                    c_scr[t2, pl.ds(_HALF + r0, rc), ls] = ci[t2].astype(jnp.bfloat16)
            return carry

        lax.fori_loop(0, _HALF // rc, body, 0)

        skip = skip_ref[order:order + 1]
        for t2 in range(_NP):
            rows = slice(t2 * n1, (t2 + 1) * n1)
            y = jnp.dot(inv_ref[t2], c_scr[t2], preferred_element_type=jnp.float32)
            z = z_in_ref[0, rows].astype(jnp.float32)
            gate = gate_ref[0, rows].astype(jnp.float32)
            z_out_ref[0, rows] = (gate * (y + z * skip)).astype(z_out_ref.dtype)

    conv_order(0, v_ref, x1_ref, z_scr)
    conv_order(1, z_scr, x2_ref, o_ref)


def _hyena(proj, kf_all, layer, skip, fwd_tab, inv_tab, *, batch, seq, ct):
    n_proj, m, c = proj.shape
    n1 = seq // _NP
    proj4 = proj.reshape(n_proj, batch, seq, c)
    pspec = lambda j: pl.BlockSpec((None, 1, seq, ct), lambda t, b, j=j: (j, b, 0, t))
    resident = dict(pipeline_mode=pl.Buffered(1))
    out = pl.pallas_call(
        functools.partial(_hyena_kernel, seq=seq),
        grid=(c // ct, batch),
        in_specs=[
            pspec(0), pspec(1), pspec(2),
            pl.BlockSpec((None, _ORDER, _NP, 2, _HALF, ct), lambda t, b: (layer, 0, 0, 0, 0, t)),
            pl.BlockSpec((_ORDER, ct), lambda t, b: (0, t)),
            pl.BlockSpec((_NP, 2 * _HALF, n1), lambda t, b: (0, 0, 0), **resident),
            pl.BlockSpec((_NP, n1, 2 * _HALF), lambda t, b: (0, 0, 0), **resident),
        ],
        out_specs=pl.BlockSpec((1, seq, ct), lambda t, b: (b, 0, t)),
        out_shape=jax.ShapeDtypeStruct((batch, seq, c), jnp.bfloat16),
        scratch_shapes=[pltpu.VMEM((_NP, 2 * _HALF, ct), jnp.float32),
                        pltpu.VMEM((_NP, 2 * _HALF, ct), jnp.bfloat16),
                        pltpu.VMEM((1, seq, ct), jnp.float32)],
        compiler_params=_cparams(("arbitrary", "arbitrary")),
        name="hyena",
    )(proj4, proj4, proj4, kf_all, skip, fwd_tab, inv_tab)
    return out.reshape(m, c)


def kernel(x, ffn1_pre_g, ffn1_w_gate, ffn1_w_up, ffn1_w_down, ffn1_post_g, mix_pre_g, mix_w_in, gmlp_ln_g, gmlp_ln_b, gmlp_w_s, gmlp_b_s, hy_conv_w, hy_conv_b, hy_filt_w1, hy_filt_b1, hy_filt_w2, hy_filt_b2, hy_filt_w3, hy_filt_b3, hy_filt_freq, hy_filt_w_out, hy_skip, mix_w_out, mix_post_g, ffn2_pre_g, ffn2_w_gate, ffn2_w_up, ffn2_w_down, ffn2_post_g):
    batch, seq, d = x.shape
    depth = ffn1_pre_g.shape[0]
    da = gmlp_ln_g.shape[1]
    assert seq % (_NP * _CHUNK) == 0 and seq // _NP + 1 <= _HALF
    m = batch * seq
    ffn_tiles = dict(batch=batch, tm=512, chunk=_V7X_MXU_DIM)
    ct = _V7X_MXU_DIM
    xp = x.reshape(m, d)

    fwd_np, inv_np = _dft_tables(seq)
    fwd_bf16 = jnp.asarray(fwd_np).astype(jnp.bfloat16)
    inv_bf16 = jnp.asarray(inv_np).astype(jnp.bfloat16)
    kf_all = _filter_spectra(hy_filt_w1, hy_filt_b1, hy_filt_w2, hy_filt_b2, hy_filt_w3, hy_filt_b3,
                             hy_filt_freq, hy_filt_w_out, fwd_bf16, seq=seq, ct=ct)

    ffn1_w = (ffn1_w_gate, ffn1_w_up, ffn1_w_down)
    ffn2_w = (ffn2_w_gate, ffn2_w_up, ffn2_w_down)
    w_in = mix_w_in
    w_out = mix_w_out.astype(jnp.bfloat16)

    for l in range(depth):
        xp, h = _ffn(xp, l, ffn1_pre_g, *ffn1_w, ffn1_post_g, g_next=mix_pre_g,
                     relayout="deinterleave" if l == 0 else None, **ffn_tiles)
        ya = _gmlp(h, w_in, l, gmlp_ln_g[l], gmlp_ln_b[l], gmlp_w_s[l], gmlp_b_s[l], batch=batch, seq=seq, jb=4)
        proj = _hyproj(h, w_in, l, hy_conv_w[l], hy_conv_b[l], batch=batch, seq=seq)
        yb = _hyena(proj, kf_all, l, hy_skip[l], fwd_bf16, inv_bf16, batch=batch, seq=seq, ct=ct)
        xp = _ffn(xp, l, ffn2_pre_g, *ffn2_w, ffn2_post_g, mix=(ya, yb, w_out, mix_post_g),
                  relayout="interleave" if l == depth - 1 else None, **ffn_tiles)

    return xp.reshape(batch, seq, d)
```

```python
import functools
import math

import jax
import jax.numpy as jnp
import numpy as np
from jax import lax
from jax.experimental import pallas as pl
from jax.experimental.pallas import tpu as pltpu

_CHUNK = 128
_HEADS_A = 8
_HEAD_DIM_A = 64
_ORDER = 2
_SHORT_K = 3
_FILTER_BANDS = 16
_FILTER_WIDTH = 64
_DECAY_TARGET = 1e-2
_MAX_DECAY = math.log(_DECAY_TARGET) / 0.3
_MIN_DECAY = math.log(_DECAY_TARGET) / 1.5
_HALF_STEP = 0.5
_RMS_EPS = 1e-6
_LN_EPS = 1e-5

_V7X_LANES = 128
_V7X_F32_SUBLANES = 8
_V7X_BF16_SUBLANES = 16
_V7X_MXU_DIM = 256
_V7X_VMEM_LIMIT_BYTES = 60 * 1024 * 1024

_NP = 8
_HALF = 272
_SQRT_HALF = 0.7071067811865476


def _cparams(semantics):
    return pltpu.CompilerParams(dimension_semantics=semantics, vmem_limit_bytes=_V7X_VMEM_LIMIT_BYTES)


def _rms_norm(x, g):
    return x * lax.rsqrt(jnp.mean(x * x, axis=-1, keepdims=True) + _RMS_EPS) * g


def _deinterleave_rows(val, slab_scr):
    rows, d = val.shape
    n_slabs = d // _V7X_LANES
    for k in range(n_slabs):
        slab_scr[k] = val[:, k * _V7X_LANES:(k + 1) * _V7X_LANES]
    return jnp.concatenate(
        [jnp.concatenate([slab_scr[k, pl.ds(n2, rows // _NP, stride=_NP), :] for k in range(n_slabs)], axis=1)
         for n2 in range(_NP)], axis=0)


def _interleave_rows(val, slab_scr):
    rows, d = val.shape
    per = rows // _NP
    n_slabs = d // _V7X_LANES
    for n2 in range(_NP):
        for k in range(n_slabs):
            slab_scr[k, pl.ds(n2, per, stride=_NP), :] = val[n2 * per:(n2 + 1) * per, k * _V7X_LANES:(k + 1) * _V7X_LANES]
    return jnp.concatenate([slab_scr[k] for k in range(n_slabs)], axis=1)


def _ffn_kernel(*refs, chunk, mix_in, norm_out, relayout):
    refs = list(refs)
    slab_scr = refs.pop() if relayout else None
    x_ref = refs.pop(0)
    blocked_in = relayout == "interleave"
    load = (lambda r: r[0].reshape(-1, r.shape[-1])) if blocked_in else (lambda r: r[...])
    x = load(x_ref)
    if mix_in:
        ya_ref, yb_ref, wa_ref, wb_ref, gmix_ref = refs[:5]
        refs = refs[5:]
        y = jnp.dot(load(ya_ref), wa_ref[...], preferred_element_type=jnp.float32)
        y = y + jnp.dot(load(yb_ref), wb_ref[...], preferred_element_type=jnp.float32)
        x = x + _rms_norm(y, gmix_ref[...])
    gpre_ref, wg_ref, wu_ref, wd_ref, gpost_ref = refs[:5]
    refs = refs[5:]
    h = _rms_norm(x, gpre_ref[...]).astype(jnp.bfloat16)
    acc = jnp.zeros(x.shape, jnp.float32)
    dff = wg_ref.shape[1]
    for c0 in range(0, dff, chunk):
        c1 = min(c0 + chunk, dff)
        g = jnp.dot(h, wg_ref[:, c0:c1], preferred_element_type=jnp.float32)
        u = jnp.dot(h, wu_ref[:, c0:c1], preferred_element_type=jnp.float32)
        a = (g * jax.nn.sigmoid(g) * u).astype(jnp.bfloat16)
        acc = acc + jnp.dot(a, wd_ref[c0:c1, :], preferred_element_type=jnp.float32)
    out = x + _rms_norm(acc, gpost_ref[...])
    if relayout == "deinterleave":
        out = _deinterleave_rows(out, slab_scr)
    elif relayout == "interleave":
        out = _interleave_rows(out, slab_scr)
    blocked_out = relayout == "deinterleave"
    if norm_out:
        gnext_ref, o_ref, hn_ref = refs
        hn = _rms_norm(out, gnext_ref[...]).astype(jnp.bfloat16)
        if blocked_out:
            hn_ref[0] = hn.reshape(hn_ref.shape[1:])
        else:
            hn_ref[...] = hn
    else:
        (o_ref,) = refs
    if blocked_out:
        o_ref[0] = out.reshape(o_ref.shape[1:])
    else:
        o_ref[...] = out


def _ffn(x, layer, g_pre, w_gate, w_up, w_down, g_post, *, batch, tm, chunk, mix=None, g_next=None, relayout=None):
    m, d = x.shape
    dff = w_gate.shape[2]
    seq = m // batch
    steps_per_seq = seq // tm
    assert dff % _V7X_MXU_DIM == 0 and chunk % _V7X_MXU_DIM == 0 and seq % tm == 0 and tm % (_NP * _V7X_BF16_SUBLANES) == 0
    row = lambda a: a[layer].reshape(1, d)
    flat_tile = lambda width: pl.BlockSpec((tm, width), lambda i: (i, 0))
    blocked_tile = lambda width: pl.BlockSpec((1, _NP, tm // _NP, width),
                                              lambda i: (i // steps_per_seq, 0, i % steps_per_seq, 0))
    blocked = lambda a: a.reshape(batch, _NP, seq // _NP, a.shape[-1])
    if relayout == "interleave":
        in_tile, put = blocked_tile, blocked
    else:
        in_tile, put = flat_tile, lambda a: a
    vec = pl.BlockSpec((1, d), lambda i: (0, 0))
    resident = lambda r, c, blk=0: pl.BlockSpec((None, r, c), lambda i: (layer, blk, 0), pipeline_mode=pl.Buffered(1))
    args, specs = [put(x)], [in_tile(d)]
    if mix is not None:
        ya, yb, w_out, g_mix = mix
        da = ya.shape[1]
        assert yb.shape[1] == da and w_out.shape[1:] == (2 * da, d)
        args += [put(ya), put(yb), w_out, w_out, row(g_mix)]
        specs += [in_tile(da), in_tile(da), resident(da, d, 0), resident(da, d, 1), vec]
    args += [row(g_pre), w_gate, w_up, w_down, _HALF_STEP * row(g_post)]
    specs += [vec, resident(d, dff), resident(d, dff), resident(dff, d), vec]
    if relayout == "deinterleave":
        out_tile, out_dims = blocked_tile(d), (batch, _NP, seq // _NP, d)
    else:
        out_tile, out_dims = flat_tile(d), (m, d)
    out_shape = [jax.ShapeDtypeStruct(out_dims, jnp.float32)]
    out_specs = [out_tile]
    if g_next is not None:
        args.append(row(g_next))
        specs.append(vec)
        out_shape.append(jax.ShapeDtypeStruct(out_dims, jnp.bfloat16))
        out_specs.append(out_tile)
    outs = pl.pallas_call(
        functools.partial(_ffn_kernel, chunk=chunk, mix_in=mix is not None, norm_out=g_next is not None,
                          relayout=relayout),
        grid=(m // tm,),
        in_specs=specs,
        out_specs=out_specs,
        out_shape=out_shape,
        scratch_shapes=[pltpu.VMEM((d // _V7X_LANES, tm, _V7X_LANES), jnp.float32)] if relayout else [],
        compiler_params=_cparams(("arbitrary",)),
        name="ffn",
    )(*args)
    outs = [o.reshape(m, d) for o in outs]
    return outs if g_next is not None else outs[0]


def _gmlp_kernel(h_ref, w_ref, lng_ref, lnb_ref, ws_ref, bs_ref, o_ref, *, jb, da):
    rows = _CHUNK // _NP
    d = h_ref.shape[-1]
    h = h_ref[0].reshape(_NP * jb * rows, d)
    gelu = lambda p: 0.5 * p * (1.0 + lax.erf(p * _SQRT_HALF))
    u = gelu(jnp.dot(h, w_ref[:, :da], preferred_element_type=jnp.float32))
    v = gelu(jnp.dot(h, w_ref[:, da:], preferred_element_type=jnp.float32))
    mu = jnp.mean(v, axis=-1, keepdims=True)
    vc = v - mu
    var = jnp.mean(vc * vc, axis=-1, keepdims=True)
    vn = (vc * lax.rsqrt(var + _LN_EPS) * lng_ref[...] + lnb_ref[...]).astype(jnp.bfloat16)
    lane = lax.broadcasted_iota(jnp.int32, (_CHUNK, _V7X_LANES), 1)
    first_head = lane < _HEAD_DIM_A
    n_pairs = da // _V7X_LANES
    for j in range(jb):
        starts = [(n2 * jb + j) * rows for n2 in range(_NP)]
        vchunk = jnp.concatenate([vn[s:s + rows] for s in starts], axis=0)
        mixed = []
        for q in range(n_pairs):
            r = jnp.dot(ws_ref[q], vchunk[:, q * _V7X_LANES:(q + 1) * _V7X_LANES],
                        preferred_element_type=jnp.float32)
            mixed.append(jnp.where(first_head, r[:_CHUNK], r[_CHUNK:]))
        mixed = jnp.concatenate(mixed, axis=1) + bs_ref[...]
        for n2 in range(_NP):
            s = starts[n2]
            o_ref[0, n2, j] = (u[s:s + rows] * mixed[n2 * rows:(n2 + 1) * rows]).astype(o_ref.dtype)


def _gmlp(h, w_in, layer, ln_g, ln_b, w_s, b_s, *, batch, seq, jb):
    m, d = h.shape
    da = ln_g.shape[0]
    rows = _CHUNK // _NP
    n_chunks = seq // _CHUNK
    assert rows == _V7X_BF16_SUBLANES
    h5 = h.reshape(batch, _NP, n_chunks, rows, d)
    ws_p = w_s.reshape(_HEADS_A, rows, _NP, rows, _NP).transpose(0, 2, 1, 4, 3).reshape(_HEADS_A, _CHUNK, _CHUNK)
    ws_pairs = ws_p.reshape(_HEADS_A // 2, 2 * _CHUNK, _CHUNK).astype(jnp.bfloat16)
    bs_p = b_s.reshape(_HEADS_A, rows, _NP).transpose(0, 2, 1).reshape(_HEADS_A, _CHUNK)
    bs_full = jnp.repeat(bs_p.T, _HEAD_DIM_A, axis=1)
    out = pl.pallas_call(
        functools.partial(_gmlp_kernel, jb=jb, da=da),
        grid=(batch, n_chunks // jb),
        in_specs=[
            pl.BlockSpec((1, _NP, jb, rows, d), lambda b, j: (b, 0, j, 0, 0)),
            pl.BlockSpec((None, d, 2 * da), lambda b, j: (layer, 0, 0)),
            pl.BlockSpec((1, da), lambda b, j: (0, 0)),
            pl.BlockSpec((1, da), lambda b, j: (0, 0)),
            pl.BlockSpec((_HEADS_A // 2, 2 * _CHUNK, _CHUNK), lambda b, j: (0, 0, 0)),
            pl.BlockSpec((_CHUNK, da), lambda b, j: (0, 0)),
        ],
        out_specs=pl.BlockSpec((1, _NP, jb, rows, da), lambda b, j: (b, 0, j, 0, 0)),
        out_shape=jax.ShapeDtypeStruct((batch, _NP, n_chunks, rows, da), jnp.bfloat16),
        compiler_params=_cparams(("arbitrary", "arbitrary")),
        name="gmlp",
    )(h5, w_in, ln_g.reshape(1, da), ln_b.reshape(1, da), ws_pairs, bs_full)
    return out.reshape(m, da)


def _hyproj_kernel(h_ref, *refs, n1):
    n_proj = _ORDER + 1
    w_refs, (cw_ref, cb_ref, o_ref) = refs[:n_proj], refs[n_proj:]
    c = o_ref.shape[-1]
    h = h_ref[0]
    sub = cw_ref.shape[1]
    tiled = lambda a: a.reshape(n1 // sub, sub, c)
    for j in range(n_proj):
        p = jnp.dot(h, w_refs[j][...], preferred_element_type=jnp.float32)
        blocks = [p[k * n1:(k + 1) * n1] for k in range(_NP)]
        row = lax.broadcasted_iota(jnp.int32, blocks[0].shape, 0)
        before_first = jnp.where(row == 0, 0.0, pltpu.roll(blocks[_NP - 1], 1, axis=0))
        after_last = jnp.where(row == n1 - 1, 0.0, pltpu.roll(blocks[0], n1 - 1, axis=0))
        cols = slice(j * c, (j + 1) * c)
        w0, w1, w2 = cw_ref[0, :, cols], cw_ref[1, :, cols], cw_ref[2, :, cols]
        bias = cb_ref[:, cols]
        for k in range(_NP):
            prev = blocks[k - 1] if k > 0 else before_first
            nxt = blocks[k + 1] if k < _NP - 1 else after_last
            y = ((bias + tiled(prev) * w0) + tiled(blocks[k]) * w1) + tiled(nxt) * w2
            o_ref[j, k * n1:(k + 1) * n1] = y.reshape(n1, c).astype(o_ref.dtype)


def _hyproj(h, w_in, layer, conv_w, conv_b, *, batch, seq):
    m, d = h.shape
    n_proj = _ORDER + 1
    c = conv_w.shape[1] // n_proj
    first = (w_in.shape[2] - n_proj * c) // c
    w_spec = lambda j: pl.BlockSpec((None, d, c), lambda b: (layer, 0, first + j), pipeline_mode=pl.Buffered(1))
    sub = _V7X_F32_SUBLANES
    taps = jnp.broadcast_to(conv_w[:, None, :], (_SHORT_K, sub, n_proj * c))
    bias = jnp.broadcast_to(conv_b[None, :], (sub, n_proj * c))
    return pl.pallas_call(
        functools.partial(_hyproj_kernel, n1=seq // _NP),
        grid=(batch,),
        in_specs=[pl.BlockSpec((1, seq, d), lambda b: (b, 0, 0))] + [w_spec(j) for j in range(n_proj)] + [
            pl.BlockSpec((_SHORT_K, sub, n_proj * c), lambda b: (0, 0, 0)),
            pl.BlockSpec((sub, n_proj * c), lambda b: (0, 0)),
        ],
        out_specs=pl.BlockSpec((n_proj, seq, c), lambda b: (0, b, 0)),
        out_shape=jax.ShapeDtypeStruct((n_proj, m, c), jnp.bfloat16),
        compiler_params=_cparams(("arbitrary",)),
        name="hyproj",
    )(h.reshape(batch, seq, d), *([w_in] * n_proj), taps, bias)


@functools.lru_cache(maxsize=None)
def _dft_tables(seq):
    n1 = seq // _NP
    n_fft = 2 * seq
    k1 = np.arange(_HALF)[None, :, None]
    t = _NP * np.arange(n1)[None, None, :] + np.arange(_NP)[:, None, None]
    theta = ((k1 * t) % n_fft) * (2.0 * math.pi / n_fft)
    valid = k1 <= n1
    cos = np.where(valid, np.cos(theta), 0.0)
    sin = np.where(valid, np.sin(theta), 0.0)
    fwd = np.concatenate([cos, -sin], axis=1)
    weight = np.where((k1 == 0) | (k1 == n1), 1.0, 2.0) / n_fft
    inv = np.concatenate([weight * cos, -weight * sin], axis=1).transpose(0, 2, 1)
    return fwd.astype(np.float32), np.ascontiguousarray(inv).astype(np.float32)


def _fft4(cr, ci):
    d0r, d0i = cr[0] + cr[2], ci[0] + ci[2]
    d1r, d1i = cr[1] + cr[3], ci[1] + ci[3]
    d2r, d2i = cr[0] - cr[2], ci[0] - ci[2]
    er, ei = cr[1] - cr[3], ci[1] - ci[3]
    d3r, d3i = ei, -er
    return [(d0r + d1r, d0i + d1i), (d2r + d3r, d2i + d3i), (d0r - d1r, d0i - d1i), (d2r - d3r, d2i - d3i)]


def _fft8(re, im):
    br, bi = [None] * 8, [None] * 8
    for j in range(4):
        br[j], bi[j] = re[j] + re[j + 4], im[j] + im[j + 4]
        dr, di = re[j] - re[j + 4], im[j] - im[j + 4]
        if j == 0:
            br[4], bi[4] = dr, di
        elif j == 1:
            br[5], bi[5] = (dr + di) * _SQRT_HALF, (di - dr) * _SQRT_HALF
        elif j == 2:
            br[6], bi[6] = di, -dr
        else:
            br[7], bi[7] = (di - dr) * _SQRT_HALF, -(dr + di) * _SQRT_HALF
    even = _fft4(br[:4], bi[:4])
    odd = _fft4(br[4:], bi[4:])
    out = [None] * 8
    for k in range(4):
        out[2 * k], out[2 * k + 1] = even[k], odd[k]
    return [o[0] for o in out], [o[1] for o in out]


def _ifft8(re, im):
    o_im, o_re = _fft8(im, re)
    return o_re, o_im


def _filter_kernel(w1_ref, b1_ref, w2_ref, b2_ref, w3_ref, b3_ref, fr_ref, wof_ref, wob_ref, dl_ref,
                   fwd_ref, o_ref, h_scr, af_scr, ab_scr, *, seq):
    n1 = seq // _NP
    fw = _FILTER_WIDTH
    hp = lax.Precision.HIGHEST

    @pl.when(pl.program_id(1) == 0)
    def _():
        nb = _FILTER_BANDS
        r = lax.broadcasted_iota(jnp.int32, (nb, seq), 1)
        pos = (_NP * (r % n1) + r // n1).astype(jnp.float32)
        band_idx = lax.broadcasted_iota(jnp.int32, (nb, seq), 0).astype(jnp.float32)
        band = 1e-4 + band_idx * ((nb - 1 - 1e-4) / (nb - 1))
        ang = (2.0 * math.pi / seq) * pos * band
        first_row = lax.broadcasted_iota(jnp.int32, (fw - 2 * nb, seq), 0) == 0
        tail = jnp.where(first_row, jnp.concatenate([pos] * ((fw - 2 * nb) // nb), axis=0) * (1.0 / (seq - 1)), 0.0)
        z = jnp.concatenate([jnp.cos(ang), -jnp.sin(ang), tail], axis=0)
        freq = fr_ref[0]
        h = jnp.sin(freq * (jnp.dot(w1_ref[0], z, precision=hp, preferred_element_type=jnp.float32) + b1_ref[0]))
        h = jnp.sin(freq * (jnp.dot(w2_ref[0], h, precision=hp, preferred_element_type=jnp.float32) + b2_ref[0]))
        h = jnp.sin(freq * (jnp.dot(w3_ref[0], h, precision=hp, preferred_element_type=jnp.float32) + b3_ref[0]))
        h_scr[...] = h.T

    r = lax.broadcasted_iota(jnp.int32, (seq, 1), 0)
    pos = (_NP * (r % n1) + r // n1).astype(jnp.float32)
    h = h_scr[...].astype(jnp.bfloat16)
    decay = jnp.exp(-(pos * (1.0 / (seq - 1))) * dl_ref[...])
    hf = jnp.dot(h, wof_ref[0].astype(jnp.bfloat16), preferred_element_type=jnp.float32) * decay
    hb = jnp.dot(h, wob_ref[0].astype(jnp.bfloat16), preferred_element_type=jnp.float32) * decay
    hb = jnp.where(pos == 0.0, 0.0, hb)
    hf = hf.astype(jnp.bfloat16)
    hb = hb.astype(jnp.bfloat16)
    for n2 in range(_NP):
        af_scr[n2] = jnp.dot(fwd_ref[n2], hf[n2 * n1:(n2 + 1) * n1], preferred_element_type=jnp.float32)
        ab_scr[n2] = jnp.dot(fwd_ref[n2], hb[n2 * n1:(n2 + 1) * n1], preferred_element_type=jnp.float32)

    rc = _V7X_BF16_SUBLANES

    def body(i, carry):
        r0 = pl.multiple_of(i * rc, rc)
        fr, fi = _fft8([af_scr[n2, pl.ds(r0, rc)] for n2 in range(_NP)],
                       [af_scr[n2, pl.ds(_HALF + r0, rc)] for n2 in range(_NP)])
        gr, gi = _fft8([ab_scr[n2, pl.ds(r0, rc)] for n2 in range(_NP)],
                       [ab_scr[n2, pl.ds(_HALF + r0, rc)] for n2 in range(_NP)])
        for k2 in range(_NP):
            o_ref[0, 0, k2, 0, pl.ds(r0, rc)] = fr[k2] + gr[k2]
            o_ref[0, 0, k2, 1, pl.ds(r0, rc)] = fi[k2] - gi[k2]
        return carry

    lax.fori_loop(0, _HALF // rc, body, 0)


def _filter_spectra(w1, b1, w2, b2, w3, b3, freq, w_out, fwd_tab, *, seq, ct):
    n_layers = w1.shape[0]
    c = w_out.shape[2] // (2 * _ORDER)
    n_ct = c // ct
    fw = _FILTER_WIDTH
    w1r = jnp.concatenate([w1[:, 1:], w1[:, :1]], axis=1)
    w1t = jnp.pad(w1r, ((0, 0), (0, fw - w1.shape[1]), (0, 0))).transpose(0, 2, 1)
    col = lambda a: a.reshape(n_layers, fw, 1)
    deltas = jnp.abs(jnp.linspace(_MIN_DECAY, _MAX_DECAY, c, dtype=jnp.float32)).reshape(1, c)
    lmap3 = lambda l, s: (l, 0, 0)
    wo_spec = lambda direction: pl.BlockSpec(
        (1, fw, ct), lambda l, s: (l, 0, (2 * (s // n_ct) + direction) * n_ct + s % n_ct))
    return pl.pallas_call(
        functools.partial(_filter_kernel, seq=seq),
        grid=(n_layers, _ORDER * n_ct),
        in_specs=[
            pl.BlockSpec((1, fw, fw), lmap3), pl.BlockSpec((1, fw, 1), lmap3),
            pl.BlockSpec((1, fw, fw), lmap3), pl.BlockSpec((1, fw, 1), lmap3),
            pl.BlockSpec((1, fw, fw), lmap3), pl.BlockSpec((1, fw, 1), lmap3),
            pl.BlockSpec((1, fw, 1), lmap3),
            wo_spec(0), wo_spec(1),
            pl.BlockSpec((1, ct), lambda l, s: (0, s % n_ct)),
            pl.BlockSpec((_NP, 2 * _HALF, seq // _NP), lambda l, s: (0, 0, 0)),
        ],
        out_specs=pl.BlockSpec((1, 1, _NP, 2, _HALF, ct), lambda l, s: (l, s // n_ct, 0, 0, 0, s % n_ct)),
        out_shape=jax.ShapeDtypeStruct((n_layers, _ORDER, _NP, 2, _HALF, c), jnp.float32),
        scratch_shapes=[pltpu.VMEM((seq, fw), jnp.float32),
                        pltpu.VMEM((_NP, 2 * _HALF, ct), jnp.float32),
                        pltpu.VMEM((_NP, 2 * _HALF, ct), jnp.float32)],
        compiler_params=_cparams(("arbitrary", "arbitrary")),
        name="filter_spectra",
    )(w1t, col(b1), w2.transpose(0, 2, 1), col(b2), w3.transpose(0, 2, 1), col(b3), col(freq),
      w_out, w_out, deltas, fwd_tab)


def _hyena_kernel(v_ref, x1_ref, x2_ref, kf_ref, skip_ref, fwd_ref, inv_ref, o_ref, a_scr, c_scr, z_scr, *, seq):
    n1 = seq // _NP
    rc = _V7X_BF16_SUBLANES
    ct = o_ref.shape[-1]

    def conv_order(order, z_in_ref, gate_ref, z_out_ref):
        for n2 in range(_NP):
            zb = z_in_ref[0, n2 * n1:(n2 + 1) * n1].astype(jnp.bfloat16)
            a_scr[n2] = jnp.dot(fwd_ref[n2], zb, preferred_element_type=jnp.float32)

        def body(i, carry):
            r0 = pl.multiple_of(i * rc, rc)
            for l0 in range(0, ct, _V7X_LANES):
                ls = slice(l0, l0 + _V7X_LANES)
                xr, xi = _fft8([a_scr[n2, pl.ds(r0, rc), ls] for n2 in range(_NP)],
                               [a_scr[n2, pl.ds(_HALF + r0, rc), ls] for n2 in range(_NP)])
                yr, yi = [], []
                for k2 in range(_NP):
                    kr = kf_ref[order, k2, 0, pl.ds(r0, rc), ls]
                    ki = kf_ref[order, k2, 1, pl.ds(r0, rc), ls]
                    yr.append(xr[k2] * kr - xi[k2] * ki)
                    yi.append(xr[k2] * ki + xi[k2] * kr)
                cr, ci = _ifft8(yr, yi)
                for t2 in range(_NP):
                    c_scr[t2, pl.ds(r0, rc), ls] = cr[t2].astype(jnp.bfloat16)
                    c_scr[t2, pl.ds(_HALF + r0, rc), ls] = ci[t2].astype(jnp.bfloat16)
            return carry

        lax.fori_loop(0, _HALF // rc, body, 0)

        skip = skip_ref[order:order + 1]
        for t2 in range(_NP):
            rows = slice(t2 * n1, (t2 + 1) * n1)
            y = jnp.dot(inv_ref[t2], c_scr[t2], preferred_element_type=jnp.float32)
            z = z_in_ref[0, rows].astype(jnp.float32)
            gate = gate_ref[0, rows].astype(jnp.float32)
            z_out_ref[0, rows] = (gate * (y + z * skip)).astype(z_out_ref.dtype)

    conv_order(0, v_ref, x1_ref, z_scr)
    conv_order(1, z_scr, x2_ref, o_ref)


def _hyena(proj, kf_all, layer, skip, fwd_tab, inv_tab, *, batch, seq, ct):
    n_proj, m, c = proj.shape
    n1 = seq // _NP
    proj4 = proj.reshape(n_proj, batch, seq, c)
    pspec = lambda j: pl.BlockSpec((None, 1, seq, ct), lambda t, b, j=j: (j, b, 0, t))
    resident = dict(pipeline_mode=pl.Buffered(1))
    out = pl.pallas_call(
        functools.partial(_hyena_kernel, seq=seq),
        grid=(c // ct, batch),
        in_specs=[
            pspec(0), pspec(1), pspec(2),
            pl.BlockSpec((None, _ORDER, _NP, 2, _HALF, ct), lambda t, b: (layer, 0, 0, 0, 0, t)),
            pl.BlockSpec((_ORDER, ct), lambda t, b: (0, t)),
            pl.BlockSpec((_NP, 2 * _HALF, n1), lambda t, b: (0, 0, 0), **resident),
            pl.BlockSpec((_NP, n1, 2 * _HALF), lambda t, b: (0, 0, 0), **resident),
        ],
        out_specs=pl.BlockSpec((1, seq, ct), lambda t, b: (b, 0, t)),
        out_shape=jax.ShapeDtypeStruct((batch, seq, c), jnp.bfloat16),
        scratch_shapes=[pltpu.VMEM((_NP, 2 * _HALF, ct), jnp.float32),
                        pltpu.VMEM((_NP, 2 * _HALF, ct), jnp.bfloat16),
                        pltpu.VMEM((1, seq, ct), jnp.float32)],
        compiler_params=_cparams(("arbitrary", "arbitrary")),
        name="hyena",
    )(proj4, proj4, proj4, kf_all, skip, fwd_tab, inv_tab)
    return out.reshape(m, c)


def kernel(x, ffn1_pre_g, ffn1_w_gate, ffn1_w_up, ffn1_w_down, ffn1_post_g, mix_pre_g, mix_w_in, gmlp_ln_g, gmlp_ln_b, gmlp_w_s, gmlp_b_s, hy_conv_w, hy_conv_b, hy_filt_w1, hy_filt_b1, hy_filt_w2, hy_filt_b2, hy_filt_w3, hy_filt_b3, hy_filt_freq, hy_filt_w_out, hy_skip, mix_w_out, mix_post_g, ffn2_pre_g, ffn2_w_gate, ffn2_w_up, ffn2_w_down, ffn2_post_g):
    batch, seq, d = x.shape
    depth = ffn1_pre_g.shape[0]
    da = gmlp_ln_g.shape[1]
    assert seq % (_NP * _CHUNK) == 0 and seq // _NP + 1 <= _HALF
    m = batch * seq
    ffn_tiles = dict(batch=batch, tm=512, chunk=_V7X_MXU_DIM)
    ct = _V7X_MXU_DIM
    xp = x.reshape(m, d)

    fwd_np, inv_np = _dft_tables(seq)
    fwd_bf16 = jnp.asarray(fwd_np).astype(jnp.bfloat16)
    inv_bf16 = jnp.asarray(inv_np).astype(jnp.bfloat16)
    kf_all = _filter_spectra(hy_filt_w1, hy_filt_b1, hy_filt_w2, hy_filt_b2, hy_filt_w3, hy_filt_b3,
                             hy_filt_freq, hy_filt_w_out, fwd_bf16, seq=seq, ct=ct)

    ffn1_w = (ffn1_w_gate, ffn1_w_up, ffn1_w_down)
    ffn2_w = (ffn2_w_gate, ffn2_w_up, ffn2_w_down)
    w_in = mix_w_in
    w_out = mix_w_out.astype(jnp.bfloat16)

    for l in range(depth):
        xp, h = _ffn(xp, l, ffn1_pre_g, *ffn1_w, ffn1_post_g, g_next=mix_pre_g,
                     relayout="deinterleave" if l == 0 else None, **ffn_tiles)
        ya = _gmlp(h, w_in, l, gmlp_ln_g[l], gmlp_ln_b[l], gmlp_w_s[l], gmlp_b_s[l], batch=batch, seq=seq, jb=8)
        proj = _hyproj(h, w_in, l, hy_conv_w[l], hy_conv_b[l], batch=batch, seq=seq)
        yb = _hyena(proj, kf_all, l, hy_skip[l], fwd_bf16, inv_bf16, batch=batch, seq=seq, ct=ct)
        xp = _ffn(xp, l, ffn2_pre_g, *ffn2_w, ffn2_post_g, mix=(ya, yb, w_out, mix_post_g),
                  relayout="interleave" if l == depth - 1 else None, **ffn_tiles)

    return xp.reshape(batch, seq, d)
```

```python
import functools
import math

import jax
import jax.numpy as jnp
import numpy as np
from jax import lax
from jax.experimental import pallas as pl
from jax.experimental.pallas import tpu as pltpu

_CHUNK = 128
_HEADS_A = 8
_HEAD_DIM_A = 64
_ORDER = 2
_SHORT_K = 3
_FILTER_BANDS = 16
_FILTER_WIDTH = 64
_DECAY_TARGET = 1e-2
_MAX_DECAY = math.log(_DECAY_TARGET) / 0.3
_MIN_DECAY = math.log(_DECAY_TARGET) / 1.5
_HALF_STEP = 0.5
_RMS_EPS = 1e-6
_LN_EPS = 1e-5

_V7X_LANES = 128
_V7X_F32_SUBLANES = 8
_V7X_BF16_SUBLANES = 16
_V7X_MXU_DIM = 256
_V7X_VMEM_LIMIT_BYTES = 60 * 1024 * 1024

_NP = 8
_HALF = 272
_SQRT_HALF = 0.7071067811865476
_STAGE_TWO_DTYPE = jnp.bfloat16


def _cparams(semantics):
    return pltpu.CompilerParams(dimension_semantics=semantics, vmem_limit_bytes=_V7X_VMEM_LIMIT_BYTES)


def _rms_norm(x, g):
    return x * lax.rsqrt(jnp.mean(x * x, axis=-1, keepdims=True) + _RMS_EPS) * g


def _deinterleave_rows(val, slab_scr):
    rows, d = val.shape
    n_slabs = d // _V7X_LANES
    for k in range(n_slabs):
        slab_scr[k] = val[:, k * _V7X_LANES:(k + 1) * _V7X_LANES]
    return jnp.concatenate(
        [jnp.concatenate([slab_scr[k, pl.ds(n2, rows // _NP, stride=_NP), :] for k in range(n_slabs)], axis=1)
         for n2 in range(_NP)], axis=0)


def _interleave_rows(val, slab_scr):
    rows, d = val.shape
    per = rows // _NP
    n_slabs = d // _V7X_LANES
    for n2 in range(_NP):
        for k in range(n_slabs):
            slab_scr[k, pl.ds(n2, per, stride=_NP), :] = val[n2 * per:(n2 + 1) * per, k * _V7X_LANES:(k + 1) * _V7X_LANES]
    return jnp.concatenate([slab_scr[k] for k in range(n_slabs)], axis=1)


def _ffn_kernel(*refs, chunk, mix_in, norm_out, relayout):
    refs = list(refs)
    slab_scr = refs.pop() if relayout else None
    x_ref = refs.pop(0)
    blocked_in = relayout == "interleave"
    load = (lambda r: r[0].reshape(-1, r.shape[-1])) if blocked_in else (lambda r: r[...])
    x = load(x_ref)
    if mix_in:
        ya_ref, yb_ref, wa_ref, wb_ref, gmix_ref = refs[:5]
        refs = refs[5:]
        y = jnp.dot(load(ya_ref), wa_ref[...], preferred_element_type=jnp.float32)
        y = y + jnp.dot(load(yb_ref), wb_ref[...], preferred_element_type=jnp.float32)
        x = x + _rms_norm(y, gmix_ref[...])
    gpre_ref, wg_ref, wu_ref, wd_ref, gpost_ref = refs[:5]
    refs = refs[5:]
    h = _rms_norm(x, gpre_ref[...]).astype(jnp.bfloat16)
    acc = jnp.zeros(x.shape, jnp.float32)
    dff = wg_ref.shape[1]
    for c0 in range(0, dff, chunk):
        c1 = min(c0 + chunk, dff)
        g = jnp.dot(h, wg_ref[:, c0:c1], preferred_element_type=jnp.float32)
        u = jnp.dot(h, wu_ref[:, c0:c1], preferred_element_type=jnp.float32)
        a = (g * jax.nn.sigmoid(g) * u).astype(jnp.bfloat16)
        acc = acc + jnp.dot(a, wd_ref[c0:c1, :], preferred_element_type=jnp.float32)
    out = x + _rms_norm(acc, gpost_ref[...])
    if relayout == "deinterleave":
        out = _deinterleave_rows(out, slab_scr)
    elif relayout == "interleave":
        out = _interleave_rows(out, slab_scr)
    blocked_out = relayout == "deinterleave"
    if norm_out:
        gnext_ref, o_ref, hn_ref = refs
        hn = _rms_norm(out, gnext_ref[...]).astype(jnp.bfloat16)
        if blocked_out:
            hn_ref[0] = hn.reshape(hn_ref.shape[1:])
        else:
            hn_ref[...] = hn
    else:
        (o_ref,) = refs
    if blocked_out:
        o_ref[0] = out.reshape(o_ref.shape[1:])
    else:
        o_ref[...] = out


def _ffn(x, layer, g_pre, w_gate, w_up, w_down, g_post, *, batch, tm, chunk, mix=None, g_next=None, relayout=None):
    m, d = x.shape
    dff = w_gate.shape[2]
    seq = m // batch
    steps_per_seq = seq // tm
    assert dff % _V7X_MXU_DIM == 0 and chunk % _V7X_MXU_DIM == 0 and seq % tm == 0 and tm % (_NP * _V7X_BF16_SUBLANES) == 0
    row = lambda a: a[layer].reshape(1, d)
    flat_tile = lambda width: pl.BlockSpec((tm, width), lambda i: (i, 0))
    blocked_tile = lambda width: pl.BlockSpec((1, _NP, tm // _NP, width),
                                              lambda i: (i // steps_per_seq, 0, i % steps_per_seq, 0))
    blocked = lambda a: a.reshape(batch, _NP, seq // _NP, a.shape[-1])
    if relayout == "interleave":
        in_tile, put = blocked_tile, blocked
    else:
        in_tile, put = flat_tile, lambda a: a
    vec = pl.BlockSpec((1, d), lambda i: (0, 0))
    resident = lambda r, c, blk=0: pl.BlockSpec((None, r, c), lambda i: (layer, blk, 0), pipeline_mode=pl.Buffered(1))
    args, specs = [put(x)], [in_tile(d)]
    if mix is not None:
        ya, yb, w_out, g_mix = mix
        da = ya.shape[1]
        assert yb.shape[1] == da and w_out.shape[1:] == (2 * da, d)
        args += [put(ya), put(yb), w_out, w_out, row(g_mix)]
        specs += [in_tile(da), in_tile(da), resident(da, d, 0), resident(da, d, 1), vec]
    args += [row(g_pre), w_gate, w_up, w_down, _HALF_STEP * row(g_post)]
    specs += [vec, resident(d, dff), resident(d, dff), resident(dff, d), vec]
    if relayout == "deinterleave":
        out_tile, out_dims = blocked_tile(d), (batch, _NP, seq // _NP, d)
    else:
        out_tile, out_dims = flat_tile(d), (m, d)
    out_shape = [jax.ShapeDtypeStruct(out_dims, jnp.float32)]
    out_specs = [out_tile]
    if g_next is not None:
        args.append(row(g_next))
        specs.append(vec)
        out_shape.append(jax.ShapeDtypeStruct(out_dims, jnp.bfloat16))
        out_specs.append(out_tile)
    outs = pl.pallas_call(
        functools.partial(_ffn_kernel, chunk=chunk, mix_in=mix is not None, norm_out=g_next is not None,
                          relayout=relayout),
        grid=(m // tm,),
        in_specs=specs,
        out_specs=out_specs,
        out_shape=out_shape,
        scratch_shapes=[pltpu.VMEM((d // _V7X_LANES, tm, _V7X_LANES), jnp.float32)] if relayout else [],
        compiler_params=_cparams(("arbitrary",)),
        name="ffn",
    )(*args)
    outs = [o.reshape(m, d) for o in outs]
    return outs if g_next is not None else outs[0]


def _gmlp_kernel(h_ref, w_ref, lng_ref, lnb_ref, ws_ref, bs_ref, o_ref, *, jb, da):
    rows = _CHUNK // _NP
    d = h_ref.shape[-1]
    h = h_ref[0].reshape(_NP * jb * rows, d)
    gelu = lambda p: 0.5 * p * (1.0 + lax.erf(p * _SQRT_HALF))
    u = gelu(jnp.dot(h, w_ref[:, :da], preferred_element_type=jnp.float32))
    v = gelu(jnp.dot(h, w_ref[:, da:], preferred_element_type=jnp.float32))
    mu = jnp.mean(v, axis=-1, keepdims=True)
    vc = v - mu
    var = jnp.mean(vc * vc, axis=-1, keepdims=True)
    vn = (vc * lax.rsqrt(var + _LN_EPS) * lng_ref[...] + lnb_ref[...]).astype(jnp.bfloat16)
    lane = lax.broadcasted_iota(jnp.int32, (_CHUNK, _V7X_LANES), 1)
    first_head = lane < _HEAD_DIM_A
    n_pairs = da // _V7X_LANES
    for j in range(jb):
        starts = [(n2 * jb + j) * rows for n2 in range(_NP)]
        vchunk = jnp.concatenate([vn[s:s + rows] for s in starts], axis=0)
        mixed = []
        for q in range(n_pairs):
            r = jnp.dot(ws_ref[q], vchunk[:, q * _V7X_LANES:(q + 1) * _V7X_LANES],
                        preferred_element_type=jnp.float32)
            mixed.append(jnp.where(first_head, r[:_CHUNK], r[_CHUNK:]))
        mixed = jnp.concatenate(mixed, axis=1) + bs_ref[...]
        for n2 in range(_NP):
            s = starts[n2]
            o_ref[0, n2, j] = (u[s:s + rows] * mixed[n2 * rows:(n2 + 1) * rows]).astype(o_ref.dtype)


def _gmlp(h, w_in, layer, ln_g, ln_b, w_s, b_s, *, batch, seq, jb):
    m, d = h.shape
    da = ln_g.shape[0]
    rows = _CHUNK // _NP
    n_chunks = seq // _CHUNK
    assert rows == _V7X_BF16_SUBLANES
    h5 = h.reshape(batch, _NP, n_chunks, rows, d)
    ws_p = w_s.reshape(_HEADS_A, rows, _NP, rows, _NP).transpose(0, 2, 1, 4, 3).reshape(_HEADS_A, _CHUNK, _CHUNK)
    ws_pairs = ws_p.reshape(_HEADS_A // 2, 2 * _CHUNK, _CHUNK).astype(jnp.bfloat16)
    bs_p = b_s.reshape(_HEADS_A, rows, _NP).transpose(0, 2, 1).reshape(_HEADS_A, _CHUNK)
    bs_full = jnp.repeat(bs_p.T, _HEAD_DIM_A, axis=1)
    out = pl.pallas_call(
        functools.partial(_gmlp_kernel, jb=jb, da=da),
        grid=(batch, n_chunks // jb),
        in_specs=[
            pl.BlockSpec((1, _NP, jb, rows, d), lambda b, j: (b, 0, j, 0, 0)),
            pl.BlockSpec((None, d, 2 * da), lambda b, j: (layer, 0, 0)),
            pl.BlockSpec((1, da), lambda b, j: (0, 0)),
            pl.BlockSpec((1, da), lambda b, j: (0, 0)),
            pl.BlockSpec((_HEADS_A // 2, 2 * _CHUNK, _CHUNK), lambda b, j: (0, 0, 0)),
            pl.BlockSpec((_CHUNK, da), lambda b, j: (0, 0)),
        ],
        out_specs=pl.BlockSpec((1, _NP, jb, rows, da), lambda b, j: (b, 0, j, 0, 0)),
        out_shape=jax.ShapeDtypeStruct((batch, _NP, n_chunks, rows, da), jnp.bfloat16),
        compiler_params=_cparams(("arbitrary", "arbitrary")),
        name="gmlp",
    )(h5, w_in, ln_g.reshape(1, da), ln_b.reshape(1, da), ws_pairs, bs_full)
    return out.reshape(m, da)


def _hyproj_kernel(h_ref, *refs, n1):
    n_proj = _ORDER + 1
    w_refs, (cw_ref, cb_ref, o_ref) = refs[:n_proj], refs[n_proj:]
    c = o_ref.shape[-1]
    h = h_ref[0]
    sub = cw_ref.shape[1]
    tiled = lambda a: a.reshape(n1 // sub, sub, c)
    for j in range(n_proj):
        p = jnp.dot(h, w_refs[j][...], preferred_element_type=jnp.float32)
        blocks = [p[k * n1:(k + 1) * n1] for k in range(_NP)]
        row = lax.broadcasted_iota(jnp.int32, blocks[0].shape, 0)
        before_first = jnp.where(row == 0, 0.0, pltpu.roll(blocks[_NP - 1], 1, axis=0))
        after_last = jnp.where(row == n1 - 1, 0.0, pltpu.roll(blocks[0], n1 - 1, axis=0))
        cols = slice(j * c, (j + 1) * c)
        w0, w1, w2 = cw_ref[0, :, cols], cw_ref[1, :, cols], cw_ref[2, :, cols]
        bias = cb_ref[:, cols]
        for k in range(_NP):
            prev = blocks[k - 1] if k > 0 else before_first
            nxt = blocks[k + 1] if k < _NP - 1 else after_last
            y = ((bias + tiled(prev) * w0) + tiled(blocks[k]) * w1) + tiled(nxt) * w2
            o_ref[j, k * n1:(k + 1) * n1] = y.reshape(n1, c).astype(o_ref.dtype)


def _hyproj(h, w_in, layer, conv_w, conv_b, *, batch, seq):
    m, d = h.shape
    n_proj = _ORDER + 1
    c = conv_w.shape[1] // n_proj
    first = (w_in.shape[2] - n_proj * c) // c
    w_spec = lambda j: pl.BlockSpec((None, d, c), lambda b: (layer, 0, first + j), pipeline_mode=pl.Buffered(1))
    sub = _V7X_F32_SUBLANES
    taps = jnp.broadcast_to(conv_w[:, None, :], (_SHORT_K, sub, n_proj * c))
    bias = jnp.broadcast_to(conv_b[None, :], (sub, n_proj * c))
    return pl.pallas_call(
        functools.partial(_hyproj_kernel, n1=seq // _NP),
        grid=(batch,),
        in_specs=[pl.BlockSpec((1, seq, d), lambda b: (b, 0, 0))] + [w_spec(j) for j in range(n_proj)] + [
            pl.BlockSpec((_SHORT_K, sub, n_proj * c), lambda b: (0, 0, 0)),
            pl.BlockSpec((sub, n_proj * c), lambda b: (0, 0)),
        ],
        out_specs=pl.BlockSpec((n_proj, seq, c), lambda b: (0, b, 0)),
        out_shape=jax.ShapeDtypeStruct((n_proj, m, c), jnp.bfloat16),
        compiler_params=_cparams(("arbitrary",)),
        name="hyproj",
    )(h.reshape(batch, seq, d), *([w_in] * n_proj), taps, bias)


@functools.lru_cache(maxsize=None)
def _dft_tables(seq):
    n1 = seq // _NP
    n_fft = 2 * seq
    k1 = np.arange(_HALF)[None, :, None]
    t = _NP * np.arange(n1)[None, None, :] + np.arange(_NP)[:, None, None]
    theta = ((k1 * t) % n_fft) * (2.0 * math.pi / n_fft)
    valid = k1 <= n1
    cos = np.where(valid, np.cos(theta), 0.0)
    sin = np.where(valid, np.sin(theta), 0.0)
    fwd = np.concatenate([cos, -sin], axis=1)
    weight = np.where((k1 == 0) | (k1 == n1), 1.0, 2.0) / n_fft
    inv = np.concatenate([weight * cos, -weight * sin], axis=1).transpose(0, 2, 1)
    return fwd.astype(np.float32), np.ascontiguousarray(inv).astype(np.float32)


def _fft4(cr, ci):
    d0r, d0i = cr[0] + cr[2], ci[0] + ci[2]
    d1r, d1i = cr[1] + cr[3], ci[1] + ci[3]
    d2r, d2i = cr[0] - cr[2], ci[0] - ci[2]
    er, ei = cr[1] - cr[3], ci[1] - ci[3]
    d3r, d3i = ei, -er
    return [(d0r + d1r, d0i + d1i), (d2r + d3r, d2i + d3i), (d0r - d1r, d0i - d1i), (d2r - d3r, d2i - d3i)]


def _fft8(re, im):
    br, bi = [None] * 8, [None] * 8
    for j in range(4):
        br[j], bi[j] = re[j] + re[j + 4], im[j] + im[j + 4]
        dr, di = re[j] - re[j + 4], im[j] - im[j + 4]
        if j == 0:
            br[4], bi[4] = dr, di
        elif j == 1:
            br[5], bi[5] = (dr + di) * _SQRT_HALF, (di - dr) * _SQRT_HALF
        elif j == 2:
            br[6], bi[6] = di, -dr
        else:
            br[7], bi[7] = (di - dr) * _SQRT_HALF, -(dr + di) * _SQRT_HALF
    even = _fft4(br[:4], bi[:4])
    odd = _fft4(br[4:], bi[4:])
    out = [None] * 8
    for k in range(4):
        out[2 * k], out[2 * k + 1] = even[k], odd[k]
    return [o[0] for o in out], [o[1] for o in out]


def _ifft8(re, im):
    o_im, o_re = _fft8(im, re)
    return o_re, o_im


def _filter_kernel(w1_ref, b1_ref, w2_ref, b2_ref, w3_ref, b3_ref, fr_ref, wof_ref, wob_ref, dl_ref,
                   fwd_ref, o_ref, h_scr, af_scr, ab_scr, *, seq):
    n1 = seq // _NP
    fw = _FILTER_WIDTH
    hp = lax.Precision.HIGHEST

    @pl.when(pl.program_id(1) == 0)
    def _():
        nb = _FILTER_BANDS
        r = lax.broadcasted_iota(jnp.int32, (nb, seq), 1)
        pos = (_NP * (r % n1) + r // n1).astype(jnp.float32)
        band_idx = lax.broadcasted_iota(jnp.int32, (nb, seq), 0).astype(jnp.float32)
        band = 1e-4 + band_idx * ((nb - 1 - 1e-4) / (nb - 1))
        ang = (2.0 * math.pi / seq) * pos * band
        first_row = lax.broadcasted_iota(jnp.int32, (fw - 2 * nb, seq), 0) == 0
        tail = jnp.where(first_row, jnp.concatenate([pos] * ((fw - 2 * nb) // nb), axis=0) * (1.0 / (seq - 1)), 0.0)
        z = jnp.concatenate([jnp.cos(ang), -jnp.sin(ang), tail], axis=0)
        freq = fr_ref[0]
        h = jnp.sin(freq * (jnp.dot(w1_ref[0], z, precision=hp, preferred_element_type=jnp.float32) + b1_ref[0]))
        h = jnp.sin(freq * (jnp.dot(w2_ref[0], h, precision=hp, preferred_element_type=jnp.float32) + b2_ref[0]))
        h = jnp.sin(freq * (jnp.dot(w3_ref[0], h, precision=hp, preferred_element_type=jnp.float32) + b3_ref[0]))
        h_scr[...] = h.T

    r = lax.broadcasted_iota(jnp.int32, (seq, 1), 0)
    pos = (_NP * (r % n1) + r // n1).astype(jnp.float32)
    h = h_scr[...].astype(jnp.bfloat16)
    decay = jnp.exp(-(pos * (1.0 / (seq - 1))) * dl_ref[...])
    hf = jnp.dot(h, wof_ref[0].astype(jnp.bfloat16), preferred_element_type=jnp.float32) * decay
    hb = jnp.dot(h, wob_ref[0].astype(jnp.bfloat16), preferred_element_type=jnp.float32) * decay
    hb = jnp.where(pos == 0.0, 0.0, hb)
    hf = hf.astype(jnp.bfloat16)
    hb = hb.astype(jnp.bfloat16)
    for n2 in range(_NP):
        af_scr[n2] = jnp.dot(fwd_ref[n2], hf[n2 * n1:(n2 + 1) * n1], preferred_element_type=jnp.float32)
        ab_scr[n2] = jnp.dot(fwd_ref[n2], hb[n2 * n1:(n2 + 1) * n1], preferred_element_type=jnp.float32)

    rc = _V7X_BF16_SUBLANES

    def body(i, carry):
        r0 = pl.multiple_of(i * rc, rc)
        fr, fi = _fft8([af_scr[n2, pl.ds(r0, rc)] for n2 in range(_NP)],
                       [af_scr[n2, pl.ds(_HALF + r0, rc)] for n2 in range(_NP)])
        gr, gi = _fft8([ab_scr[n2, pl.ds(r0, rc)] for n2 in range(_NP)],
                       [ab_scr[n2, pl.ds(_HALF + r0, rc)] for n2 in range(_NP)])
        for k2 in range(_NP):
            o_ref[0, 0, k2, 0, pl.ds(r0, rc)] = (fr[k2] + gr[k2]).astype(o_ref.dtype)
            o_ref[0, 0, k2, 1, pl.ds(r0, rc)] = (fi[k2] - gi[k2]).astype(o_ref.dtype)
        return carry

    lax.fori_loop(0, _HALF // rc, body, 0)


def _filter_spectra(w1, b1, w2, b2, w3, b3, freq, w_out, fwd_tab, *, seq, ct):
    n_layers = w1.shape[0]
    c = w_out.shape[2] // (2 * _ORDER)
    n_ct = c // ct
    fw = _FILTER_WIDTH
    w1r = jnp.concatenate([w1[:, 1:], w1[:, :1]], axis=1)
    w1t = jnp.pad(w1r, ((0, 0), (0, fw - w1.shape[1]), (0, 0))).transpose(0, 2, 1)
    col = lambda a: a.reshape(n_layers, fw, 1)
    deltas = jnp.abs(jnp.linspace(_MIN_DECAY, _MAX_DECAY, c, dtype=jnp.float32)).reshape(1, c)
    lmap3 = lambda l, s: (l, 0, 0)
    wo_spec = lambda direction: pl.BlockSpec(
        (1, fw, ct), lambda l, s: (l, 0, (2 * (s // n_ct) + direction) * n_ct + s % n_ct))
    return pl.pallas_call(
        functools.partial(_filter_kernel, seq=seq),
        grid=(n_layers, _ORDER * n_ct),
        in_specs=[
            pl.BlockSpec((1, fw, fw), lmap3), pl.BlockSpec((1, fw, 1), lmap3),
            pl.BlockSpec((1, fw, fw), lmap3), pl.BlockSpec((1, fw, 1), lmap3),
            pl.BlockSpec((1, fw, fw), lmap3), pl.BlockSpec((1, fw, 1), lmap3),
            pl.BlockSpec((1, fw, 1), lmap3),
            wo_spec(0), wo_spec(1),
            pl.BlockSpec((1, ct), lambda l, s: (0, s % n_ct)),
            pl.BlockSpec((_NP, 2 * _HALF, seq // _NP), lambda l, s: (0, 0, 0)),
        ],
        out_specs=pl.BlockSpec((1, 1, _NP, 2, _HALF, ct), lambda l, s: (l, s // n_ct, 0, 0, 0, s % n_ct)),
        out_shape=jax.ShapeDtypeStruct((n_layers, _ORDER, _NP, 2, _HALF, c), _STAGE_TWO_DTYPE),
        scratch_shapes=[pltpu.VMEM((seq, fw), jnp.float32),
                        pltpu.VMEM((_NP, 2 * _HALF, ct), jnp.float32),
                        pltpu.VMEM((_NP, 2 * _HALF, ct), jnp.float32)],
        compiler_params=_cparams(("arbitrary", "arbitrary")),
        name="filter_spectra",
    )(w1t, col(b1), w2.transpose(0, 2, 1), col(b2), w3.transpose(0, 2, 1), col(b3), col(freq),
      w_out, w_out, deltas, fwd_tab)


def _hyena_kernel(v_ref, x1_ref, x2_ref, kf_ref, skip_ref, fwd_ref, inv_ref, o_ref, a_scr, c_scr, z_scr, *, seq):
    n1 = seq // _NP
    rc = _V7X_BF16_SUBLANES
    ct = o_ref.shape[-1]

    def conv_order(order, z_in_ref, gate_ref, z_out_ref):
        for n2 in range(_NP):
            zb = z_in_ref[0, n2 * n1:(n2 + 1) * n1].astype(jnp.bfloat16)
            a_scr[n2] = jnp.dot(fwd_ref[n2], zb, preferred_element_type=jnp.float32).astype(a_scr.dtype)

        def body(i, carry):
            r0 = pl.multiple_of(i * rc, rc)
            for l0 in range(0, ct, _V7X_LANES):
                ls = slice(l0, l0 + _V7X_LANES)
                xr, xi = _fft8([a_scr[n2, pl.ds(r0, rc), ls] for n2 in range(_NP)],
                               [a_scr[n2, pl.ds(_HALF + r0, rc), ls] for n2 in range(_NP)])
                yr, yi = [], []
                for k2 in range(_NP):
                    kr = kf_ref[order, k2, 0, pl.ds(r0, rc), ls]
                    ki = kf_ref[order, k2, 1, pl.ds(r0, rc), ls]
                    yr.append(xr[k2] * kr - xi[k2] * ki)
                    yi.append(xr[k2] * ki + xi[k2] * kr)
                cr, ci = _ifft8(yr, yi)
                for t2 in range(_NP):
                    c_scr[t2, pl.ds(r0, rc), ls] = cr[t2].astype(jnp.bfloat16)
                    c_scr[t2, pl.ds(_HALF + r0, rc), ls] = ci[t2].astype(jnp.bfloat16)
            return carry

        lax.fori_loop(0, _HALF // rc, body, 0)

        skip = skip_ref[order:order + 1]
        for t2 in range(_NP):
            rows = slice(t2 * n1, (t2 + 1) * n1)
            y = jnp.dot(inv_ref[t2], c_scr[t2], preferred_element_type=jnp.float32)
            z = z_in_ref[0, rows].astype(jnp.float32)
            gate = gate_ref[0, rows].astype(jnp.float32)
            z_out_ref[0, rows] = (gate * (y + z * skip)).astype(z_out_ref.dtype)

    conv_order(0, v_ref, x1_ref, z_scr)
    conv_order(1, z_scr, x2_ref, o_ref)


def _hyena(proj, kf_all, layer, skip, fwd_tab, inv_tab, *, batch, seq, ct):
    n_proj, m, c = proj.shape
    n1 = seq // _NP
    proj4 = proj.reshape(n_proj, batch, seq, c)
    pspec = lambda j: pl.BlockSpec((None, 1, seq, ct), lambda t, b, j=j: (j, b, 0, t))
    resident = dict(pipeline_mode=pl.Buffered(1))
    out = pl.pallas_call(
        functools.partial(_hyena_kernel, seq=seq),
        grid=(c // ct, batch),
        in_specs=[
            pspec(0), pspec(1), pspec(2),
            pl.BlockSpec((None, _ORDER, _NP, 2, _HALF, ct), lambda t, b: (layer, 0, 0, 0, 0, t)),
            pl.BlockSpec((_ORDER, ct), lambda t, b: (0, t)),
            pl.BlockSpec((_NP, 2 * _HALF, n1), lambda t, b: (0, 0, 0), **resident),
            pl.BlockSpec((_NP, n1, 2 * _HALF), lambda t, b: (0, 0, 0), **resident),
        ],
        out_specs=pl.BlockSpec((1, seq, ct), lambda t, b: (b, 0, t)),
        out_shape=jax.ShapeDtypeStruct((batch, seq, c), jnp.bfloat16),
        scratch_shapes=[pltpu.VMEM((_NP, 2 * _HALF, ct), _STAGE_TWO_DTYPE),
                        pltpu.VMEM((_NP, 2 * _HALF, ct), jnp.bfloat16),
                        pltpu.VMEM((1, seq, ct), jnp.float32)],
        compiler_params=_cparams(("arbitrary", "arbitrary")),
        name="hyena",
    )(proj4, proj4, proj4, kf_all, skip, fwd_tab, inv_tab)
    return out.reshape(m, c)


def kernel(x, ffn1_pre_g, ffn1_w_gate, ffn1_w_up, ffn1_w_down, ffn1_post_g, mix_pre_g, mix_w_in, gmlp_ln_g, gmlp_ln_b, gmlp_w_s, gmlp_b_s, hy_conv_w, hy_conv_b, hy_filt_w1, hy_filt_b1, hy_filt_w2, hy_filt_b2, hy_filt_w3, hy_filt_b3, hy_filt_freq, hy_filt_w_out, hy_skip, mix_w_out, mix_post_g, ffn2_pre_g, ffn2_w_gate, ffn2_w_up, ffn2_w_down, ffn2_post_g):
    batch, seq, d = x.shape
    depth = ffn1_pre_g.shape[0]
    da = gmlp_ln_g.shape[1]
    assert seq % (_NP * _CHUNK) == 0 and seq // _NP + 1 <= _HALF
    m = batch * seq
    ffn_tiles = dict(batch=batch, tm=512, chunk=_V7X_MXU_DIM)
    ct = _V7X_MXU_DIM
    xp = x.reshape(m, d)

    fwd_np, inv_np = _dft_tables(seq)
    fwd_bf16 = jnp.asarray(fwd_np).astype(jnp.bfloat16)
    inv_bf16 = jnp.asarray(inv_np).astype(jnp.bfloat16)
    kf_all = _filter_spectra(hy_filt_w1, hy_filt_b1, hy_filt_w2, hy_filt_b2, hy_filt_w3, hy_filt_b3,
                             hy_filt_freq, hy_filt_w_out, fwd_bf16, seq=seq, ct=ct)

    ffn1_w = (ffn1_w_gate, ffn1_w_up, ffn1_w_down)
    ffn2_w = (ffn2_w_gate, ffn2_w_up, ffn2_w_down)
    w_in = mix_w_in
    w_out = mix_w_out.astype(jnp.bfloat16)

    for l in range(depth):
        xp, h = _ffn(xp, l, ffn1_pre_g, *ffn1_w, ffn1_post_g, g_next=mix_pre_g,
                     relayout="deinterleave" if l == 0 else None, **ffn_tiles)
        ya = _gmlp(h, w_in, l, gmlp_ln_g[l], gmlp_ln_b[l], gmlp_w_s[l], gmlp_b_s[l], batch=batch, seq=seq, jb=8)
        proj = _hyproj(h, w_in, l, hy_conv_w[l], hy_conv_b[l], batch=batch, seq=seq)
        yb = _hyena(proj, kf_all, l, hy_skip[l], fwd_bf16, inv_bf16, batch=batch, seq=seq, ct=ct)
        xp = _ffn(xp, l, ffn2_pre_g, *ffn2_w, ffn2_post_g, mix=(ya, yb, w_out, mix_post_g),
                  relayout="interleave" if l == depth - 1 else None, **ffn_tiles)

    return xp.reshape(batch, seq, d)
```

```python
import functools
import math

import jax
import jax.numpy as jnp
import numpy as np
from jax import lax
from jax.experimental import pallas as pl
from jax.experimental.pallas import tpu as pltpu

_CHUNK = 128
_HEADS_A = 8
_HEAD_DIM_A = 64
_ORDER = 2
_SHORT_K = 3
_FILTER_BANDS = 16
_FILTER_WIDTH = 64
_DECAY_TARGET = 1e-2
_MAX_DECAY = math.log(_DECAY_TARGET) / 0.3
_MIN_DECAY = math.log(_DECAY_TARGET) / 1.5
_HALF_STEP = 0.5
_RMS_EPS = 1e-6
_LN_EPS = 1e-5

_V7X_LANES = 128
_V7X_F32_SUBLANES = 8
_V7X_BF16_SUBLANES = 16
_V7X_MXU_DIM = 256
_V7X_VMEM_LIMIT_BYTES = 60 * 1024 * 1024

_NP = 8
_HALF = 272
_SQRT_HALF = 0.7071067811865476
_STAGE_TWO_DTYPE = jnp.bfloat16


def _cparams(semantics):
    return pltpu.CompilerParams(dimension_semantics=semantics, vmem_limit_bytes=_V7X_VMEM_LIMIT_BYTES)


def _rms_norm(x, g):
    return x * lax.rsqrt(jnp.mean(x * x, axis=-1, keepdims=True) + _RMS_EPS) * g


def _deinterleave_rows(val, slab_scr):
    rows, d = val.shape
    n_slabs = d // _V7X_LANES
    for k in range(n_slabs):
        slab_scr[k] = val[:, k * _V7X_LANES:(k + 1) * _V7X_LANES]
    return jnp.concatenate(
        [jnp.concatenate([slab_scr[k, pl.ds(n2, rows // _NP, stride=_NP), :] for k in range(n_slabs)], axis=1)
         for n2 in range(_NP)], axis=0)


def _interleave_rows(val, slab_scr):
    rows, d = val.shape
    per = rows // _NP
    n_slabs = d // _V7X_LANES
    for n2 in range(_NP):
        for k in range(n_slabs):
            slab_scr[k, pl.ds(n2, per, stride=_NP), :] = val[n2 * per:(n2 + 1) * per, k * _V7X_LANES:(k + 1) * _V7X_LANES]
    return jnp.concatenate([slab_scr[k] for k in range(n_slabs)], axis=1)


def _ffn_kernel(*refs, chunk, mix_in, norm_out, relayout):
    refs = list(refs)
    slab_scr = refs.pop() if relayout else None
    x_ref = refs.pop(0)
    blocked_in = relayout == "interleave"
    load = (lambda r: r[0].reshape(-1, r.shape[-1])) if blocked_in else (lambda r: r[...])
    x = load(x_ref)
    if mix_in:
        ya_ref, yb_ref, wa_ref, wb_ref, gmix_ref = refs[:5]
        refs = refs[5:]
        y = jnp.dot(load(ya_ref), wa_ref[...], preferred_element_type=jnp.float32)
        y = y + jnp.dot(load(yb_ref), wb_ref[...], preferred_element_type=jnp.float32)
        x = x + _rms_norm(y, gmix_ref[...])
    gpre_ref, wg_ref, wu_ref, wd_ref, gpost_ref = refs[:5]
    refs = refs[5:]
    h = _rms_norm(x, gpre_ref[...]).astype(jnp.bfloat16)
    acc = jnp.zeros(x.shape, jnp.float32)
    dff = wg_ref.shape[1]
    for c0 in range(0, dff, chunk):
        c1 = min(c0 + chunk, dff)
        g = jnp.dot(h, wg_ref[:, c0:c1], preferred_element_type=jnp.float32)
        u = jnp.dot(h, wu_ref[:, c0:c1], preferred_element_type=jnp.float32)
        a = (g * jax.nn.sigmoid(g) * u).astype(jnp.bfloat16)
        acc = acc + jnp.dot(a, wd_ref[c0:c1, :], preferred_element_type=jnp.float32)
    out = x + _rms_norm(acc, gpost_ref[...])
    if relayout == "deinterleave":
        out = _deinterleave_rows(out, slab_scr)
    elif relayout == "interleave":
        out = _interleave_rows(out, slab_scr)
    blocked_out = relayout == "deinterleave"
    if norm_out:
        gnext_ref, o_ref, hn_ref = refs
        hn = _rms_norm(out, gnext_ref[...]).astype(jnp.bfloat16)
        if blocked_out:
            hn_ref[0] = hn.reshape(hn_ref.shape[1:])
        else:
            hn_ref[...] = hn
    else:
        (o_ref,) = refs
    if blocked_out:
        o_ref[0] = out.reshape(o_ref.shape[1:])
    else:
        o_ref[...] = out


def _ffn(x, layer, g_pre, w_gate, w_up, w_down, g_post, *, batch, tm, chunk, mix=None, g_next=None, relayout=None):
    m, d = x.shape
    dff = w_gate.shape[2]
    seq = m // batch
    steps_per_seq = seq // tm
    assert dff % _V7X_MXU_DIM == 0 and chunk % _V7X_MXU_DIM == 0 and seq % tm == 0 and tm % (_NP * _V7X_BF16_SUBLANES) == 0
    row = lambda a: a[layer].reshape(1, d)
    flat_tile = lambda width: pl.BlockSpec((tm, width), lambda i: (i, 0))
    blocked_tile = lambda width: pl.BlockSpec((1, _NP, tm // _NP, width),
                                              lambda i: (i // steps_per_seq, 0, i % steps_per_seq, 0))
    blocked = lambda a: a.reshape(batch, _NP, seq // _NP, a.shape[-1])
    if relayout == "interleave":
        in_tile, put = blocked_tile, blocked
    else:
        in_tile, put = flat_tile, lambda a: a
    vec = pl.BlockSpec((1, d), lambda i: (0, 0))
    resident = lambda r, c, blk=0: pl.BlockSpec((None, r, c), lambda i: (layer, blk, 0), pipeline_mode=pl.Buffered(1))
    args, specs = [put(x)], [in_tile(d)]
    if mix is not None:
        ya, yb, w_out, g_mix = mix
        da = ya.shape[1]
        assert yb.shape[1] == da and w_out.shape[1:] == (2 * da, d)
        args += [put(ya), put(yb), w_out, w_out, row(g_mix)]
        specs += [in_tile(da), in_tile(da), resident(da, d, 0), resident(da, d, 1), vec]
    args += [row(g_pre), w_gate, w_up, w_down, _HALF_STEP * row(g_post)]
    specs += [vec, resident(d, dff), resident(d, dff), resident(dff, d), vec]
    if relayout == "deinterleave":
        out_tile, out_dims = blocked_tile(d), (batch, _NP, seq // _NP, d)
    else:
        out_tile, out_dims = flat_tile(d), (m, d)
    out_shape = [jax.ShapeDtypeStruct(out_dims, jnp.float32)]
    out_specs = [out_tile]
    if g_next is not None:
        args.append(row(g_next))
        specs.append(vec)
        out_shape.append(jax.ShapeDtypeStruct(out_dims, jnp.bfloat16))
        out_specs.append(out_tile)
    outs = pl.pallas_call(
        functools.partial(_ffn_kernel, chunk=chunk, mix_in=mix is not None, norm_out=g_next is not None,
                          relayout=relayout),
        grid=(m // tm,),
        in_specs=specs,
        out_specs=out_specs,
        out_shape=out_shape,
        scratch_shapes=[pltpu.VMEM((d // _V7X_LANES, tm, _V7X_LANES), jnp.float32)] if relayout else [],
        compiler_params=_cparams(("arbitrary",)),
        name="ffn",
    )(*args)
    outs = [o.reshape(m, d) for o in outs]
    return outs if g_next is not None else outs[0]


def _gmlp_kernel(h_ref, w_ref, lng_ref, lnb_ref, ws_ref, bs_ref, o_ref, *, jb, da):
    rows = _CHUNK // _NP
    d = h_ref.shape[-1]
    h = h_ref[0].reshape(_NP * jb * rows, d)
    gelu = lambda p: 0.5 * p * (1.0 + lax.erf(p * _SQRT_HALF))
    u = gelu(jnp.dot(h, w_ref[:, :da], preferred_element_type=jnp.float32))
    v = gelu(jnp.dot(h, w_ref[:, da:], preferred_element_type=jnp.float32))
    mu = jnp.mean(v, axis=-1, keepdims=True)
    vc = v - mu
    var = jnp.mean(vc * vc, axis=-1, keepdims=True)
    vn = (vc * lax.rsqrt(var + _LN_EPS) * lng_ref[...] + lnb_ref[...]).astype(jnp.bfloat16)
    lane = lax.broadcasted_iota(jnp.int32, (_CHUNK, _V7X_LANES), 1)
    first_head = lane < _HEAD_DIM_A
    n_pairs = da // _V7X_LANES
    for j in range(jb):
        starts = [(n2 * jb + j) * rows for n2 in range(_NP)]
        vchunk = jnp.concatenate([vn[s:s + rows] for s in starts], axis=0)
        mixed = []
        for q in range(n_pairs):
            r = jnp.dot(ws_ref[q], vchunk[:, q * _V7X_LANES:(q + 1) * _V7X_LANES],
                        preferred_element_type=jnp.float32)
            mixed.append(jnp.where(first_head, r[:_CHUNK], r[_CHUNK:]))
        mixed = jnp.concatenate(mixed, axis=1) + bs_ref[...]
        for n2 in range(_NP):
            s = starts[n2]
            o_ref[0, n2, j] = (u[s:s + rows] * mixed[n2 * rows:(n2 + 1) * rows]).astype(o_ref.dtype)


def _gmlp(h, w_in, layer, ln_g, ln_b, w_s, b_s, *, batch, seq, jb):
    m, d = h.shape
    da = ln_g.shape[0]
    rows = _CHUNK // _NP
    n_chunks = seq // _CHUNK
    assert rows == _V7X_BF16_SUBLANES
    h5 = h.reshape(batch, _NP, n_chunks, rows, d)
    ws_p = w_s.reshape(_HEADS_A, rows, _NP, rows, _NP).transpose(0, 2, 1, 4, 3).reshape(_HEADS_A, _CHUNK, _CHUNK)
    ws_pairs = ws_p.reshape(_HEADS_A // 2, 2 * _CHUNK, _CHUNK).astype(jnp.bfloat16)
    bs_p = b_s.reshape(_HEADS_A, rows, _NP).transpose(0, 2, 1).reshape(_HEADS_A, _CHUNK)
    bs_full = jnp.repeat(bs_p.T, _HEAD_DIM_A, axis=1)
    out = pl.pallas_call(
        functools.partial(_gmlp_kernel, jb=jb, da=da),
        grid=(batch, n_chunks // jb),
        in_specs=[
            pl.BlockSpec((1, _NP, jb, rows, d), lambda b, j: (b, 0, j, 0, 0)),
            pl.BlockSpec((None, d, 2 * da), lambda b, j: (layer, 0, 0)),
            pl.BlockSpec((1, da), lambda b, j: (0, 0)),
            pl.BlockSpec((1, da), lambda b, j: (0, 0)),
            pl.BlockSpec((_HEADS_A // 2, 2 * _CHUNK, _CHUNK), lambda b, j: (0, 0, 0)),
            pl.BlockSpec((_CHUNK, da), lambda b, j: (0, 0)),
        ],
        out_specs=pl.BlockSpec((1, _NP, jb, rows, da), lambda b, j: (b, 0, j, 0, 0)),
        out_shape=jax.ShapeDtypeStruct((batch, _NP, n_chunks, rows, da), jnp.bfloat16),
        compiler_params=_cparams(("arbitrary", "arbitrary")),
        name="gmlp",
    )(h5, w_in, ln_g.reshape(1, da), ln_b.reshape(1, da), ws_pairs, bs_full)
    return out.reshape(m, da)


def _hyproj_kernel(h_ref, *refs, n1):
    n_proj = _ORDER + 1
    w_refs, (cw_ref, cb_ref, o_ref) = refs[:n_proj], refs[n_proj:]
    c = o_ref.shape[-1]
    h = h_ref[0]
    sub = cw_ref.shape[1]
    tiled = lambda a: a.reshape(n1 // sub, sub, c)
    for j in range(n_proj):
        p = jnp.dot(h, w_refs[j][...], preferred_element_type=jnp.float32)
        blocks = [p[k * n1:(k + 1) * n1] for k in range(_NP)]
        row = lax.broadcasted_iota(jnp.int32, blocks[0].shape, 0)
        before_first = jnp.where(row == 0, 0.0, pltpu.roll(blocks[_NP - 1], 1, axis=0))
        after_last = jnp.where(row == n1 - 1, 0.0, pltpu.roll(blocks[0], n1 - 1, axis=0))
        cols = slice(j * c, (j + 1) * c)
        w0, w1, w2 = cw_ref[0, :, cols], cw_ref[1, :, cols], cw_ref[2, :, cols]
        bias = cb_ref[:, cols]
        for k in range(_NP):
            prev = blocks[k - 1] if k > 0 else before_first
            nxt = blocks[k + 1] if k < _NP - 1 else after_last
            y = ((bias + tiled(prev) * w0) + tiled(blocks[k]) * w1) + tiled(nxt) * w2
            o_ref[j, k * n1:(k + 1) * n1] = y.reshape(n1, c).astype(o_ref.dtype)


def _hyproj(h, w_in, layer, conv_w, conv_b, *, batch, seq):
    m, d = h.shape
    n_proj = _ORDER + 1
    c = conv_w.shape[1] // n_proj
    first = (w_in.shape[2] - n_proj * c) // c
    w_spec = lambda j: pl.BlockSpec((None, d, c), lambda b: (layer, 0, first + j), pipeline_mode=pl.Buffered(1))
    sub = _V7X_F32_SUBLANES
    taps = jnp.broadcast_to(conv_w[:, None, :], (_SHORT_K, sub, n_proj * c))
    bias = jnp.broadcast_to(conv_b[None, :], (sub, n_proj * c))
    return pl.pallas_call(
        functools.partial(_hyproj_kernel, n1=seq // _NP),
        grid=(batch,),
        in_specs=[pl.BlockSpec((1, seq, d), lambda b: (b, 0, 0))] + [w_spec(j) for j in range(n_proj)] + [
            pl.BlockSpec((_SHORT_K, sub, n_proj * c), lambda b: (0, 0, 0)),
            pl.BlockSpec((sub, n_proj * c), lambda b: (0, 0)),
        ],
        out_specs=pl.BlockSpec((n_proj, seq, c), lambda b: (0, b, 0)),
        out_shape=jax.ShapeDtypeStruct((n_proj, m, c), jnp.bfloat16),
        compiler_params=_cparams(("arbitrary",)),
        name="hyproj",
    )(h.reshape(batch, seq, d), *([w_in] * n_proj), taps, bias)


@functools.lru_cache(maxsize=None)
def _dft_tables(seq):
    n1 = seq // _NP
    n_fft = 2 * seq
    k1 = np.arange(_HALF)[None, :, None]
    t = _NP * np.arange(n1)[None, None, :] + np.arange(_NP)[:, None, None]
    theta = ((k1 * t) % n_fft) * (2.0 * math.pi / n_fft)
    valid = k1 <= n1
    cos = np.where(valid, np.cos(theta), 0.0)
    sin = np.where(valid, np.sin(theta), 0.0)
    fwd = np.concatenate([cos, -sin], axis=1)
    weight = np.where((k1 == 0) | (k1 == n1), 1.0, 2.0) / n_fft
    inv = np.concatenate([weight * cos, -weight * sin], axis=1).transpose(0, 2, 1)
    return fwd.astype(np.float32), np.ascontiguousarray(inv).astype(np.float32)


def _fft4(cr, ci):
    d0r, d0i = cr[0] + cr[2], ci[0] + ci[2]
    d1r, d1i = cr[1] + cr[3], ci[1] + ci[3]
    d2r, d2i = cr[0] - cr[2], ci[0] - ci[2]
    er, ei = cr[1] - cr[3], ci[1] - ci[3]
    d3r, d3i = ei, -er
    return [(d0r + d1r, d0i + d1i), (d2r + d3r, d2i + d3i), (d0r - d1r, d0i - d1i), (d2r - d3r, d2i - d3i)]


def _fft8(re, im):
    br, bi = [None] * 8, [None] * 8
    for j in range(4):
        br[j], bi[j] = re[j] + re[j + 4], im[j] + im[j + 4]
        dr, di = re[j] - re[j + 4], im[j] - im[j + 4]
        if j == 0:
            br[4], bi[4] = dr, di
        elif j == 1:
            br[5], bi[5] = (dr + di) * _SQRT_HALF, (di - dr) * _SQRT_HALF
        elif j == 2:
            br[6], bi[6] = di, -dr
        else:
            br[7], bi[7] = (di - dr) * _SQRT_HALF, -(dr + di) * _SQRT_HALF
    even = _fft4(br[:4], bi[:4])
    odd = _fft4(br[4:], bi[4:])
    out = [None] * 8
    for k in range(4):
        out[2 * k], out[2 * k + 1] = even[k], odd[k]
    return [o[0] for o in out], [o[1] for o in out]


def _ifft8(re, im):
    o_im, o_re = _fft8(im, re)
    return o_re, o_im


def _filter_kernel(w1_ref, b1_ref, w2_ref, b2_ref, w3_ref, b3_ref, fr_ref, wof_ref, wob_ref, dl_ref,
                   fwd_ref, o_ref, h_scr, af_scr, ab_scr, *, seq):
    n1 = seq // _NP
    fw = _FILTER_WIDTH
    hp = lax.Precision.HIGHEST

    @pl.when(pl.program_id(1) == 0)
    def _():
        nb = _FILTER_BANDS
        r = lax.broadcasted_iota(jnp.int32, (nb, seq), 1)
        pos = (_NP * (r % n1) + r // n1).astype(jnp.float32)
        band_idx = lax.broadcasted_iota(jnp.int32, (nb, seq), 0).astype(jnp.float32)
        band = 1e-4 + band_idx * ((nb - 1 - 1e-4) / (nb - 1))
        ang = (2.0 * math.pi / seq) * pos * band
        first_row = lax.broadcasted_iota(jnp.int32, (fw - 2 * nb, seq), 0) == 0
        tail = jnp.where(first_row, jnp.concatenate([pos] * ((fw - 2 * nb) // nb), axis=0) * (1.0 / (seq - 1)), 0.0)
        z = jnp.concatenate([jnp.cos(ang), -jnp.sin(ang), tail], axis=0)
        freq = fr_ref[0]
        h = jnp.sin(freq * (jnp.dot(w1_ref[0], z, precision=hp, preferred_element_type=jnp.float32) + b1_ref[0]))
        h = jnp.sin(freq * (jnp.dot(w2_ref[0], h, precision=hp, preferred_element_type=jnp.float32) + b2_ref[0]))
        h = jnp.sin(freq * (jnp.dot(w3_ref[0], h, precision=hp, preferred_element_type=jnp.float32) + b3_ref[0]))
        h_scr[...] = h.T

    r = lax.broadcasted_iota(jnp.int32, (seq, 1), 0)
    pos = (_NP * (r % n1) + r // n1).astype(jnp.float32)
    h = h_scr[...].astype(jnp.bfloat16)
    decay = jnp.exp(-(pos * (1.0 / (seq - 1))) * dl_ref[...])
    hf = jnp.dot(h, wof_ref[0].astype(jnp.bfloat16), preferred_element_type=jnp.float32) * decay
    hb = jnp.dot(h, wob_ref[0].astype(jnp.bfloat16), preferred_element_type=jnp.float32) * decay
    hb = jnp.where(pos == 0.0, 0.0, hb)
    hf = hf.astype(jnp.bfloat16)
    hb = hb.astype(jnp.bfloat16)
    for n2 in range(_NP):
        af_scr[n2] = jnp.dot(fwd_ref[n2], hf[n2 * n1:(n2 + 1) * n1], preferred_element_type=jnp.float32)
        ab_scr[n2] = jnp.dot(fwd_ref[n2], hb[n2 * n1:(n2 + 1) * n1], preferred_element_type=jnp.float32)

    rc = _V7X_BF16_SUBLANES

    def body(i, carry):
        r0 = pl.multiple_of(i * rc, rc)
        fr, fi = _fft8([af_scr[n2, pl.ds(r0, rc)] for n2 in range(_NP)],
                       [af_scr[n2, pl.ds(_HALF + r0, rc)] for n2 in range(_NP)])
        gr, gi = _fft8([ab_scr[n2, pl.ds(r0, rc)] for n2 in range(_NP)],
                       [ab_scr[n2, pl.ds(_HALF + r0, rc)] for n2 in range(_NP)])
        for k2 in range(_NP):
            o_ref[0, 0, k2, 0, pl.ds(r0, rc)] = (fr[k2] + gr[k2]).astype(o_ref.dtype)
            o_ref[0, 0, k2, 1, pl.ds(r0, rc)] = (fi[k2] - gi[k2]).astype(o_ref.dtype)
        return carry

    lax.fori_loop(0, _HALF // rc, body, 0)


def _filter_spectra(w1, b1, w2, b2, w3, b3, freq, w_out, fwd_tab, *, seq, ct):
    n_layers = w1.shape[0]
    c = w_out.shape[2] // (2 * _ORDER)
    n_ct = c // ct
    fw = _FILTER_WIDTH
    w1r = jnp.concatenate([w1[:, 1:], w1[:, :1]], axis=1)
    w1t = jnp.pad(w1r, ((0, 0), (0, fw - w1.shape[1]), (0, 0))).transpose(0, 2, 1)
    col = lambda a: a.reshape(n_layers, fw, 1)
    deltas = jnp.abs(jnp.linspace(_MIN_DECAY, _MAX_DECAY, c, dtype=jnp.float32)).reshape(1, c)
    lmap3 = lambda l, s: (l, 0, 0)
    wo_spec = lambda direction: pl.BlockSpec(
        (1, fw, ct), lambda l, s: (l, 0, (2 * (s // n_ct) + direction) * n_ct + s % n_ct))
    return pl.pallas_call(
        functools.partial(_filter_kernel, seq=seq),
        grid=(n_layers, _ORDER * n_ct),
        in_specs=[
            pl.BlockSpec((1, fw, fw), lmap3), pl.BlockSpec((1, fw, 1), lmap3),
            pl.BlockSpec((1, fw, fw), lmap3), pl.BlockSpec((1, fw, 1), lmap3),
            pl.BlockSpec((1, fw, fw), lmap3), pl.BlockSpec((1, fw, 1), lmap3),
            pl.BlockSpec((1, fw, 1), lmap3),
            wo_spec(0), wo_spec(1),
            pl.BlockSpec((1, ct), lambda l, s: (0, s % n_ct)),
            pl.BlockSpec((_NP, 2 * _HALF, seq // _NP), lambda l, s: (0, 0, 0)),
        ],
        out_specs=pl.BlockSpec((1, 1, _NP, 2, _HALF, ct), lambda l, s: (l, s // n_ct, 0, 0, 0, s % n_ct)),
        out_shape=jax.ShapeDtypeStruct((n_layers, _ORDER, _NP, 2, _HALF, c), _STAGE_TWO_DTYPE),
        scratch_shapes=[pltpu.VMEM((seq, fw), jnp.float32),
                        pltpu.VMEM((_NP, 2 * _HALF, ct), jnp.float32),
                        pltpu.VMEM((_NP, 2 * _HALF, ct), jnp.float32)],
        compiler_params=_cparams(("arbitrary", "arbitrary")),
        name="filter_spectra",
    )(w1t, col(b1), w2.transpose(0, 2, 1), col(b2), w3.transpose(0, 2, 1), col(b3), col(freq),
      w_out, w_out, deltas, fwd_tab)


def _hyena_kernel(v_ref, x1_ref, x2_ref, kf_ref, skip_ref, fwd_ref, inv_ref, o_ref, a_scr, c_scr, z_scr, *, seq):
    n1 = seq // _NP
    rc = _V7X_BF16_SUBLANES
    ct = o_ref.shape[-1]

    def conv_order(order, z_in_ref, gate_ref, z_out_ref):
        for n2 in range(_NP):
            zb = z_in_ref[0, n2 * n1:(n2 + 1) * n1].astype(jnp.bfloat16)
            a_scr[n2] = jnp.dot(fwd_ref[n2], zb, preferred_element_type=jnp.float32).astype(a_scr.dtype)

        def body(i, carry):
            r0 = pl.multiple_of(i * rc, rc)
            for l0 in range(0, ct, _V7X_LANES):
                ls = slice(l0, l0 + _V7X_LANES)
                xr, xi = _fft8([a_scr[n2, pl.ds(r0, rc), ls] for n2 in range(_NP)],
                               [a_scr[n2, pl.ds(_HALF + r0, rc), ls] for n2 in range(_NP)])
                yr, yi = [], []
                for k2 in range(_NP):
                    kr = kf_ref[order, k2, 0, pl.ds(r0, rc), ls]
                    ki = kf_ref[order, k2, 1, pl.ds(r0, rc), ls]
                    yr.append(xr[k2] * kr - xi[k2] * ki)
                    yi.append(xr[k2] * ki + xi[k2] * kr)
                cr, ci = _ifft8(yr, yi)
                for t2 in range(_NP):
                    c_scr[t2, pl.ds(r0, rc), ls] = cr[t2].astype(jnp.bfloat16)
                    c_scr[t2, pl.ds(_HALF + r0, rc), ls] = ci[t2].astype(jnp.bfloat16)
            return carry

        lax.fori_loop(0, _HALF // rc, body, 0)

        skip = skip_ref[order:order + 1]
        for t2 in range(_NP):
            rows = slice(t2 * n1, (t2 + 1) * n1)
            y = jnp.dot(inv_ref[t2], c_scr[t2], preferred_element_type=jnp.float32)
            z = z_in_ref[0, rows].astype(jnp.float32)
            gate = gate_ref[0, rows].astype(jnp.float32)
            z_out_ref[0, rows] = (gate * (y + z * skip)).astype(z_out_ref.dtype)

    conv_order(0, v_ref, x1_ref, z_scr)
    conv_order(1, z_scr, x2_ref, o_ref)


def _hyena(proj, kf_all, layer, skip, fwd_tab, inv_tab, *, batch, seq, ct):
    n_proj, m, c = proj.shape
    n1 = seq // _NP
    proj4 = proj.reshape(n_proj, batch, seq, c)
    pspec = lambda j: pl.BlockSpec((None, 1, seq, ct), lambda t, b, j=j: (j, b, 0, t))
    resident = dict(pipeline_mode=pl.Buffered(1))
    out = pl.pallas_call(
        functools.partial(_hyena_kernel, seq=seq),
        grid=(c // ct, batch),
        in_specs=[
            pspec(0), pspec(1), pspec(2),
            pl.BlockSpec((None, _ORDER, _NP, 2, _HALF, ct), lambda t, b: (layer, 0, 0, 0, 0, t), **resident),
            pl.BlockSpec((_ORDER, ct), lambda t, b: (0, t)),
            pl.BlockSpec((_NP, 2 * _HALF, n1), lambda t, b: (0, 0, 0), **resident),
            pl.BlockSpec((_NP, n1, 2 * _HALF), lambda t, b: (0, 0, 0), **resident),
        ],
        out_specs=pl.BlockSpec((1, seq, ct), lambda t, b: (b, 0, t)),
        out_shape=jax.ShapeDtypeStruct((batch, seq, c), jnp.bfloat16),
        scratch_shapes=[pltpu.VMEM((_NP, 2 * _HALF, ct), _STAGE_TWO_DTYPE),
                        pltpu.VMEM((_NP, 2 * _HALF, ct), jnp.bfloat16),
                        pltpu.VMEM((1, seq, ct), jnp.float32)],
        compiler_params=_cparams(("arbitrary", "arbitrary")),
        name="hyena",
    )(proj4, proj4, proj4, kf_all, skip, fwd_tab, inv_tab)
    return out.reshape(m, c)


def kernel(x, ffn1_pre_g, ffn1_w_gate, ffn1_w_up, ffn1_w_down, ffn1_post_g, mix_pre_g, mix_w_in, gmlp_ln_g, gmlp_ln_b, gmlp_w_s, gmlp_b_s, hy_conv_w, hy_conv_b, hy_filt_w1, hy_filt_b1, hy_filt_w2, hy_filt_b2, hy_filt_w3, hy_filt_b3, hy_filt_freq, hy_filt_w_out, hy_skip, mix_w_out, mix_post_g, ffn2_pre_g, ffn2_w_gate, ffn2_w_up, ffn2_w_down, ffn2_post_g):
    batch, seq, d = x.shape
    depth = ffn1_pre_g.shape[0]
    da = gmlp_ln_g.shape[1]
    assert seq % (_NP * _CHUNK) == 0 and seq // _NP + 1 <= _HALF
    m = batch * seq
    ffn_tiles = dict(batch=batch, tm=512, chunk=_V7X_MXU_DIM)
    ct = _V7X_MXU_DIM
    xp = x.reshape(m, d)

    fwd_np, inv_np = _dft_tables(seq)
    fwd_bf16 = jnp.asarray(fwd_np).astype(jnp.bfloat16)
    inv_bf16 = jnp.asarray(inv_np).astype(jnp.bfloat16)
    kf_all = _filter_spectra(hy_filt_w1, hy_filt_b1, hy_filt_w2, hy_filt_b2, hy_filt_w3, hy_filt_b3,
                             hy_filt_freq, hy_filt_w_out, fwd_bf16, seq=seq, ct=ct)

    ffn1_w = (ffn1_w_gate, ffn1_w_up, ffn1_w_down)
    ffn2_w = (ffn2_w_gate, ffn2_w_up, ffn2_w_down)
    w_in = mix_w_in
    w_out = mix_w_out.astype(jnp.bfloat16)

    for l in range(depth):
        xp, h = _ffn(xp, l, ffn1_pre_g, *ffn1_w, ffn1_post_g, g_next=mix_pre_g,
                     relayout="deinterleave" if l == 0 else None, **ffn_tiles)
        ya = _gmlp(h, w_in, l, gmlp_ln_g[l], gmlp_ln_b[l], gmlp_w_s[l], gmlp_b_s[l], batch=batch, seq=seq, jb=8)
        proj = _hyproj(h, w_in, l, hy_conv_w[l], hy_conv_b[l], batch=batch, seq=seq)
        yb = _hyena(proj, kf_all, l, hy_skip[l], fwd_bf16, inv_bf16, batch=batch, seq=seq, ct=proj.shape[-1])
        xp = _ffn(xp, l, ffn2_pre_g, *ffn2_w, ffn2_post_g, mix=(ya, yb, w_out, mix_post_g),
                  relayout="interleave" if l == depth - 1 else None, **ffn_tiles)

    return xp.reshape(batch, seq, d)
```

```python
import functools
import math

import jax
import jax.numpy as jnp
import numpy as np
from jax import lax
from jax.experimental import pallas as pl
from jax.experimental.pallas import tpu as pltpu

_CHUNK = 128
_HEADS_A = 8
_HEAD_DIM_A = 64
_ORDER = 2
_SHORT_K = 3
_FILTER_BANDS = 16
_FILTER_WIDTH = 64
_DECAY_TARGET = 1e-2
_MAX_DECAY = math.log(_DECAY_TARGET) / 0.3
_MIN_DECAY = math.log(_DECAY_TARGET) / 1.5
_HALF_STEP = 0.5
_RMS_EPS = 1e-6
_LN_EPS = 1e-5

_V7X_LANES = 128
_V7X_F32_SUBLANES = 8
_V7X_BF16_SUBLANES = 16
_V7X_MXU_DIM = 256
_V7X_VMEM_LIMIT_BYTES = 60 * 1024 * 1024

_NP = 8
_HALF = 272
_SQRT_HALF = 0.7071067811865476
_STAGE_TWO_DTYPE = jnp.float32


def _cparams(semantics):
    return pltpu.CompilerParams(dimension_semantics=semantics, vmem_limit_bytes=_V7X_VMEM_LIMIT_BYTES)


def _rms_norm(x, g):
    return x * lax.rsqrt(jnp.mean(x * x, axis=-1, keepdims=True) + _RMS_EPS) * g


def _deinterleave_rows(val, slab_scr):
    rows, d = val.shape
    n_slabs = d // _V7X_LANES
    for k in range(n_slabs):
        slab_scr[k] = val[:, k * _V7X_LANES:(k + 1) * _V7X_LANES]
    return jnp.concatenate(
        [jnp.concatenate([slab_scr[k, pl.ds(n2, rows // _NP, stride=_NP), :] for k in range(n_slabs)], axis=1)
         for n2 in range(_NP)], axis=0)


def _interleave_rows(val, slab_scr):
    rows, d = val.shape
    per = rows // _NP
    n_slabs = d // _V7X_LANES
    for n2 in range(_NP):
        for k in range(n_slabs):
            slab_scr[k, pl.ds(n2, per, stride=_NP), :] = val[n2 * per:(n2 + 1) * per, k * _V7X_LANES:(k + 1) * _V7X_LANES]
    return jnp.concatenate([slab_scr[k] for k in range(n_slabs)], axis=1)


def _ffn_kernel(*refs, chunk, mix_in, norm_out, relayout):
    refs = list(refs)
    slab_scr = refs.pop() if relayout else None
    x_ref = refs.pop(0)
    blocked_in = relayout == "interleave"
    load = (lambda r: r[0].reshape(-1, r.shape[-1])) if blocked_in else (lambda r: r[...])
    x = load(x_ref)
    if mix_in:
        ya_ref, yb_ref, wa_ref, wb_ref, gmix_ref = refs[:5]
        refs = refs[5:]
        y = jnp.dot(load(ya_ref), wa_ref[...], preferred_element_type=jnp.float32)
        y = y + jnp.dot(load(yb_ref), wb_ref[...], preferred_element_type=jnp.float32)
        x = x + _rms_norm(y, gmix_ref[...])
    gpre_ref, wg_ref, wu_ref, wd_ref, gpost_ref = refs[:5]
    refs = refs[5:]
    h = _rms_norm(x, gpre_ref[...]).astype(jnp.bfloat16)
    acc = jnp.zeros(x.shape, jnp.float32)
    dff = wg_ref.shape[1]
    for c0 in range(0, dff, chunk):
        c1 = min(c0 + chunk, dff)
        g = jnp.dot(h, wg_ref[:, c0:c1], preferred_element_type=jnp.float32)
        u = jnp.dot(h, wu_ref[:, c0:c1], preferred_element_type=jnp.float32)
        a = (g * jax.nn.sigmoid(g) * u).astype(jnp.bfloat16)
        acc = acc + jnp.dot(a, wd_ref[c0:c1, :], preferred_element_type=jnp.float32)
    out = x + _rms_norm(acc, gpost_ref[...])
    if relayout == "deinterleave":
        out = _deinterleave_rows(out, slab_scr)
    elif relayout == "interleave":
        out = _interleave_rows(out, slab_scr)
    blocked_out = relayout == "deinterleave"
    if norm_out:
        gnext_ref, o_ref, hn_ref = refs
        hn = _rms_norm(out, gnext_ref[...]).astype(jnp.bfloat16)
        if blocked_out:
            hn_ref[0] = hn.reshape(hn_ref.shape[1:])
        else:
            hn_ref[...] = hn
    else:
        (o_ref,) = refs
    if blocked_out:
        o_ref[0] = out.reshape(o_ref.shape[1:])
    else:
        o_ref[...] = out


def _ffn(x, layer, g_pre, w_gate, w_up, w_down, g_post, *, batch, tm, chunk, mix=None, g_next=None, relayout=None):
    m, d = x.shape
    dff = w_gate.shape[2]
    seq = m // batch
    steps_per_seq = seq // tm
    assert dff % _V7X_MXU_DIM == 0 and chunk % _V7X_MXU_DIM == 0 and seq % tm == 0 and tm % (_NP * _V7X_BF16_SUBLANES) == 0
    row = lambda a: a[layer].reshape(1, d)
    flat_tile = lambda width: pl.BlockSpec((tm, width), lambda i: (i, 0))
    blocked_tile = lambda width: pl.BlockSpec((1, _NP, tm // _NP, width),
                                              lambda i: (i // steps_per_seq, 0, i % steps_per_seq, 0))
    blocked = lambda a: a.reshape(batch, _NP, seq // _NP, a.shape[-1])
    if relayout == "interleave":
        in_tile, put = blocked_tile, blocked
    else:
        in_tile, put = flat_tile, lambda a: a
    vec = pl.BlockSpec((1, d), lambda i: (0, 0))
    resident = lambda r, c, blk=0: pl.BlockSpec((None, r, c), lambda i: (layer, blk, 0), pipeline_mode=pl.Buffered(1))
    args, specs = [put(x)], [in_tile(d)]
    if mix is not None:
        ya, yb, w_out, g_mix = mix
        da = ya.shape[1]
        assert yb.shape[1] == da and w_out.shape[1:] == (2 * da, d)
        args += [put(ya), put(yb), w_out, w_out, row(g_mix)]
        specs += [in_tile(da), in_tile(da), resident(da, d, 0), resident(da, d, 1), vec]
    args += [row(g_pre), w_gate, w_up, w_down, _HALF_STEP * row(g_post)]
    specs += [vec, resident(d, dff), resident(d, dff), resident(dff, d), vec]
    if relayout == "deinterleave":
        out_tile, out_dims = blocked_tile(d), (batch, _NP, seq // _NP, d)
    else:
        out_tile, out_dims = flat_tile(d), (m, d)
    out_shape = [jax.ShapeDtypeStruct(out_dims, jnp.float32)]
    out_specs = [out_tile]
    if g_next is not None:
        args.append(row(g_next))
        specs.append(vec)
        out_shape.append(jax.ShapeDtypeStruct(out_dims, jnp.bfloat16))
        out_specs.append(out_tile)
    outs = pl.pallas_call(
        functools.partial(_ffn_kernel, chunk=chunk, mix_in=mix is not None, norm_out=g_next is not None,
                          relayout=relayout),
        grid=(m // tm,),
        in_specs=specs,
        out_specs=out_specs,
        out_shape=out_shape,
        scratch_shapes=[pltpu.VMEM((d // _V7X_LANES, tm, _V7X_LANES), jnp.float32)] if relayout else [],
        compiler_params=_cparams(("arbitrary",)),
        name="ffn",
    )(*args)
    outs = [o.reshape(m, d) for o in outs]
    return outs if g_next is not None else outs[0]


def _gmlp_kernel(h_ref, w_ref, lng_ref, lnb_ref, ws_ref, bs_ref, o_ref, *, jb, da):
    rows = _CHUNK // _NP
    d = h_ref.shape[-1]
    h = h_ref[0].reshape(_NP * jb * rows, d)
    gelu = lambda p: 0.5 * p * (1.0 + lax.erf(p * _SQRT_HALF))
    u = gelu(jnp.dot(h, w_ref[:, :da], preferred_element_type=jnp.float32))
    v = gelu(jnp.dot(h, w_ref[:, da:], preferred_element_type=jnp.float32))
    mu = jnp.mean(v, axis=-1, keepdims=True)
    vc = v - mu
    var = jnp.mean(vc * vc, axis=-1, keepdims=True)
    vn = (vc * lax.rsqrt(var + _LN_EPS) * lng_ref[...] + lnb_ref[...]).astype(jnp.bfloat16)
    lane = lax.broadcasted_iota(jnp.int32, (_CHUNK, _V7X_LANES), 1)
    first_head = lane < _HEAD_DIM_A
    n_pairs = da // _V7X_LANES
    for j in range(jb):
        starts = [(n2 * jb + j) * rows for n2 in range(_NP)]
        vchunk = jnp.concatenate([vn[s:s + rows] for s in starts], axis=0)
        mixed = []
        for q in range(n_pairs):
            r = jnp.dot(ws_ref[q], vchunk[:, q * _V7X_LANES:(q + 1) * _V7X_LANES],
                        preferred_element_type=jnp.float32)
            mixed.append(jnp.where(first_head, r[:_CHUNK], r[_CHUNK:]))
        mixed = jnp.concatenate(mixed, axis=1) + bs_ref[...]
        for n2 in range(_NP):
            s = starts[n2]
            o_ref[0, n2, j] = (u[s:s + rows] * mixed[n2 * rows:(n2 + 1) * rows]).astype(o_ref.dtype)


def _gmlp(h, w_in, layer, ln_g, ln_b, w_s, b_s, *, batch, seq, jb):
    m, d = h.shape
    da = ln_g.shape[0]
    rows = _CHUNK // _NP
    n_chunks = seq // _CHUNK
    assert rows == _V7X_BF16_SUBLANES
    h5 = h.reshape(batch, _NP, n_chunks, rows, d)
    ws_p = w_s.reshape(_HEADS_A, rows, _NP, rows, _NP).transpose(0, 2, 1, 4, 3).reshape(_HEADS_A, _CHUNK, _CHUNK)
    ws_pairs = ws_p.reshape(_HEADS_A // 2, 2 * _CHUNK, _CHUNK).astype(jnp.bfloat16)
    bs_p = b_s.reshape(_HEADS_A, rows, _NP).transpose(0, 2, 1).reshape(_HEADS_A, _CHUNK)
    bs_full = jnp.repeat(bs_p.T, _HEAD_DIM_A, axis=1)
    out = pl.pallas_call(
        functools.partial(_gmlp_kernel, jb=jb, da=da),
        grid=(batch, n_chunks // jb),
        in_specs=[
            pl.BlockSpec((1, _NP, jb, rows, d), lambda b, j: (b, 0, j, 0, 0)),
            pl.BlockSpec((None, d, 2 * da), lambda b, j: (layer, 0, 0)),
            pl.BlockSpec((1, da), lambda b, j: (0, 0)),
            pl.BlockSpec((1, da), lambda b, j: (0, 0)),
            pl.BlockSpec((_HEADS_A // 2, 2 * _CHUNK, _CHUNK), lambda b, j: (0, 0, 0)),
            pl.BlockSpec((_CHUNK, da), lambda b, j: (0, 0)),
        ],
        out_specs=pl.BlockSpec((1, _NP, jb, rows, da), lambda b, j: (b, 0, j, 0, 0)),
        out_shape=jax.ShapeDtypeStruct((batch, _NP, n_chunks, rows, da), jnp.bfloat16),
        compiler_params=_cparams(("arbitrary", "arbitrary")),
        name="gmlp",
    )(h5, w_in, ln_g.reshape(1, da), ln_b.reshape(1, da), ws_pairs, bs_full)
    return out.reshape(m, da)


def _hyproj_kernel(h_ref, *refs, n1):
    n_proj = _ORDER + 1
    w_refs, (cw_ref, cb_ref, o_ref) = refs[:n_proj], refs[n_proj:]
    c = o_ref.shape[-1]
    h = h_ref[0]
    sub = cw_ref.shape[1]
    tiled = lambda a: a.reshape(n1 // sub, sub, c)
    for j in range(n_proj):
        p = jnp.dot(h, w_refs[j][...], preferred_element_type=jnp.float32)
        blocks = [p[k * n1:(k + 1) * n1] for k in range(_NP)]
        row = lax.broadcasted_iota(jnp.int32, blocks[0].shape, 0)
        before_first = jnp.where(row == 0, 0.0, pltpu.roll(blocks[_NP - 1], 1, axis=0))
        after_last = jnp.where(row == n1 - 1, 0.0, pltpu.roll(blocks[0], n1 - 1, axis=0))
        cols = slice(j * c, (j + 1) * c)
        w0, w1, w2 = cw_ref[0, :, cols], cw_ref[1, :, cols], cw_ref[2, :, cols]
        bias = cb_ref[:, cols]
        for k in range(_NP):
            prev = blocks[k - 1] if k > 0 else before_first
            nxt = blocks[k + 1] if k < _NP - 1 else after_last
            y = ((bias + tiled(prev) * w0) + tiled(blocks[k]) * w1) + tiled(nxt) * w2
            o_ref[j, k * n1:(k + 1) * n1] = y.reshape(n1, c).astype(o_ref.dtype)


def _hyproj(h, w_in, layer, conv_w, conv_b, *, batch, seq):
    m, d = h.shape
    n_proj = _ORDER + 1
    c = conv_w.shape[1] // n_proj
    first = (w_in.shape[2] - n_proj * c) // c
    w_spec = lambda j: pl.BlockSpec((None, d, c), lambda b: (layer, 0, first + j), pipeline_mode=pl.Buffered(1))
    sub = _V7X_F32_SUBLANES
    taps = jnp.broadcast_to(conv_w[:, None, :], (_SHORT_K, sub, n_proj * c))
    bias = jnp.broadcast_to(conv_b[None, :], (sub, n_proj * c))
    return pl.pallas_call(
        functools.partial(_hyproj_kernel, n1=seq // _NP),
        grid=(batch,),
        in_specs=[pl.BlockSpec((1, seq, d), lambda b: (b, 0, 0))] + [w_spec(j) for j in range(n_proj)] + [
            pl.BlockSpec((_SHORT_K, sub, n_proj * c), lambda b: (0, 0, 0)),
            pl.BlockSpec((sub, n_proj * c), lambda b: (0, 0)),
        ],
        out_specs=pl.BlockSpec((n_proj, seq, c), lambda b: (0, b, 0)),
        out_shape=jax.ShapeDtypeStruct((n_proj, m, c), jnp.bfloat16),
        compiler_params=_cparams(("arbitrary",)),
        name="hyproj",
    )(h.reshape(batch, seq, d), *([w_in] * n_proj), taps, bias)


@functools.lru_cache(maxsize=None)
def _dft_tables(seq):
    n1 = seq // _NP
    n_fft = 2 * seq
    k1 = np.arange(_HALF)[None, :, None]
    t = _NP * np.arange(n1)[None, None, :] + np.arange(_NP)[:, None, None]
    theta = ((k1 * t) % n_fft) * (2.0 * math.pi / n_fft)
    valid = k1 <= n1
    cos = np.where(valid, np.cos(theta), 0.0)
    sin = np.where(valid, np.sin(theta), 0.0)
    fwd = np.concatenate([cos, -sin], axis=1)
    weight = np.where((k1 == 0) | (k1 == n1), 1.0, 2.0) / n_fft
    inv = np.concatenate([weight * cos, -weight * sin], axis=1).transpose(0, 2, 1)
    return fwd.astype(np.float32), np.ascontiguousarray(inv).astype(np.float32)


def _fft4(cr, ci):
    d0r, d0i = cr[0] + cr[2], ci[0] + ci[2]
    d1r, d1i = cr[1] + cr[3], ci[1] + ci[3]
    d2r, d2i = cr[0] - cr[2], ci[0] - ci[2]
    er, ei = cr[1] - cr[3], ci[1] - ci[3]
    d3r, d3i = ei, -er
    return [(d0r + d1r, d0i + d1i), (d2r + d3r, d2i + d3i), (d0r - d1r, d0i - d1i), (d2r - d3r, d2i - d3i)]


def _fft8(re, im):
    br, bi = [None] * 8, [None] * 8
    for j in range(4):
        br[j], bi[j] = re[j] + re[j + 4], im[j] + im[j + 4]
        dr, di = re[j] - re[j + 4], im[j] - im[j + 4]
        if j == 0:
            br[4], bi[4] = dr, di
        elif j == 1:
            br[5], bi[5] = (dr + di) * _SQRT_HALF, (di - dr) * _SQRT_HALF
        elif j == 2:
            br[6], bi[6] = di, -dr
        else:
            br[7], bi[7] = (di - dr) * _SQRT_HALF, -(dr + di) * _SQRT_HALF
    even = _fft4(br[:4], bi[:4])
    odd = _fft4(br[4:], bi[4:])
    out = [None] * 8
    for k in range(4):
        out[2 * k], out[2 * k + 1] = even[k], odd[k]
    return [o[0] for o in out], [o[1] for o in out]


def _ifft8(re, im):
    o_im, o_re = _fft8(im, re)
    return o_re, o_im


def _filter_kernel(w1_ref, b1_ref, w2_ref, b2_ref, w3_ref, b3_ref, fr_ref, wof_ref, wob_ref, dl_ref,
                   fwd_ref, o_ref, h_scr, af_scr, ab_scr, *, seq):
    n1 = seq // _NP
    fw = _FILTER_WIDTH
    hp = lax.Precision.HIGHEST

    @pl.when(pl.program_id(1) == 0)
    def _():
        nb = _FILTER_BANDS
        r = lax.broadcasted_iota(jnp.int32, (nb, seq), 1)
        pos = (_NP * (r % n1) + r // n1).astype(jnp.float32)
        band_idx = lax.broadcasted_iota(jnp.int32, (nb, seq), 0).astype(jnp.float32)
        band = 1e-4 + band_idx * ((nb - 1 - 1e-4) / (nb - 1))
        ang = (2.0 * math.pi / seq) * pos * band
        first_row = lax.broadcasted_iota(jnp.int32, (fw - 2 * nb, seq), 0) == 0
        tail = jnp.where(first_row, jnp.concatenate([pos] * ((fw - 2 * nb) // nb), axis=0) * (1.0 / (seq - 1)), 0.0)
        z = jnp.concatenate([jnp.cos(ang), -jnp.sin(ang), tail], axis=0)
        freq = fr_ref[0]
        h = jnp.sin(freq * (jnp.dot(w1_ref[0], z, precision=hp, preferred_element_type=jnp.float32) + b1_ref[0]))
        h = jnp.sin(freq * (jnp.dot(w2_ref[0], h, precision=hp, preferred_element_type=jnp.float32) + b2_ref[0]))
        h = jnp.sin(freq * (jnp.dot(w3_ref[0], h, precision=hp, preferred_element_type=jnp.float32) + b3_ref[0]))
        h_scr[...] = h.T

    r = lax.broadcasted_iota(jnp.int32, (seq, 1), 0)
    pos = (_NP * (r % n1) + r // n1).astype(jnp.float32)
    h = h_scr[...].astype(jnp.bfloat16)
    decay = jnp.exp(-(pos * (1.0 / (seq - 1))) * dl_ref[...])
    hf = jnp.dot(h, wof_ref[0].astype(jnp.bfloat16), preferred_element_type=jnp.float32) * decay
    hb = jnp.dot(h, wob_ref[0].astype(jnp.bfloat16), preferred_element_type=jnp.float32) * decay
    hb = jnp.where(pos == 0.0, 0.0, hb)
    hf = hf.astype(jnp.bfloat16)
    hb = hb.astype(jnp.bfloat16)
    for n2 in range(_NP):
        af_scr[n2] = jnp.dot(fwd_ref[n2], hf[n2 * n1:(n2 + 1) * n1], preferred_element_type=jnp.float32)
        ab_scr[n2] = jnp.dot(fwd_ref[n2], hb[n2 * n1:(n2 + 1) * n1], preferred_element_type=jnp.float32)

    rc = _V7X_BF16_SUBLANES

    def body(i, carry):
        r0 = pl.multiple_of(i * rc, rc)
        fr, fi = _fft8([af_scr[n2, pl.ds(r0, rc)] for n2 in range(_NP)],
                       [af_scr[n2, pl.ds(_HALF + r0, rc)] for n2 in range(_NP)])
        gr, gi = _fft8([ab_scr[n2, pl.ds(r0, rc)] for n2 in range(_NP)],
                       [ab_scr[n2, pl.ds(_HALF + r0, rc)] for n2 in range(_NP)])
        for k2 in range(_NP):
            o_ref[0, 0, k2, 0, pl.ds(r0, rc)] = (fr[k2] + gr[k2]).astype(o_ref.dtype)
            o_ref[0, 0, k2, 1, pl.ds(r0, rc)] = (fi[k2] - gi[k2]).astype(o_ref.dtype)
        return carry

    lax.fori_loop(0, _HALF // rc, body, 0)


def _filter_spectra(w1, b1, w2, b2, w3, b3, freq, w_out, fwd_tab, *, seq, ct):
    n_layers = w1.shape[0]
    c = w_out.shape[2] // (2 * _ORDER)
    n_ct = c // ct
    fw = _FILTER_WIDTH
    w1r = jnp.concatenate([w1[:, 1:], w1[:, :1]], axis=1)
    w1t = jnp.pad(w1r, ((0, 0), (0, fw - w1.shape[1]), (0, 0))).transpose(0, 2, 1)
    col = lambda a: a.reshape(n_layers, fw, 1)
    deltas = jnp.abs(jnp.linspace(_MIN_DECAY, _MAX_DECAY, c, dtype=jnp.float32)).reshape(1, c)
    lmap3 = lambda l, s: (l, 0, 0)
    wo_spec = lambda direction: pl.BlockSpec(
        (1, fw, ct), lambda l, s: (l, 0, (2 * (s // n_ct) + direction) * n_ct + s % n_ct))
    return pl.pallas_call(
        functools.partial(_filter_kernel, seq=seq),
        grid=(n_layers, _ORDER * n_ct),
        in_specs=[
            pl.BlockSpec((1, fw, fw), lmap3), pl.BlockSpec((1, fw, 1), lmap3),
            pl.BlockSpec((1, fw, fw), lmap3), pl.BlockSpec((1, fw, 1), lmap3),
            pl.BlockSpec((1, fw, fw), lmap3), pl.BlockSpec((1, fw, 1), lmap3),
            pl.BlockSpec((1, fw, 1), lmap3),
            wo_spec(0), wo_spec(1),
            pl.BlockSpec((1, ct), lambda l, s: (0, s % n_ct)),
            pl.BlockSpec((_NP, 2 * _HALF, seq // _NP), lambda l, s: (0, 0, 0)),
        ],
        out_specs=pl.BlockSpec((1, 1, _NP, 2, _HALF, ct), lambda l, s: (l, s // n_ct, 0, 0, 0, s % n_ct)),
        out_shape=jax.ShapeDtypeStruct((n_layers, _ORDER, _NP, 2, _HALF, c), _STAGE_TWO_DTYPE),
        scratch_shapes=[pltpu.VMEM((seq, fw), jnp.float32),
                        pltpu.VMEM((_NP, 2 * _HALF, ct), jnp.float32),
                        pltpu.VMEM((_NP, 2 * _HALF, ct), jnp.float32)],
        compiler_params=_cparams(("arbitrary", "arbitrary")),
        name="filter_spectra",
    )(w1t, col(b1), w2.transpose(0, 2, 1), col(b2), w3.transpose(0, 2, 1), col(b3), col(freq),
      w_out, w_out, deltas, fwd_tab)


def _hyena_kernel(v_ref, x1_ref, x2_ref, kf_ref, skip_ref, fwd_ref, inv_ref, o_ref, a_scr, c_scr, z_scr, *, seq):
    n1 = seq // _NP
    rc = _V7X_BF16_SUBLANES
    ct = o_ref.shape[-1]

    def conv_order(order, z_in_ref, gate_ref, z_out_ref):
        for n2 in range(_NP):
            zb = z_in_ref[0, n2 * n1:(n2 + 1) * n1].astype(jnp.bfloat16)
            a_scr[n2] = jnp.dot(fwd_ref[n2], zb, preferred_element_type=jnp.float32).astype(a_scr.dtype)

        def body(i, carry):
            r0 = pl.multiple_of(i * rc, rc)
            for l0 in range(0, ct, _V7X_LANES):
                ls = slice(l0, l0 + _V7X_LANES)
                xr, xi = _fft8([a_scr[n2, pl.ds(r0, rc), ls] for n2 in range(_NP)],
                               [a_scr[n2, pl.ds(_HALF + r0, rc), ls] for n2 in range(_NP)])
                yr, yi = [], []
                for k2 in range(_NP):
                    kr = kf_ref[order, k2, 0, pl.ds(r0, rc), ls]
                    ki = kf_ref[order, k2, 1, pl.ds(r0, rc), ls]
                    yr.append(xr[k2] * kr - xi[k2] * ki)
                    yi.append(xr[k2] * ki + xi[k2] * kr)
                cr, ci = _ifft8(yr, yi)
                for t2 in range(_NP):
                    c_scr[t2, pl.ds(r0, rc), ls] = cr[t2].astype(jnp.bfloat16)
                    c_scr[t2, pl.ds(_HALF + r0, rc), ls] = ci[t2].astype(jnp.bfloat16)
            return carry

        lax.fori_loop(0, _HALF // rc, body, 0)

        skip = skip_ref[order:order + 1]
        for t2 in range(_NP):
            rows = slice(t2 * n1, (t2 + 1) * n1)
            y = jnp.dot(inv_ref[t2], c_scr[t2], preferred_element_type=jnp.float32)
            z = z_in_ref[0, rows].astype(jnp.float32)
            gate = gate_ref[0, rows].astype(jnp.float32)
            z_out_ref[0, rows] = (gate * (y + z * skip)).astype(z_out_ref.dtype)

    conv_order(0, v_ref, x1_ref, z_scr)
    conv_order(1, z_scr, x2_ref, o_ref)


def _hyena(proj, kf_all, layer, skip, fwd_tab, inv_tab, *, batch, seq, ct):
    n_proj, m, c = proj.shape
    n1 = seq // _NP
    proj4 = proj.reshape(n_proj, batch, seq, c)
    pspec = lambda j: pl.BlockSpec((None, 1, seq, ct), lambda t, b, j=j: (j, b, 0, t))
    resident = dict(pipeline_mode=pl.Buffered(1))
    out = pl.pallas_call(
        functools.partial(_hyena_kernel, seq=seq),
        grid=(c // ct, batch),
        in_specs=[
            pspec(0), pspec(1), pspec(2),
            pl.BlockSpec((None, _ORDER, _NP, 2, _HALF, ct), lambda t, b: (layer, 0, 0, 0, 0, t), **resident),
            pl.BlockSpec((_ORDER, ct), lambda t, b: (0, t)),
            pl.BlockSpec((_NP, 2 * _HALF, n1), lambda t, b: (0, 0, 0), **resident),
            pl.BlockSpec((_NP, n1, 2 * _HALF), lambda t, b: (0, 0, 0), **resident),
        ],
        out_specs=pl.BlockSpec((1, seq, ct), lambda t, b: (b, 0, t)),
        out_shape=jax.ShapeDtypeStruct((batch, seq, c), jnp.bfloat16),
        scratch_shapes=[pltpu.VMEM((_NP, 2 * _HALF, ct), _STAGE_TWO_DTYPE),
                        pltpu.VMEM((_NP, 2 * _HALF, ct), jnp.bfloat16),
                        pltpu.VMEM((1, seq, ct), jnp.float32)],
        compiler_params=_cparams(("arbitrary", "arbitrary")),
        name="hyena",
    )(proj4, proj4, proj4, kf_all, skip, fwd_tab, inv_tab)
    return out.reshape(m, c)


def kernel(x, ffn1_pre_g, ffn1_w_gate, ffn1_w_up, ffn1_w_down, ffn1_post_g, mix_pre_g, mix_w_in, gmlp_ln_g, gmlp_ln_b, gmlp_w_s, gmlp_b_s, hy_conv_w, hy_conv_b, hy_filt_w1, hy_filt_b1, hy_filt_w2, hy_filt_b2, hy_filt_w3, hy_filt_b3, hy_filt_freq, hy_filt_w_out, hy_skip, mix_w_out, mix_post_g, ffn2_pre_g, ffn2_w_gate, ffn2_w_up, ffn2_w_down, ffn2_post_g):
    batch, seq, d = x.shape
    depth = ffn1_pre_g.shape[0]
    da = gmlp_ln_g.shape[1]
    assert seq % (_NP * _CHUNK) == 0 and seq // _NP + 1 <= _HALF
    m = batch * seq
    ffn_tiles = dict(batch=batch, tm=512, chunk=_V7X_MXU_DIM)
    ct = _V7X_MXU_DIM
    xp = x.reshape(m, d)

    fwd_np, inv_np = _dft_tables(seq)
    fwd_bf16 = jnp.asarray(fwd_np).astype(jnp.bfloat16)
    inv_bf16 = jnp.asarray(inv_np).astype(jnp.bfloat16)
    kf_all = _filter_spectra(hy_filt_w1, hy_filt_b1, hy_filt_w2, hy_filt_b2, hy_filt_w3, hy_filt_b3,
                             hy_filt_freq, hy_filt_w_out, fwd_bf16, seq=seq, ct=ct)

    ffn1_w = (ffn1_w_gate, ffn1_w_up, ffn1_w_down)
    ffn2_w = (ffn2_w_gate, ffn2_w_up, ffn2_w_down)
    w_in = mix_w_in
    w_out = mix_w_out.astype(jnp.bfloat16)

    for l in range(depth):
        xp, h = _ffn(xp, l, ffn1_pre_g, *ffn1_w, ffn1_post_g, g_next=mix_pre_g,
                     relayout="deinterleave" if l == 0 else None, **ffn_tiles)
        ya = _gmlp(h, w_in, l, gmlp_ln_g[l], gmlp_ln_b[l], gmlp_w_s[l], gmlp_b_s[l], batch=batch, seq=seq, jb=8)
        proj = _hyproj(h, w_in, l, hy_conv_w[l], hy_conv_b[l], batch=batch, seq=seq)
        yb = _hyena(proj, kf_all, l, hy_skip[l], fwd_bf16, inv_bf16, batch=batch, seq=seq, ct=proj.shape[-1])
        xp = _ffn(xp, l, ffn2_pre_g, *ffn2_w, ffn2_post_g, mix=(ya, yb, w_out, mix_post_g),
                  relayout="interleave" if l == depth - 1 else None, **ffn_tiles)

    return xp.reshape(batch, seq, d)
```

```python
import functools
import math

import jax
import jax.numpy as jnp
import numpy as np
from jax import lax
from jax.experimental import pallas as pl
from jax.experimental.pallas import tpu as pltpu

_CHUNK = 128
_HEADS_A = 8
_HEAD_DIM_A = 64
_ORDER = 2
_SHORT_K = 3
_FILTER_BANDS = 16
_FILTER_WIDTH = 64
_DECAY_TARGET = 1e-2
_MAX_DECAY = math.log(_DECAY_TARGET) / 0.3
_MIN_DECAY = math.log(_DECAY_TARGET) / 1.5
_HALF_STEP = 0.5
_RMS_EPS = 1e-6
_LN_EPS = 1e-5

_V7X_LANES = 128
_V7X_F32_SUBLANES = 8
_V7X_BF16_SUBLANES = 16
_V7X_MXU_DIM = 256
_V7X_VMEM_LIMIT_BYTES = 60 * 1024 * 1024

_NP = 8
_HALF = 272
_SQRT_HALF = 0.7071067811865476
_STAGE_TWO_DTYPE = jnp.bfloat16


def _cparams(semantics):
    return pltpu.CompilerParams(dimension_semantics=semantics, vmem_limit_bytes=_V7X_VMEM_LIMIT_BYTES)


def _rms_norm(x, g):
    return x * lax.rsqrt(jnp.mean(x * x, axis=-1, keepdims=True) + _RMS_EPS) * g


def _deinterleave_rows(val, slab_scr):
    rows, d = val.shape
    n_slabs = d // _V7X_LANES
    for k in range(n_slabs):
        slab_scr[k] = val[:, k * _V7X_LANES:(k + 1) * _V7X_LANES]
    return jnp.concatenate(
        [jnp.concatenate([slab_scr[k, pl.ds(n2, rows // _NP, stride=_NP), :] for k in range(n_slabs)], axis=1)
         for n2 in range(_NP)], axis=0)


def _interleave_rows(val, slab_scr):
    rows, d = val.shape
    per = rows // _NP
    n_slabs = d // _V7X_LANES
    for n2 in range(_NP):
        for k in range(n_slabs):
            slab_scr[k, pl.ds(n2, per, stride=_NP), :] = val[n2 * per:(n2 + 1) * per, k * _V7X_LANES:(k + 1) * _V7X_LANES]
    return jnp.concatenate([slab_scr[k] for k in range(n_slabs)], axis=1)


def _ffn_kernel(*refs, chunk, mix_in, norm_out, relayout):
    refs = list(refs)
    slab_scr = refs.pop() if relayout else None
    x_ref = refs.pop(0)
    blocked_in = relayout == "interleave"
    load = (lambda r: r[0].reshape(-1, r.shape[-1])) if blocked_in else (lambda r: r[...])
    x = load(x_ref)
    if mix_in:
        ya_ref, yb_ref, wa_ref, wb_ref, gmix_ref = refs[:5]
        refs = refs[5:]
        y = jnp.dot(load(ya_ref), wa_ref[...], preferred_element_type=jnp.float32)
        y = y + jnp.dot(load(yb_ref), wb_ref[...], preferred_element_type=jnp.float32)
        x = x + _rms_norm(y, gmix_ref[...])
    gpre_ref, wg_ref, wu_ref, wd_ref, gpost_ref = refs[:5]
    refs = refs[5:]
    h = _rms_norm(x, gpre_ref[...]).astype(jnp.bfloat16)
    acc = jnp.zeros(x.shape, jnp.float32)
    dff = wg_ref.shape[1]
    for c0 in range(0, dff, chunk):
        c1 = min(c0 + chunk, dff)
        g = jnp.dot(h, wg_ref[:, c0:c1], preferred_element_type=jnp.float32)
        u = jnp.dot(h, wu_ref[:, c0:c1], preferred_element_type=jnp.float32)
        a = (g * jax.nn.sigmoid(g) * u).astype(jnp.bfloat16)
        acc = acc + jnp.dot(a, wd_ref[c0:c1, :], preferred_element_type=jnp.float32)
    out = x + _rms_norm(acc, gpost_ref[...])
    if relayout == "deinterleave":
        out = _deinterleave_rows(out, slab_scr)
    elif relayout == "interleave":
        out = _interleave_rows(out, slab_scr)
    blocked_out = relayout == "deinterleave"
    if norm_out:
        gnext_ref, o_ref, hn_ref = refs
        hn = _rms_norm(out, gnext_ref[...]).astype(jnp.bfloat16)
        if blocked_out:
            hn_ref[0] = hn.reshape(hn_ref.shape[1:])
        else:
            hn_ref[...] = hn
    else:
        (o_ref,) = refs
    if blocked_out:
        o_ref[0] = out.reshape(o_ref.shape[1:])
    else:
        o_ref[...] = out


def _ffn(x, layer, g_pre, w_gate, w_up, w_down, g_post, *, batch, tm, chunk, mix=None, g_next=None, relayout=None):
    m, d = x.shape
    dff = w_gate.shape[2]
    seq = m // batch
    steps_per_seq = seq // tm
    assert dff % _V7X_MXU_DIM == 0 and chunk % _V7X_MXU_DIM == 0 and seq % tm == 0 and tm % (_NP * _V7X_BF16_SUBLANES) == 0
    row = lambda a: a[layer].reshape(1, d)
    flat_tile = lambda width: pl.BlockSpec((tm, width), lambda i: (i, 0))
    blocked_tile = lambda width: pl.BlockSpec((1, _NP, tm // _NP, width),
                                              lambda i: (i // steps_per_seq, 0, i % steps_per_seq, 0))
    blocked = lambda a: a.reshape(batch, _NP, seq // _NP, a.shape[-1])
    if relayout == "interleave":
        in_tile, put = blocked_tile, blocked
    else:
        in_tile, put = flat_tile, lambda a: a
    vec = pl.BlockSpec((1, d), lambda i: (0, 0))
    resident = lambda r, c, blk=0: pl.BlockSpec((None, r, c), lambda i: (layer, blk, 0), pipeline_mode=pl.Buffered(1))
    args, specs = [put(x)], [in_tile(d)]
    if mix is not None:
        ya, yb, w_out, g_mix = mix
        da = ya.shape[1]
        assert yb.shape[1] == da and w_out.shape[1:] == (2 * da, d)
        args += [put(ya), put(yb), w_out, w_out, row(g_mix)]
        specs += [in_tile(da), in_tile(da), resident(da, d, 0), resident(da, d, 1), vec]
    args += [row(g_pre), w_gate, w_up, w_down, _HALF_STEP * row(g_post)]
    specs += [vec, resident(d, dff), resident(d, dff), resident(dff, d), vec]
    if relayout == "deinterleave":
        out_tile, out_dims = blocked_tile(d), (batch, _NP, seq // _NP, d)
    else:
        out_tile, out_dims = flat_tile(d), (m, d)
    out_shape = [jax.ShapeDtypeStruct(out_dims, jnp.float32)]
    out_specs = [out_tile]
    if g_next is not None:
        args.append(row(g_next))
        specs.append(vec)
        out_shape.append(jax.ShapeDtypeStruct(out_dims, jnp.bfloat16))
        out_specs.append(out_tile)
    outs = pl.pallas_call(
        functools.partial(_ffn_kernel, chunk=chunk, mix_in=mix is not None, norm_out=g_next is not None,
                          relayout=relayout),
        grid=(m // tm,),
        in_specs=specs,
        out_specs=out_specs,
        out_shape=out_shape,
        scratch_shapes=[pltpu.VMEM((d // _V7X_LANES, tm, _V7X_LANES), jnp.float32)] if relayout else [],
        compiler_params=_cparams(("arbitrary",)),
        name="ffn",
    )(*args)
    outs = [o.reshape(m, d) for o in outs]
    return outs if g_next is not None else outs[0]


def _gmlp_kernel(h_ref, w_ref, lng_ref, lnb_ref, ws_ref, bs_ref, o_ref, *, jb, da):
    rows = _CHUNK // _NP
    d = h_ref.shape[-1]
    h = h_ref[0].reshape(_NP * jb * rows, d)
    gelu = lambda p: 0.5 * p * (1.0 + lax.erf(p * _SQRT_HALF))
    u = gelu(jnp.dot(h, w_ref[:, :da], preferred_element_type=jnp.float32))
    v = gelu(jnp.dot(h, w_ref[:, da:], preferred_element_type=jnp.float32))
    mu = jnp.mean(v, axis=-1, keepdims=True)
    vc = v - mu
    var = jnp.mean(vc * vc, axis=-1, keepdims=True)
    vn = (vc * lax.rsqrt(var + _LN_EPS) * lng_ref[...] + lnb_ref[...]).astype(jnp.bfloat16)
    lane = lax.broadcasted_iota(jnp.int32, (_CHUNK, _V7X_LANES), 1)
    first_head = lane < _HEAD_DIM_A
    n_pairs = da // _V7X_LANES
    for j in range(jb):
        starts = [(n2 * jb + j) * rows for n2 in range(_NP)]
        vchunk = jnp.concatenate([vn[s:s + rows] for s in starts], axis=0)
        mixed = []
        for q in range(n_pairs):
            r = jnp.dot(ws_ref[q], vchunk[:, q * _V7X_LANES:(q + 1) * _V7X_LANES],
                        preferred_element_type=jnp.float32)
            mixed.append(jnp.where(first_head, r[:_CHUNK], r[_CHUNK:]))
        mixed = jnp.concatenate(mixed, axis=1) + bs_ref[...]
        for n2 in range(_NP):
            s = starts[n2]
            o_ref[0, n2, j] = (u[s:s + rows] * mixed[n2 * rows:(n2 + 1) * rows]).astype(o_ref.dtype)


def _gmlp(h, w_in, layer, ln_g, ln_b, w_s, b_s, *, batch, seq, jb):
    m, d = h.shape
    da = ln_g.shape[0]
    rows = _CHUNK // _NP
    n_chunks = seq // _CHUNK
    assert rows == _V7X_BF16_SUBLANES
    h5 = h.reshape(batch, _NP, n_chunks, rows, d)
    ws_p = w_s.reshape(_HEADS_A, rows, _NP, rows, _NP).transpose(0, 2, 1, 4, 3).reshape(_HEADS_A, _CHUNK, _CHUNK)
    ws_pairs = ws_p.reshape(_HEADS_A // 2, 2 * _CHUNK, _CHUNK).astype(jnp.bfloat16)
    bs_p = b_s.reshape(_HEADS_A, rows, _NP).transpose(0, 2, 1).reshape(_HEADS_A, _CHUNK)
    bs_full = jnp.repeat(bs_p.T, _HEAD_DIM_A, axis=1)
    out = pl.pallas_call(
        functools.partial(_gmlp_kernel, jb=jb, da=da),
        grid=(batch, n_chunks // jb),
        in_specs=[
            pl.BlockSpec((1, _NP, jb, rows, d), lambda b, j: (b, 0, j, 0, 0)),
            pl.BlockSpec((None, d, 2 * da), lambda b, j: (layer, 0, 0)),
            pl.BlockSpec((1, da), lambda b, j: (0, 0)),
            pl.BlockSpec((1, da), lambda b, j: (0, 0)),
            pl.BlockSpec((_HEADS_A // 2, 2 * _CHUNK, _CHUNK), lambda b, j: (0, 0, 0)),
            pl.BlockSpec((_CHUNK, da), lambda b, j: (0, 0)),
        ],
        out_specs=pl.BlockSpec((1, _NP, jb, rows, da), lambda b, j: (b, 0, j, 0, 0)),
        out_shape=jax.ShapeDtypeStruct((batch, _NP, n_chunks, rows, da), jnp.bfloat16),
        compiler_params=_cparams(("arbitrary", "arbitrary")),
        name="gmlp",
    )(h5, w_in, ln_g.reshape(1, da), ln_b.reshape(1, da), ws_pairs, bs_full)
    return out.reshape(m, da)


def _hyproj_kernel(h_ref, *refs, n1):
    n_proj = _ORDER + 1
    w_refs, (cw_ref, cb_ref, o_ref) = refs[:n_proj], refs[n_proj:]
    c = o_ref.shape[-1]
    h = h_ref[0]
    sub = cw_ref.shape[1]
    tiled = lambda a: a.reshape(n1 // sub, sub, c)
    for j in range(n_proj):
        p = jnp.dot(h, w_refs[j][...], preferred_element_type=jnp.float32)
        blocks = [p[k * n1:(k + 1) * n1] for k in range(_NP)]
        row = lax.broadcasted_iota(jnp.int32, blocks[0].shape, 0)
        before_first = jnp.where(row == 0, 0.0, pltpu.roll(blocks[_NP - 1], 1, axis=0))
        after_last = jnp.where(row == n1 - 1, 0.0, pltpu.roll(blocks[0], n1 - 1, axis=0))
        cols = slice(j * c, (j + 1) * c)
        w0, w1, w2 = cw_ref[0, :, cols], cw_ref[1, :, cols], cw_ref[2, :, cols]
        bias = cb_ref[:, cols]
        for k in range(_NP):
            prev = blocks[k - 1] if k > 0 else before_first
            nxt = blocks[k + 1] if k < _NP - 1 else after_last
            y = ((bias + tiled(prev) * w0) + tiled(blocks[k]) * w1) + tiled(nxt) * w2
            o_ref[j, k * n1:(k + 1) * n1] = y.reshape(n1, c).astype(o_ref.dtype)


def _hyproj(h, w_in, layer, conv_w, conv_b, *, batch, seq):
    m, d = h.shape
    n_proj = _ORDER + 1
    c = conv_w.shape[1] // n_proj
    first = (w_in.shape[2] - n_proj * c) // c
    w_spec = lambda j: pl.BlockSpec((None, d, c), lambda b: (layer, 0, first + j), pipeline_mode=pl.Buffered(1))
    sub = _V7X_F32_SUBLANES
    taps = jnp.broadcast_to(conv_w[:, None, :], (_SHORT_K, sub, n_proj * c))
    bias = jnp.broadcast_to(conv_b[None, :], (sub, n_proj * c))
    return pl.pallas_call(
        functools.partial(_hyproj_kernel, n1=seq // _NP),
        grid=(batch,),
        in_specs=[pl.BlockSpec((1, seq, d), lambda b: (b, 0, 0))] + [w_spec(j) for j in range(n_proj)] + [
            pl.BlockSpec((_SHORT_K, sub, n_proj * c), lambda b: (0, 0, 0)),
            pl.BlockSpec((sub, n_proj * c), lambda b: (0, 0)),
        ],
        out_specs=pl.BlockSpec((n_proj, seq, c), lambda b: (0, b, 0)),
        out_shape=jax.ShapeDtypeStruct((n_proj, m, c), jnp.bfloat16),
        compiler_params=_cparams(("arbitrary",)),
        name="hyproj",
    )(h.reshape(batch, seq, d), *([w_in] * n_proj), taps, bias)


@functools.lru_cache(maxsize=None)
def _dft_tables(seq):
    n1 = seq // _NP
    n_fft = 2 * seq
    k1 = np.arange(_HALF)[None, :, None]
    t = _NP * np.arange(n1)[None, None, :] + np.arange(_NP)[:, None, None]
    theta = ((k1 * t) % n_fft) * (2.0 * math.pi / n_fft)
    valid = k1 <= n1
    cos = np.where(valid, np.cos(theta), 0.0)
    sin = np.where(valid, np.sin(theta), 0.0)
    fwd = np.concatenate([cos, -sin], axis=1)
    weight = np.where((k1 == 0) | (k1 == n1), 1.0, 2.0) / n_fft
    inv = np.concatenate([weight * cos, -weight * sin], axis=1).transpose(0, 2, 1)
    return fwd.astype(np.float32), np.ascontiguousarray(inv).astype(np.float32)


def _fft4(cr, ci):
    d0r, d0i = cr[0] + cr[2], ci[0] + ci[2]
    d1r, d1i = cr[1] + cr[3], ci[1] + ci[3]
    d2r, d2i = cr[0] - cr[2], ci[0] - ci[2]
    er, ei = cr[1] - cr[3], ci[1] - ci[3]
    d3r, d3i = ei, -er
    return [(d0r + d1r, d0i + d1i), (d2r + d3r, d2i + d3i), (d0r - d1r, d0i - d1i), (d2r - d3r, d2i - d3i)]


def _fft8(re, im):
    br, bi = [None] * 8, [None] * 8
    for j in range(4):
        br[j], bi[j] = re[j] + re[j + 4], im[j] + im[j + 4]
        dr, di = re[j] - re[j + 4], im[j] - im[j + 4]
        if j == 0:
            br[4], bi[4] = dr, di
        elif j == 1:
            br[5], bi[5] = (dr + di) * _SQRT_HALF, (di - dr) * _SQRT_HALF
        elif j == 2:
            br[6], bi[6] = di, -dr
        else:
            br[7], bi[7] = (di - dr) * _SQRT_HALF, -(dr + di) * _SQRT_HALF
    even = _fft4(br[:4], bi[:4])
    odd = _fft4(br[4:], bi[4:])
    out = [None] * 8
    for k in range(4):
        out[2 * k], out[2 * k + 1] = even[k], odd[k]
    return [o[0] for o in out], [o[1] for o in out]


def _ifft8(re, im):
    o_im, o_re = _fft8(im, re)
    return o_re, o_im


def _filter_kernel(w1_ref, b1_ref, w2_ref, b2_ref, w3_ref, b3_ref, fr_ref, wof_ref, wob_ref, dl_ref,
                   fwd_ref, o_ref, h_scr, af_scr, ab_scr, *, seq):
    n1 = seq // _NP
    fw = _FILTER_WIDTH
    hp = lax.Precision.HIGHEST

    @pl.when(pl.program_id(1) == 0)
    def _():
        nb = _FILTER_BANDS
        r = lax.broadcasted_iota(jnp.int32, (nb, seq), 1)
        pos = (_NP * (r % n1) + r // n1).astype(jnp.float32)
        band_idx = lax.broadcasted_iota(jnp.int32, (nb, seq), 0).astype(jnp.float32)
        band = 1e-4 + band_idx * ((nb - 1 - 1e-4) / (nb - 1))
        ang = (2.0 * math.pi / seq) * pos * band
        first_row = lax.broadcasted_iota(jnp.int32, (fw - 2 * nb, seq), 0) == 0
        tail = jnp.where(first_row, jnp.concatenate([pos] * ((fw - 2 * nb) // nb), axis=0) * (1.0 / (seq - 1)), 0.0)
        z = jnp.concatenate([jnp.cos(ang), -jnp.sin(ang), tail], axis=0)
        freq = fr_ref[0]
        h = jnp.sin(freq * (jnp.dot(w1_ref[0], z, precision=hp, preferred_element_type=jnp.float32) + b1_ref[0]))
        h = jnp.sin(freq * (jnp.dot(w2_ref[0], h, precision=hp, preferred_element_type=jnp.float32) + b2_ref[0]))
        h = jnp.sin(freq * (jnp.dot(w3_ref[0], h, precision=hp, preferred_element_type=jnp.float32) + b3_ref[0]))
        h_scr[...] = h.T

    r = lax.broadcasted_iota(jnp.int32, (seq, 1), 0)
    pos = (_NP * (r % n1) + r // n1).astype(jnp.float32)
    h = h_scr[...].astype(jnp.bfloat16)
    decay = jnp.exp(-(pos * (1.0 / (seq - 1))) * dl_ref[...])
    hf = jnp.dot(h, wof_ref[0].astype(jnp.bfloat16), preferred_element_type=jnp.float32) * decay
    hb = jnp.dot(h, wob_ref[0].astype(jnp.bfloat16), preferred_element_type=jnp.float32) * decay
    hb = jnp.where(pos == 0.0, 0.0, hb)
    hf = hf.astype(jnp.bfloat16)
    hb = hb.astype(jnp.bfloat16)
    for n2 in range(_NP):
        af_scr[n2] = jnp.dot(fwd_ref[n2], hf[n2 * n1:(n2 + 1) * n1], preferred_element_type=jnp.float32)
        ab_scr[n2] = jnp.dot(fwd_ref[n2], hb[n2 * n1:(n2 + 1) * n1], preferred_element_type=jnp.float32)

    rc = _V7X_BF16_SUBLANES

    def body(i, carry):
        r0 = pl.multiple_of(i * rc, rc)
        fr, fi = _fft8([af_scr[n2, pl.ds(r0, rc)] for n2 in range(_NP)],
                       [af_scr[n2, pl.ds(_HALF + r0, rc)] for n2 in range(_NP)])
        gr, gi = _fft8([ab_scr[n2, pl.ds(r0, rc)] for n2 in range(_NP)],
                       [ab_scr[n2, pl.ds(_HALF + r0, rc)] for n2 in range(_NP)])
        for k2 in range(_NP):
            o_ref[0, 0, k2, 0, pl.ds(r0, rc)] = (fr[k2] + gr[k2]).astype(o_ref.dtype)
            o_ref[0, 0, k2, 1, pl.ds(r0, rc)] = (fi[k2] - gi[k2]).astype(o_ref.dtype)
        return carry

    lax.fori_loop(0, _HALF // rc, body, 0)


def _filter_spectra(w1, b1, w2, b2, w3, b3, freq, w_out, fwd_tab, *, seq, ct):
    n_layers = w1.shape[0]
    c = w_out.shape[2] // (2 * _ORDER)
    n_ct = c // ct
    fw = _FILTER_WIDTH
    w1r = jnp.concatenate([w1[:, 1:], w1[:, :1]], axis=1)
    w1t = jnp.pad(w1r, ((0, 0), (0, fw - w1.shape[1]), (0, 0))).transpose(0, 2, 1)
    col = lambda a: a.reshape(n_layers, fw, 1)
    deltas = jnp.abs(jnp.linspace(_MIN_DECAY, _MAX_DECAY, c, dtype=jnp.float32)).reshape(1, c)
    lmap3 = lambda l, s: (l, 0, 0)
    wo_spec = lambda direction: pl.BlockSpec(
        (1, fw, ct), lambda l, s: (l, 0, (2 * (s // n_ct) + direction) * n_ct + s % n_ct))
    return pl.pallas_call(
        functools.partial(_filter_kernel, seq=seq),
        grid=(n_layers, _ORDER * n_ct),
        in_specs=[
            pl.BlockSpec((1, fw, fw), lmap3), pl.BlockSpec((1, fw, 1), lmap3),
            pl.BlockSpec((1, fw, fw), lmap3), pl.BlockSpec((1, fw, 1), lmap3),
            pl.BlockSpec((1, fw, fw), lmap3), pl.BlockSpec((1, fw, 1), lmap3),
            pl.BlockSpec((1, fw, 1), lmap3),
            wo_spec(0), wo_spec(1),
            pl.BlockSpec((1, ct), lambda l, s: (0, s % n_ct)),
            pl.BlockSpec((_NP, 2 * _HALF, seq // _NP), lambda l, s: (0, 0, 0)),
        ],
        out_specs=pl.BlockSpec((1, 1, _NP, 2, _HALF, ct), lambda l, s: (l, s // n_ct, 0, 0, 0, s % n_ct)),
        out_shape=jax.ShapeDtypeStruct((n_layers, _ORDER, _NP, 2, _HALF, c), _STAGE_TWO_DTYPE),
        scratch_shapes=[pltpu.VMEM((seq, fw), jnp.float32),
                        pltpu.VMEM((_NP, 2 * _HALF, ct), jnp.float32),
                        pltpu.VMEM((_NP, 2 * _HALF, ct), jnp.float32)],
        compiler_params=_cparams(("arbitrary", "arbitrary")),
        name="filter_spectra",
    )(w1t, col(b1), w2.transpose(0, 2, 1), col(b2), w3.transpose(0, 2, 1), col(b3), col(freq),
      w_out, w_out, deltas, fwd_tab)


def _hyena_kernel(v_ref, x1_ref, x2_ref, kf_ref, skip_ref, fwd_ref, inv_ref, o_ref, a_scr, c_scr, z_scr, *, seq):
    n1 = seq // _NP
    rc = _V7X_BF16_SUBLANES
    ct = o_ref.shape[-1]

    def conv_order(order, z_in_ref, gate_ref, z_out_ref):
        for n2 in range(_NP):
            zb = z_in_ref[0, n2 * n1:(n2 + 1) * n1].astype(jnp.bfloat16)
            a_scr[n2] = jnp.dot(fwd_ref[n2], zb, preferred_element_type=jnp.float32).astype(a_scr.dtype)

        def body(i, carry):
            r0 = pl.multiple_of(i * rc, rc)
            for l0 in range(0, ct, _V7X_LANES):
                ls = slice(l0, l0 + _V7X_LANES)
                xr, xi = _fft8([a_scr[n2, pl.ds(r0, rc), ls] for n2 in range(_NP)],
                               [a_scr[n2, pl.ds(_HALF + r0, rc), ls] for n2 in range(_NP)])
                yr, yi = [], []
                for k2 in range(_NP):
                    kr = kf_ref[order, k2, 0, pl.ds(r0, rc), ls]
                    ki = kf_ref[order, k2, 1, pl.ds(r0, rc), ls]
                    yr.append(xr[k2] * kr - xi[k2] * ki)
                    yi.append(xr[k2] * ki + xi[k2] * kr)
                cr, ci = _ifft8(yr, yi)
                for t2 in range(_NP):
                    c_scr[t2, pl.ds(r0, rc), ls] = cr[t2].astype(jnp.bfloat16)
                    c_scr[t2, pl.ds(_HALF + r0, rc), ls] = ci[t2].astype(jnp.bfloat16)
            return carry

        lax.fori_loop(0, _HALF // rc, body, 0)

        skip = skip_ref[order:order + 1]
        for t2 in range(_NP):
            rows = slice(t2 * n1, (t2 + 1) * n1)
            y = jnp.dot(inv_ref[t2], c_scr[t2], preferred_element_type=jnp.float32)
            z = z_in_ref[0, rows].astype(jnp.float32)
            gate = gate_ref[0, rows].astype(jnp.float32)
            z_out_ref[0, rows] = (gate * (y + z * skip)).astype(z_out_ref.dtype)

    conv_order(0, v_ref, x1_ref, z_scr)
    conv_order(1, z_scr, x2_ref, o_ref)


def _hyena(proj, kf_all, layer, skip, fwd_tab, inv_tab, *, batch, seq, ct):
    n_proj, m, c = proj.shape
    n1 = seq // _NP
    proj4 = proj.reshape(n_proj, batch, seq, c)
    pspec = lambda j: pl.BlockSpec((None, 1, seq, ct), lambda t, b, j=j: (j, b, 0, t))
    resident = dict(pipeline_mode=pl.Buffered(1))
    out = pl.pallas_call(
        functools.partial(_hyena_kernel, seq=seq),
        grid=(c // ct, batch),
        in_specs=[
            pspec(0), pspec(1), pspec(2),
            pl.BlockSpec((None, _ORDER, _NP, 2, _HALF, ct), lambda t, b: (layer, 0, 0, 0, 0, t), **resident),
            pl.BlockSpec((_ORDER, ct), lambda t, b: (0, t)),
            pl.BlockSpec((_NP, 2 * _HALF, n1), lambda t, b: (0, 0, 0), **resident),
            pl.BlockSpec((_NP, n1, 2 * _HALF), lambda t, b: (0, 0, 0), **resident),
        ],
        out_specs=pl.BlockSpec((1, seq, ct), lambda t, b: (b, 0, t)),
        out_shape=jax.ShapeDtypeStruct((batch, seq, c), jnp.bfloat16),
        scratch_shapes=[pltpu.VMEM((_NP, 2 * _HALF, ct), _STAGE_TWO_DTYPE),
                        pltpu.VMEM((_NP, 2 * _HALF, ct), jnp.bfloat16),
                        pltpu.VMEM((1, seq, ct), jnp.float32)],
        compiler_params=_cparams(("arbitrary", "arbitrary")),
        name="hyena",
    )(proj4, proj4, proj4, kf_all, skip, fwd_tab, inv_tab)
    return out.reshape(m, c)


def kernel(x, ffn1_pre_g, ffn1_w_gate, ffn1_w_up, ffn1_w_down, ffn1_post_g, mix_pre_g, mix_w_in, gmlp_ln_g, gmlp_ln_b, gmlp_w_s, gmlp_b_s, hy_conv_w, hy_conv_b, hy_filt_w1, hy_filt_b1, hy_filt_w2, hy_filt_b2, hy_filt_w3, hy_filt_b3, hy_filt_freq, hy_filt_w_out, hy_skip, mix_w_out, mix_post_g, ffn2_pre_g, ffn2_w_gate, ffn2_w_up, ffn2_w_down, ffn2_post_g):
    batch, seq, d = x.shape
    depth = ffn1_pre_g.shape[0]
    da = gmlp_ln_g.shape[1]
    assert seq % (_NP * _CHUNK) == 0 and seq // _NP + 1 <= _HALF
    m = batch * seq
    ffn_tiles = dict(batch=batch, tm=512, chunk=_V7X_MXU_DIM)
    ct = _V7X_MXU_DIM
    xp = x.reshape(batch, seq // _NP, _NP, d).transpose(0, 2, 1, 3).reshape(m, d)

    fwd_np, inv_np = _dft_tables(seq)
    fwd_bf16 = jnp.asarray(fwd_np).astype(jnp.bfloat16)
    inv_bf16 = jnp.asarray(inv_np).astype(jnp.bfloat16)
    kf_all = _filter_spectra(hy_filt_w1, hy_filt_b1, hy_filt_w2, hy_filt_b2, hy_filt_w3, hy_filt_b3,
                             hy_filt_freq, hy_filt_w_out, fwd_bf16, seq=seq, ct=ct)

    ffn1_w = (ffn1_w_gate, ffn1_w_up, ffn1_w_down)
    ffn2_w = (ffn2_w_gate, ffn2_w_up, ffn2_w_down)
    w_in = mix_w_in
    w_out = mix_w_out.astype(jnp.bfloat16)

    for l in range(depth):
        xp, h = _ffn(xp, l, ffn1_pre_g, *ffn1_w, ffn1_post_g, g_next=mix_pre_g, **ffn_tiles)
        ya = _gmlp(h, w_in, l, gmlp_ln_g[l], gmlp_ln_b[l], gmlp_w_s[l], gmlp_b_s[l], batch=batch, seq=seq, jb=8)
        proj = _hyproj(h, w_in, l, hy_conv_w[l], hy_conv_b[l], batch=batch, seq=seq)
        yb = _hyena(proj, kf_all, l, hy_skip[l], fwd_bf16, inv_bf16, batch=batch, seq=seq, ct=proj.shape[-1])
        xp = _ffn(xp, l, ffn2_pre_g, *ffn2_w, ffn2_post_g, mix=(ya, yb, w_out, mix_post_g),
                  relayout="interleave" if l == depth - 1 else None, **ffn_tiles)

    return xp.reshape(batch, seq, d)
```

```python
import functools
import math

import jax
import jax.numpy as jnp
import numpy as np
from jax import lax
from jax.experimental import pallas as pl
from jax.experimental.pallas import tpu as pltpu

_CHUNK = 128
_HEADS_A = 8
_HEAD_DIM_A = 64
_ORDER = 2
_SHORT_K = 3
_FILTER_BANDS = 16
_FILTER_WIDTH = 64
_DECAY_TARGET = 1e-2
_MAX_DECAY = math.log(_DECAY_TARGET) / 0.3
_MIN_DECAY = math.log(_DECAY_TARGET) / 1.5
_HALF_STEP = 0.5
_RMS_EPS = 1e-6
_LN_EPS = 1e-5

_V7X_LANES = 128
_V7X_F32_SUBLANES = 8
_V7X_BF16_SUBLANES = 16
_V7X_MXU_DIM = 256
_V7X_VMEM_LIMIT_BYTES = 60 * 1024 * 1024

_NP = 8
_HALF = 272
_SQRT_HALF = 0.7071067811865476
_STAGE_TWO_DTYPE = jnp.bfloat16


def _cparams(semantics):
    return pltpu.CompilerParams(dimension_semantics=semantics, vmem_limit_bytes=_V7X_VMEM_LIMIT_BYTES)


def _rms_norm(x, g):
    return x * lax.rsqrt(jnp.mean(x * x, axis=-1, keepdims=True) + _RMS_EPS) * g


def _deinterleave_rows(val, slab_scr):
    rows, d = val.shape
    n_slabs = d // _V7X_LANES
    for k in range(n_slabs):
        slab_scr[k] = val[:, k * _V7X_LANES:(k + 1) * _V7X_LANES]
    return jnp.concatenate(
        [jnp.concatenate([slab_scr[k, pl.ds(n2, rows // _NP, stride=_NP), :] for k in range(n_slabs)], axis=1)
         for n2 in range(_NP)], axis=0)


def _interleave_rows(val, slab_scr):
    rows, d = val.shape
    per = rows // _NP
    n_slabs = d // _V7X_LANES
    for n2 in range(_NP):
        for k in range(n_slabs):
            slab_scr[k, pl.ds(n2, per, stride=_NP), :] = val[n2 * per:(n2 + 1) * per, k * _V7X_LANES:(k + 1) * _V7X_LANES]
    return jnp.concatenate([slab_scr[k] for k in range(n_slabs)], axis=1)


def _ffn_kernel(*refs, chunk, mix_in, norm_out, relayout):
    refs = list(refs)
    slab_scr = refs.pop() if relayout else None
    x_ref = refs.pop(0)
    blocked_in = relayout == "interleave"
    load = (lambda r: r[0].reshape(-1, r.shape[-1])) if blocked_in else (lambda r: r[...])
    x = load(x_ref)
    if mix_in:
        ya_ref, yb_ref, wa_ref, wb_ref, gmix_ref = refs[:5]
        refs = refs[5:]
        y = jnp.dot(load(ya_ref), wa_ref[...], preferred_element_type=jnp.float32)
        y = y + jnp.dot(load(yb_ref), wb_ref[...], preferred_element_type=jnp.float32)
        x = x + _rms_norm(y, gmix_ref[...])
    gpre_ref, wg_ref, wu_ref, wd_ref, gpost_ref = refs[:5]
    refs = refs[5:]
    h = _rms_norm(x, gpre_ref[...])
    acc = jnp.zeros(x.shape, jnp.float32)
    dff = wg_ref.shape[1]
    for c0 in range(0, dff, chunk):
        c1 = min(c0 + chunk, dff)
        g = jnp.dot(h, wg_ref[:, c0:c1], preferred_element_type=jnp.float32)
        u = jnp.dot(h, wu_ref[:, c0:c1], preferred_element_type=jnp.float32)
        a = g * jax.nn.sigmoid(g) * u
        acc = acc + jnp.dot(a, wd_ref[c0:c1, :], preferred_element_type=jnp.float32)
    out = x + _rms_norm(acc, gpost_ref[...])
    if relayout == "deinterleave":
        out = _deinterleave_rows(out, slab_scr)
    elif relayout == "interleave":
        out = _interleave_rows(out, slab_scr)
    blocked_out = relayout == "deinterleave"
    if norm_out:
        gnext_ref, o_ref, hn_ref = refs
        hn = _rms_norm(out, gnext_ref[...]).astype(jnp.bfloat16)
        if blocked_out:
            hn_ref[0] = hn.reshape(hn_ref.shape[1:])
        else:
            hn_ref[...] = hn
    else:
        (o_ref,) = refs
    if blocked_out:
        o_ref[0] = out.reshape(o_ref.shape[1:])
    else:
        o_ref[...] = out


def _ffn(x, layer, g_pre, w_gate, w_up, w_down, g_post, *, batch, tm, chunk, mix=None, g_next=None, relayout=None):
    m, d = x.shape
    dff = w_gate.shape[2]
    seq = m // batch
    steps_per_seq = seq // tm
    assert dff % _V7X_MXU_DIM == 0 and chunk % _V7X_MXU_DIM == 0 and seq % tm == 0 and tm % (_NP * _V7X_BF16_SUBLANES) == 0
    row = lambda a: a[layer].reshape(1, d)
    flat_tile = lambda width: pl.BlockSpec((tm, width), lambda i: (i, 0))
    blocked_tile = lambda width: pl.BlockSpec((1, _NP, tm // _NP, width),
                                              lambda i: (i // steps_per_seq, 0, i % steps_per_seq, 0))
    blocked = lambda a: a.reshape(batch, _NP, seq // _NP, a.shape[-1])
    if relayout == "interleave":
        in_tile, put = blocked_tile, blocked
    else:
        in_tile, put = flat_tile, lambda a: a
    vec = pl.BlockSpec((1, d), lambda i: (0, 0))
    resident = lambda r, c, blk=0: pl.BlockSpec((None, r, c), lambda i: (layer, blk, 0), pipeline_mode=pl.Buffered(1))
    args, specs = [put(x)], [in_tile(d)]
    if mix is not None:
        ya, yb, w_out, g_mix = mix
        da = ya.shape[1]
        assert yb.shape[1] == da and w_out.shape[1:] == (2 * da, d)
        args += [put(ya), put(yb), w_out, w_out, row(g_mix)]
        specs += [in_tile(da), in_tile(da), resident(da, d, 0), resident(da, d, 1), vec]
    args += [row(g_pre), w_gate, w_up, w_down, _HALF_STEP * row(g_post)]
    specs += [vec, resident(d, dff), resident(d, dff), resident(dff, d), vec]
    if relayout == "deinterleave":
        out_tile, out_dims = blocked_tile(d), (batch, _NP, seq // _NP, d)
    else:
        out_tile, out_dims = flat_tile(d), (m, d)
    out_shape = [jax.ShapeDtypeStruct(out_dims, jnp.float32)]
    out_specs = [out_tile]
    if g_next is not None:
        args.append(row(g_next))
        specs.append(vec)
        out_shape.append(jax.ShapeDtypeStruct(out_dims, jnp.bfloat16))
        out_specs.append(out_tile)
    outs = pl.pallas_call(
        functools.partial(_ffn_kernel, chunk=chunk, mix_in=mix is not None, norm_out=g_next is not None,
                          relayout=relayout),
        grid=(m // tm,),
        in_specs=specs,
        out_specs=out_specs,
        out_shape=out_shape,
        scratch_shapes=[pltpu.VMEM((d // _V7X_LANES, tm, _V7X_LANES), jnp.float32)] if relayout else [],
        compiler_params=_cparams(("arbitrary",)),
        name="ffn",
    )(*args)
    outs = [o.reshape(m, d) for o in outs]
    return outs if g_next is not None else outs[0]


def _gmlp_kernel(h_ref, w_ref, lng_ref, lnb_ref, ws_ref, bs_ref, o_ref, *, jb, da):
    rows = _CHUNK // _NP
    d = h_ref.shape[-1]
    h = h_ref[0].reshape(_NP * jb * rows, d)
    gelu = lambda p: 0.5 * p * (1.0 + lax.erf(p * _SQRT_HALF))
    u = gelu(jnp.dot(h, w_ref[:, :da], preferred_element_type=jnp.float32))
    v = gelu(jnp.dot(h, w_ref[:, da:], preferred_element_type=jnp.float32))
    mu = jnp.mean(v, axis=-1, keepdims=True)
    vc = v - mu
    var = jnp.mean(vc * vc, axis=-1, keepdims=True)
    vn = (vc * lax.rsqrt(var + _LN_EPS) * lng_ref[...] + lnb_ref[...]).astype(jnp.bfloat16)
    lane = lax.broadcasted_iota(jnp.int32, (_CHUNK, _V7X_LANES), 1)
    first_head = lane < _HEAD_DIM_A
    n_pairs = da // _V7X_LANES
    for j in range(jb):
        starts = [(n2 * jb + j) * rows for n2 in range(_NP)]
        vchunk = jnp.concatenate([vn[s:s + rows] for s in starts], axis=0)
        mixed = []
        for q in range(n_pairs):
            r = jnp.dot(ws_ref[q], vchunk[:, q * _V7X_LANES:(q + 1) * _V7X_LANES],
                        preferred_element_type=jnp.float32)
            mixed.append(jnp.where(first_head, r[:_CHUNK], r[_CHUNK:]))
        mixed = jnp.concatenate(mixed, axis=1) + bs_ref[...]
        for n2 in range(_NP):
            s = starts[n2]
            o_ref[0, n2, j] = (u[s:s + rows] * mixed[n2 * rows:(n2 + 1) * rows]).astype(o_ref.dtype)


def _gmlp(h, w_in, layer, ln_g, ln_b, w_s, b_s, *, batch, seq, jb):
    m, d = h.shape
    da = ln_g.shape[0]
    rows = _CHUNK // _NP
    n_chunks = seq // _CHUNK
    assert rows == _V7X_BF16_SUBLANES
    h5 = h.reshape(batch, _NP, n_chunks, rows, d)
    ws_p = w_s.reshape(_HEADS_A, rows, _NP, rows, _NP).transpose(0, 2, 1, 4, 3).reshape(_HEADS_A, _CHUNK, _CHUNK)
    ws_pairs = ws_p.reshape(_HEADS_A // 2, 2 * _CHUNK, _CHUNK).astype(jnp.bfloat16)
    bs_p = b_s.reshape(_HEADS_A, rows, _NP).transpose(0, 2, 1).reshape(_HEADS_A, _CHUNK)
    bs_full = jnp.repeat(bs_p.T, _HEAD_DIM_A, axis=1)
    out = pl.pallas_call(
        functools.partial(_gmlp_kernel, jb=jb, da=da),
        grid=(batch, n_chunks // jb),
        in_specs=[
            pl.BlockSpec((1, _NP, jb, rows, d), lambda b, j: (b, 0, j, 0, 0)),
            pl.BlockSpec((None, d, 2 * da), lambda b, j: (layer, 0, 0)),
            pl.BlockSpec((1, da), lambda b, j: (0, 0)),
            pl.BlockSpec((1, da), lambda b, j: (0, 0)),
            pl.BlockSpec((_HEADS_A // 2, 2 * _CHUNK, _CHUNK), lambda b, j: (0, 0, 0)),
            pl.BlockSpec((_CHUNK, da), lambda b, j: (0, 0)),
        ],
        out_specs=pl.BlockSpec((1, _NP, jb, rows, da), lambda b, j: (b, 0, j, 0, 0)),
        out_shape=jax.ShapeDtypeStruct((batch, _NP, n_chunks, rows, da), jnp.bfloat16),
        compiler_params=_cparams(("arbitrary", "arbitrary")),
        name="gmlp",
    )(h5, w_in, ln_g.reshape(1, da), ln_b.reshape(1, da), ws_pairs, bs_full)
    return out.reshape(m, da)


def _hyproj_kernel(h_ref, *refs, n1):
    n_proj = _ORDER + 1
    w_refs, (cw_ref, cb_ref, o_ref) = refs[:n_proj], refs[n_proj:]
    c = o_ref.shape[-1]
    h = h_ref[0]
    sub = cw_ref.shape[1]
    tiled = lambda a: a.reshape(n1 // sub, sub, c)
    for j in range(n_proj):
        p = jnp.dot(h, w_refs[j][...], preferred_element_type=jnp.float32)
        blocks = [p[k * n1:(k + 1) * n1] for k in range(_NP)]
        row = lax.broadcasted_iota(jnp.int32, blocks[0].shape, 0)
        before_first = jnp.where(row == 0, 0.0, pltpu.roll(blocks[_NP - 1], 1, axis=0))
        after_last = jnp.where(row == n1 - 1, 0.0, pltpu.roll(blocks[0], n1 - 1, axis=0))
        cols = slice(j * c, (j + 1) * c)
        w0, w1, w2 = cw_ref[0, :, cols], cw_ref[1, :, cols], cw_ref[2, :, cols]
        bias = cb_ref[:, cols]
        for k in range(_NP):
            prev = blocks[k - 1] if k > 0 else before_first
            nxt = blocks[k + 1] if k < _NP - 1 else after_last
            y = ((bias + tiled(prev) * w0) + tiled(blocks[k]) * w1) + tiled(nxt) * w2
            o_ref[j, k * n1:(k + 1) * n1] = y.reshape(n1, c).astype(o_ref.dtype)


def _hyproj(h, w_in, layer, conv_w, conv_b, *, batch, seq):
    m, d = h.shape
    n_proj = _ORDER + 1
    c = conv_w.shape[1] // n_proj
    first = (w_in.shape[2] - n_proj * c) // c
    w_spec = lambda j: pl.BlockSpec((None, d, c), lambda b: (layer, 0, first + j), pipeline_mode=pl.Buffered(1))
    sub = _V7X_F32_SUBLANES
    taps = jnp.broadcast_to(conv_w[:, None, :], (_SHORT_K, sub, n_proj * c))
    bias = jnp.broadcast_to(conv_b[None, :], (sub, n_proj * c))
    return pl.pallas_call(
        functools.partial(_hyproj_kernel, n1=seq // _NP),
        grid=(batch,),
        in_specs=[pl.BlockSpec((1, seq, d), lambda b: (b, 0, 0))] + [w_spec(j) for j in range(n_proj)] + [
            pl.BlockSpec((_SHORT_K, sub, n_proj * c), lambda b: (0, 0, 0)),
            pl.BlockSpec((sub, n_proj * c), lambda b: (0, 0)),
        ],
        out_specs=pl.BlockSpec((n_proj, seq, c), lambda b: (0, b, 0)),
        out_shape=jax.ShapeDtypeStruct((n_proj, m, c), jnp.bfloat16),
        compiler_params=_cparams(("arbitrary",)),
        name="hyproj",
    )(h.reshape(batch, seq, d), *([w_in] * n_proj), taps, bias)


@functools.lru_cache(maxsize=None)
def _dft_tables(seq):
    n1 = seq // _NP
    n_fft = 2 * seq
    k1 = np.arange(_HALF)[None, :, None]
    t = _NP * np.arange(n1)[None, None, :] + np.arange(_NP)[:, None, None]
    theta = ((k1 * t) % n_fft) * (2.0 * math.pi / n_fft)
    valid = k1 <= n1
    cos = np.where(valid, np.cos(theta), 0.0)
    sin = np.where(valid, np.sin(theta), 0.0)
    fwd = np.concatenate([cos, -sin], axis=1)
    weight = np.where((k1 == 0) | (k1 == n1), 1.0, 2.0) / n_fft
    inv = np.concatenate([weight * cos, -weight * sin], axis=1).transpose(0, 2, 1)
    return fwd.astype(np.float32), np.ascontiguousarray(inv).astype(np.float32)


def _fft4(cr, ci):
    d0r, d0i = cr[0] + cr[2], ci[0] + ci[2]
    d1r, d1i = cr[1] + cr[3], ci[1] + ci[3]
    d2r, d2i = cr[0] - cr[2], ci[0] - ci[2]
    er, ei = cr[1] - cr[3], ci[1] - ci[3]
    d3r, d3i = ei, -er
    return [(d0r + d1r, d0i + d1i), (d2r + d3r, d2i + d3i), (d0r - d1r, d0i - d1i), (d2r - d3r, d2i - d3i)]


def _fft8(re, im):
    br, bi = [None] * 8, [None] * 8
    for j in range(4):
        br[j], bi[j] = re[j] + re[j + 4], im[j] + im[j + 4]
        dr, di = re[j] - re[j + 4], im[j] - im[j + 4]
        if j == 0:
            br[4], bi[4] = dr, di
        elif j == 1:
            br[5], bi[5] = (dr + di) * _SQRT_HALF, (di - dr) * _SQRT_HALF
        elif j == 2:
            br[6], bi[6] = di, -dr
        else:
            br[7], bi[7] = (di - dr) * _SQRT_HALF, -(dr + di) * _SQRT_HALF
    even = _fft4(br[:4], bi[:4])
    odd = _fft4(br[4:], bi[4:])
    out = [None] * 8
    for k in range(4):
        out[2 * k], out[2 * k + 1] = even[k], odd[k]
    return [o[0] for o in out], [o[1] for o in out]


def _ifft8(re, im):
    o_im, o_re = _fft8(im, re)
    return o_re, o_im


def _filter_kernel(w1_ref, b1_ref, w2_ref, b2_ref, w3_ref, b3_ref, fr_ref, wof_ref, wob_ref, dl_ref,
                   fwd_ref, o_ref, h_scr, af_scr, ab_scr, *, seq):
    n1 = seq // _NP
    fw = _FILTER_WIDTH
    hp = lax.Precision.HIGHEST

    @pl.when(pl.program_id(1) == 0)
    def _():
        nb = _FILTER_BANDS
        r = lax.broadcasted_iota(jnp.int32, (nb, seq), 1)
        pos = (_NP * (r % n1) + r // n1).astype(jnp.float32)
        band_idx = lax.broadcasted_iota(jnp.int32, (nb, seq), 0).astype(jnp.float32)
        band = 1e-4 + band_idx * ((nb - 1 - 1e-4) / (nb - 1))
        ang = (2.0 * math.pi / seq) * pos * band
        first_row = lax.broadcasted_iota(jnp.int32, (fw - 2 * nb, seq), 0) == 0
        tail = jnp.where(first_row, jnp.concatenate([pos] * ((fw - 2 * nb) // nb), axis=0) * (1.0 / (seq - 1)), 0.0)
        z = jnp.concatenate([jnp.cos(ang), -jnp.sin(ang), tail], axis=0)
        freq = fr_ref[0]
        h = jnp.sin(freq * (jnp.dot(w1_ref[0], z, precision=hp, preferred_element_type=jnp.float32) + b1_ref[0]))
        h = jnp.sin(freq * (jnp.dot(w2_ref[0], h, precision=hp, preferred_element_type=jnp.float32) + b2_ref[0]))
        h = jnp.sin(freq * (jnp.dot(w3_ref[0], h, precision=hp, preferred_element_type=jnp.float32) + b3_ref[0]))
        h_scr[...] = h.T

    r = lax.broadcasted_iota(jnp.int32, (seq, 1), 0)
    pos = (_NP * (r % n1) + r // n1).astype(jnp.float32)
    h = h_scr[...].astype(jnp.bfloat16)
    decay = jnp.exp(-(pos * (1.0 / (seq - 1))) * dl_ref[...])
    hf = jnp.dot(h, wof_ref[0].astype(jnp.bfloat16), preferred_element_type=jnp.float32) * decay
    hb = jnp.dot(h, wob_ref[0].astype(jnp.bfloat16), preferred_element_type=jnp.float32) * decay
    hb = jnp.where(pos == 0.0, 0.0, hb)
    hf = hf.astype(jnp.bfloat16)
    hb = hb.astype(jnp.bfloat16)
    for n2 in range(_NP):
        af_scr[n2] = jnp.dot(fwd_ref[n2], hf[n2 * n1:(n2 + 1) * n1], preferred_element_type=jnp.float32)
        ab_scr[n2] = jnp.dot(fwd_ref[n2], hb[n2 * n1:(n2 + 1) * n1], preferred_element_type=jnp.float32)

    rc = _V7X_BF16_SUBLANES

    def body(i, carry):
        r0 = pl.multiple_of(i * rc, rc)
        fr, fi = _fft8([af_scr[n2, pl.ds(r0, rc)] for n2 in range(_NP)],
                       [af_scr[n2, pl.ds(_HALF + r0, rc)] for n2 in range(_NP)])
        gr, gi = _fft8([ab_scr[n2, pl.ds(r0, rc)] for n2 in range(_NP)],
                       [ab_scr[n2, pl.ds(_HALF + r0, rc)] for n2 in range(_NP)])
        for k2 in range(_NP):
            o_ref[0, 0, k2, 0, pl.ds(r0, rc)] = (fr[k2] + gr[k2]).astype(o_ref.dtype)
            o_ref[0, 0, k2, 1, pl.ds(r0, rc)] = (fi[k2] - gi[k2]).astype(o_ref.dtype)
        return carry

    lax.fori_loop(0, _HALF // rc, body, 0)


def _filter_spectra(w1, b1, w2, b2, w3, b3, freq, w_out, fwd_tab, *, seq, ct):
    n_layers = w1.shape[0]
    c = w_out.shape[2] // (2 * _ORDER)
    n_ct = c // ct
    fw = _FILTER_WIDTH
    w1r = jnp.concatenate([w1[:, 1:], w1[:, :1]], axis=1)
    w1t = jnp.pad(w1r, ((0, 0), (0, fw - w1.shape[1]), (0, 0))).transpose(0, 2, 1)
    col = lambda a: a.reshape(n_layers, fw, 1)
    deltas = jnp.abs(jnp.linspace(_MIN_DECAY, _MAX_DECAY, c, dtype=jnp.float32)).reshape(1, c)
    lmap3 = lambda l, s: (l, 0, 0)
    wo_spec = lambda direction: pl.BlockSpec(
        (1, fw, ct), lambda l, s: (l, 0, (2 * (s // n_ct) + direction) * n_ct + s % n_ct))
    return pl.pallas_call(
        functools.partial(_filter_kernel, seq=seq),
        grid=(n_layers, _ORDER * n_ct),
        in_specs=[
            pl.BlockSpec((1, fw, fw), lmap3), pl.BlockSpec((1, fw, 1), lmap3),
            pl.BlockSpec((1, fw, fw), lmap3), pl.BlockSpec((1, fw, 1), lmap3),
            pl.BlockSpec((1, fw, fw), lmap3), pl.BlockSpec((1, fw, 1), lmap3),
            pl.BlockSpec((1, fw, 1), lmap3),
            wo_spec(0), wo_spec(1),
            pl.BlockSpec((1, ct), lambda l, s: (0, s % n_ct)),
            pl.BlockSpec((_NP, 2 * _HALF, seq // _NP), lambda l, s: (0, 0, 0)),
        ],
        out_specs=pl.BlockSpec((1, 1, _NP, 2, _HALF, ct), lambda l, s: (l, s // n_ct, 0, 0, 0, s % n_ct)),
        out_shape=jax.ShapeDtypeStruct((n_layers, _ORDER, _NP, 2, _HALF, c), _STAGE_TWO_DTYPE),
        scratch_shapes=[pltpu.VMEM((seq, fw), jnp.float32),
                        pltpu.VMEM((_NP, 2 * _HALF, ct), jnp.float32),
                        pltpu.VMEM((_NP, 2 * _HALF, ct), jnp.float32)],
        compiler_params=_cparams(("arbitrary", "arbitrary")),
        name="filter_spectra",
    )(w1t, col(b1), w2.transpose(0, 2, 1), col(b2), w3.transpose(0, 2, 1), col(b3), col(freq),
      w_out, w_out, deltas, fwd_tab)


def _hyena_kernel(v_ref, x1_ref, x2_ref, kf_ref, skip_ref, fwd_ref, inv_ref, o_ref, a_scr, c_scr, z_scr, *, seq):
    n1 = seq // _NP
    rc = _V7X_BF16_SUBLANES
    ct = o_ref.shape[-1]

    def conv_order(order, z_in_ref, gate_ref, z_out_ref):
        for n2 in range(_NP):
            zb = z_in_ref[0, n2 * n1:(n2 + 1) * n1].astype(jnp.bfloat16)
            a_scr[n2] = jnp.dot(fwd_ref[n2], zb, preferred_element_type=jnp.float32).astype(a_scr.dtype)

        def body(i, carry):
            r0 = pl.multiple_of(i * rc, rc)
            for l0 in range(0, ct, _V7X_LANES):
                ls = slice(l0, l0 + _V7X_LANES)
                xr, xi = _fft8([a_scr[n2, pl.ds(r0, rc), ls] for n2 in range(_NP)],
                               [a_scr[n2, pl.ds(_HALF + r0, rc), ls] for n2 in range(_NP)])
                yr, yi = [], []
                for k2 in range(_NP):
                    kr = kf_ref[order, k2, 0, pl.ds(r0, rc), ls]
                    ki = kf_ref[order, k2, 1, pl.ds(r0, rc), ls]
                    yr.append(xr[k2] * kr - xi[k2] * ki)
                    yi.append(xr[k2] * ki + xi[k2] * kr)
                cr, ci = _ifft8(yr, yi)
                for t2 in range(_NP):
                    c_scr[t2, pl.ds(r0, rc), ls] = cr[t2].astype(jnp.bfloat16)
                    c_scr[t2, pl.ds(_HALF + r0, rc), ls] = ci[t2].astype(jnp.bfloat16)
            return carry

        lax.fori_loop(0, _HALF // rc, body, 0)

        skip = skip_ref[order:order + 1]
        for t2 in range(_NP):
            rows = slice(t2 * n1, (t2 + 1) * n1)
            y = jnp.dot(inv_ref[t2], c_scr[t2], preferred_element_type=jnp.float32)
            z = z_in_ref[0, rows].astype(jnp.float32)
            gate = gate_ref[0, rows].astype(jnp.float32)
            z_out_ref[0, rows] = (gate * (y + z * skip)).astype(z_out_ref.dtype)

    conv_order(0, v_ref, x1_ref, z_scr)
    conv_order(1, z_scr, x2_ref, o_ref)


def _hyena(proj, kf_all, layer, skip, fwd_tab, inv_tab, *, batch, seq, ct):
    n_proj, m, c = proj.shape
    n1 = seq // _NP
    proj4 = proj.reshape(n_proj, batch, seq, c)
    pspec = lambda j: pl.BlockSpec((None, 1, seq, ct), lambda t, b, j=j: (j, b, 0, t))
    resident = dict(pipeline_mode=pl.Buffered(1))
    out = pl.pallas_call(
        functools.partial(_hyena_kernel, seq=seq),
        grid=(c // ct, batch),
        in_specs=[
            pspec(0), pspec(1), pspec(2),
            pl.BlockSpec((None, _ORDER, _NP, 2, _HALF, ct), lambda t, b: (layer, 0, 0, 0, 0, t), **resident),
            pl.BlockSpec((_ORDER, ct), lambda t, b: (0, t)),
            pl.BlockSpec((_NP, 2 * _HALF, n1), lambda t, b: (0, 0, 0), **resident),
            pl.BlockSpec((_NP, n1, 2 * _HALF), lambda t, b: (0, 0, 0), **resident),
        ],
        out_specs=pl.BlockSpec((1, seq, ct), lambda t, b: (b, 0, t)),
        out_shape=jax.ShapeDtypeStruct((batch, seq, c), jnp.bfloat16),
        scratch_shapes=[pltpu.VMEM((_NP, 2 * _HALF, ct), _STAGE_TWO_DTYPE),
                        pltpu.VMEM((_NP, 2 * _HALF, ct), jnp.bfloat16),
                        pltpu.VMEM((1, seq, ct), jnp.float32)],
        compiler_params=_cparams(("arbitrary", "arbitrary")),
        name="hyena",
    )(proj4, proj4, proj4, kf_all, skip, fwd_tab, inv_tab)
    return out.reshape(m, c)


def kernel(x, ffn1_pre_g, ffn1_w_gate, ffn1_w_up, ffn1_w_down, ffn1_post_g, mix_pre_g, mix_w_in, gmlp_ln_g, gmlp_ln_b, gmlp_w_s, gmlp_b_s, hy_conv_w, hy_conv_b, hy_filt_w1, hy_filt_b1, hy_filt_w2, hy_filt_b2, hy_filt_w3, hy_filt_b3, hy_filt_freq, hy_filt_w_out, hy_skip, mix_w_out, mix_post_g, ffn2_pre_g, ffn2_w_gate, ffn2_w_up, ffn2_w_down, ffn2_post_g):
    batch, seq, d = x.shape
    depth = ffn1_pre_g.shape[0]
    da = gmlp_ln_g.shape[1]
    assert seq % (_NP * _CHUNK) == 0 and seq // _NP + 1 <= _HALF
    m = batch * seq
    ffn_tiles = dict(batch=batch, tm=512, chunk=_V7X_MXU_DIM)
    ct = _V7X_MXU_DIM
    xp = x.reshape(m, d)

    fwd_np, inv_np = _dft_tables(seq)
    fwd_bf16 = jnp.asarray(fwd_np).astype(jnp.bfloat16)
    inv_bf16 = jnp.asarray(inv_np).astype(jnp.bfloat16)
    kf_all = _filter_spectra(hy_filt_w1, hy_filt_b1, hy_filt_w2, hy_filt_b2, hy_filt_w3, hy_filt_b3,
                             hy_filt_freq, hy_filt_w_out, fwd_bf16, seq=seq, ct=ct)

    ffn1_w = (ffn1_w_gate, ffn1_w_up, ffn1_w_down)
    ffn2_w = (ffn2_w_gate, ffn2_w_up, ffn2_w_down)
    w_in = mix_w_in
    w_out = mix_w_out.astype(jnp.bfloat16)

    for l in range(depth):
        xp, h = _ffn(xp, l, ffn1_pre_g, *ffn1_w, ffn1_post_g, g_next=mix_pre_g,
                     relayout="deinterleave" if l == 0 else None, **ffn_tiles)
        ya = _gmlp(h, w_in, l, gmlp_ln_g[l], gmlp_ln_b[l], gmlp_w_s[l], gmlp_b_s[l], batch=batch, seq=seq, jb=8)
        proj = _hyproj(h, w_in, l, hy_conv_w[l], hy_conv_b[l], batch=batch, seq=seq)
        yb = _hyena(proj, kf_all, l, hy_skip[l], fwd_bf16, inv_bf16, batch=batch, seq=seq, ct=proj.shape[-1])
        xp = _ffn(xp, l, ffn2_pre_g, *ffn2_w, ffn2_post_g, mix=(ya, yb, w_out, mix_post_g),
                  relayout="interleave" if l == depth - 1 else None, **ffn_tiles)

    return xp.reshape(batch, seq, d)
```

```python
import functools
import math

import jax
import jax.numpy as jnp
import numpy as np
from jax import lax
from jax.experimental import pallas as pl
from jax.experimental.pallas import tpu as pltpu

_CHUNK = 128
_HEADS_A = 8
_HEAD_DIM_A = 64
_ORDER = 2
_SHORT_K = 3
_FILTER_BANDS = 16
_FILTER_WIDTH = 64
_DECAY_TARGET = 1e-2
_MAX_DECAY = math.log(_DECAY_TARGET) / 0.3
_MIN_DECAY = math.log(_DECAY_TARGET) / 1.5
_HALF_STEP = 0.5
_RMS_EPS = 1e-6
_LN_EPS = 1e-5

_V7X_LANES = 128
_V7X_F32_SUBLANES = 8
_V7X_BF16_SUBLANES = 16
_V7X_MXU_DIM = 256
_V7X_VMEM_LIMIT_BYTES = 60 * 1024 * 1024

_NP = 8
_HALF = 272
_SQRT_HALF = 0.7071067811865476
_STAGE_TWO_DTYPE = jnp.bfloat16


def _cparams(semantics):
    return pltpu.CompilerParams(dimension_semantics=semantics, vmem_limit_bytes=_V7X_VMEM_LIMIT_BYTES)


def _rms_norm(x, g):
    return x * lax.rsqrt(jnp.mean(x * x, axis=-1, keepdims=True) + _RMS_EPS) * g


def _deinterleave_rows(val, slab_scr):
    rows, d = val.shape
    n_slabs = d // _V7X_LANES
    for k in range(n_slabs):
        slab_scr[k] = val[:, k * _V7X_LANES:(k + 1) * _V7X_LANES]
    return jnp.concatenate(
        [jnp.concatenate([slab_scr[k, pl.ds(n2, rows // _NP, stride=_NP), :] for k in range(n_slabs)], axis=1)
         for n2 in range(_NP)], axis=0)


def _interleave_rows(val, slab_scr):
    rows, d = val.shape
    per = rows // _NP
    n_slabs = d // _V7X_LANES
    for n2 in range(_NP):
        for k in range(n_slabs):
            slab_scr[k, pl.ds(n2, per, stride=_NP), :] = val[n2 * per:(n2 + 1) * per, k * _V7X_LANES:(k + 1) * _V7X_LANES]
    return jnp.concatenate([slab_scr[k] for k in range(n_slabs)], axis=1)


def _ffn_kernel(*refs, chunk, mix_in, norm_out, relayout):
    refs = list(refs)
    slab_scr = refs.pop() if relayout else None
    x_ref = refs.pop(0)
    blocked_in = relayout == "interleave"
    load = (lambda r: r[0].reshape(-1, r.shape[-1])) if blocked_in else (lambda r: r[...])
    x = load(x_ref)
    if mix_in:
        ya_ref, yb_ref, wa_ref, wb_ref, gmix_ref = refs[:5]
        refs = refs[5:]
        y = jnp.dot(load(ya_ref), wa_ref[...], preferred_element_type=jnp.float32)
        y = y + jnp.dot(load(yb_ref), wb_ref[...], preferred_element_type=jnp.float32)
        x = x + _rms_norm(y, gmix_ref[...])
    gpre_ref, wg_ref, wu_ref, wd_ref, gpost_ref = refs[:5]
    refs = refs[5:]
    h = _rms_norm(x, gpre_ref[...])
    acc = jnp.zeros(x.shape, jnp.float32)
    dff = wg_ref.shape[1]
    for c0 in range(0, dff, chunk):
        c1 = min(c0 + chunk, dff)
        g = jnp.dot(h, wg_ref[:, c0:c1], preferred_element_type=jnp.float32)
        u = jnp.dot(h, wu_ref[:, c0:c1], preferred_element_type=jnp.float32)
        a = g * jax.nn.sigmoid(g) * u
        acc = acc + jnp.dot(a, wd_ref[c0:c1, :], preferred_element_type=jnp.float32)
    out = x + _rms_norm(acc, gpost_ref[...])
    if relayout == "deinterleave":
        out = _deinterleave_rows(out, slab_scr)
    elif relayout == "interleave":
        out = _interleave_rows(out, slab_scr)
    blocked_out = relayout == "deinterleave"
    if norm_out:
        gnext_ref, o_ref, hn_ref = refs
        hn = _rms_norm(out, gnext_ref[...]).astype(jnp.bfloat16)
        if blocked_out:
            hn_ref[0] = hn.reshape(hn_ref.shape[1:])
        else:
            hn_ref[...] = hn
    else:
        (o_ref,) = refs
    if blocked_out:
        o_ref[0] = out.reshape(o_ref.shape[1:])
    else:
        o_ref[...] = out


def _ffn(x, layer, g_pre, w_gate, w_up, w_down, g_post, *, batch, tm, chunk, mix=None, g_next=None, relayout=None):
    m, d = x.shape
    dff = w_gate.shape[2]
    seq = m // batch
    steps_per_seq = seq // tm
    assert dff % _V7X_MXU_DIM == 0 and chunk % _V7X_MXU_DIM == 0 and seq % tm == 0 and tm % (_NP * _V7X_BF16_SUBLANES) == 0
    row = lambda a: a[layer].reshape(1, d)
    flat_tile = lambda width: pl.BlockSpec((tm, width), lambda i: (i, 0))
    blocked_tile = lambda width: pl.BlockSpec((1, _NP, tm // _NP, width),
                                              lambda i: (i // steps_per_seq, 0, i % steps_per_seq, 0))
    blocked = lambda a: a.reshape(batch, _NP, seq // _NP, a.shape[-1])
    if relayout == "interleave":
        in_tile, put = blocked_tile, blocked
    else:
        in_tile, put = flat_tile, lambda a: a
    vec = pl.BlockSpec((1, d), lambda i: (0, 0))
    resident = lambda r, c, blk=0: pl.BlockSpec((None, r, c), lambda i: (layer, blk, 0), pipeline_mode=pl.Buffered(1))
    args, specs = [put(x)], [in_tile(d)]
    if mix is not None:
        ya, yb, w_out, g_mix = mix
        da = ya.shape[1]
        assert yb.shape[1] == da and w_out.shape[1:] == (2 * da, d)
        args += [put(ya), put(yb), w_out, w_out, row(g_mix)]
        specs += [in_tile(da), in_tile(da), resident(da, d, 0), resident(da, d, 1), vec]
    args += [row(g_pre), w_gate, w_up, w_down, _HALF_STEP * row(g_post)]
    specs += [vec, resident(d, dff), resident(d, dff), resident(dff, d), vec]
    if relayout == "deinterleave":
        out_tile, out_dims = blocked_tile(d), (batch, _NP, seq // _NP, d)
    else:
        out_tile, out_dims = flat_tile(d), (m, d)
    out_shape = [jax.ShapeDtypeStruct(out_dims, jnp.float32)]
    out_specs = [out_tile]
    if g_next is not None:
        args.append(row(g_next))
        specs.append(vec)
        out_shape.append(jax.ShapeDtypeStruct(out_dims, jnp.bfloat16))
        out_specs.append(out_tile)
    outs = pl.pallas_call(
        functools.partial(_ffn_kernel, chunk=chunk, mix_in=mix is not None, norm_out=g_next is not None,
                          relayout=relayout),
        grid=(m // tm,),
        in_specs=specs,
        out_specs=out_specs,
        out_shape=out_shape,
        scratch_shapes=[pltpu.VMEM((d // _V7X_LANES, tm, _V7X_LANES), jnp.float32)] if relayout else [],
        compiler_params=_cparams(("arbitrary",)),
        name="ffn",
    )(*args)
    outs = [o.reshape(m, d) for o in outs]
    return outs if g_next is not None else outs[0]


def _gmlp_kernel(h_ref, w_ref, lng_ref, lnb_ref, ws_ref, bs_ref, o_ref, *, jb, da):
    rows = _CHUNK // _NP
    d = h_ref.shape[-1]
    h = h_ref[...].reshape(_NP * jb * rows, d)
    gelu = lambda p: 0.5 * p * (1.0 + lax.erf(p * _SQRT_HALF))
    u = gelu(jnp.dot(h, w_ref[:, :da], preferred_element_type=jnp.float32))
    v = gelu(jnp.dot(h, w_ref[:, da:], preferred_element_type=jnp.float32))
    mu = jnp.mean(v, axis=-1, keepdims=True)
    vc = v - mu
    var = jnp.mean(vc * vc, axis=-1, keepdims=True)
    vn = (vc * lax.rsqrt(var + _LN_EPS) * lng_ref[...] + lnb_ref[...]).astype(jnp.bfloat16)
    lane = lax.broadcasted_iota(jnp.int32, (_CHUNK, _V7X_LANES), 1)
    first_head = lane < _HEAD_DIM_A
    n_pairs = da // _V7X_LANES
    for j in range(jb):
        starts = [(n2 * jb + j) * rows for n2 in range(_NP)]
        vchunk = jnp.concatenate([vn[s:s + rows] for s in starts], axis=0)
        mixed = []
        for q in range(n_pairs):
            r = jnp.dot(ws_ref[q], vchunk[:, q * _V7X_LANES:(q + 1) * _V7X_LANES],
                        preferred_element_type=jnp.float32)
            mixed.append(jnp.where(first_head, r[:_CHUNK], r[_CHUNK:]))
        mixed = jnp.concatenate(mixed, axis=1) + bs_ref[...]
        for n2 in range(_NP):
            s = starts[n2]
            o_ref[n2, j * rows:(j + 1) * rows] = (u[s:s + rows] * mixed[n2 * rows:(n2 + 1) * rows]).astype(o_ref.dtype)


def _gmlp(h, w_in, layer, ln_g, ln_b, w_s, b_s, *, batch, seq, jb):
    m, d = h.shape
    da = ln_g.shape[0]
    rows = _CHUNK // _NP
    n_chunks = seq // _CHUNK
    assert rows == _V7X_BF16_SUBLANES
    n1 = seq // _NP
    h3 = h.reshape(batch * _NP, n1, d)
    ws_p = w_s.reshape(_HEADS_A, rows, _NP, rows, _NP).transpose(0, 2, 1, 4, 3).reshape(_HEADS_A, _CHUNK, _CHUNK)
    ws_pairs = ws_p.reshape(_HEADS_A // 2, 2 * _CHUNK, _CHUNK).astype(jnp.bfloat16)
    bs_p = b_s.reshape(_HEADS_A, rows, _NP).transpose(0, 2, 1).reshape(_HEADS_A, _CHUNK)
    bs_full = jnp.repeat(bs_p.T, _HEAD_DIM_A, axis=1)
    out = pl.pallas_call(
        functools.partial(_gmlp_kernel, jb=jb, da=da),
        grid=(batch, n_chunks // jb),
        in_specs=[
            pl.BlockSpec((_NP, jb * rows, d), lambda b, j: (b, j, 0)),
            pl.BlockSpec((None, d, 2 * da), lambda b, j: (layer, 0, 0)),
            pl.BlockSpec((1, da), lambda b, j: (0, 0)),
            pl.BlockSpec((1, da), lambda b, j: (0, 0)),
            pl.BlockSpec((_HEADS_A // 2, 2 * _CHUNK, _CHUNK), lambda b, j: (0, 0, 0)),
            pl.BlockSpec((_CHUNK, da), lambda b, j: (0, 0)),
        ],
        out_specs=pl.BlockSpec((_NP, jb * rows, da), lambda b, j: (b, j, 0)),
        out_shape=jax.ShapeDtypeStruct((batch * _NP, n1, da), jnp.bfloat16),
        compiler_params=_cparams(("arbitrary", "arbitrary")),
        name="gmlp",
    )(h3, w_in, ln_g.reshape(1, da), ln_b.reshape(1, da), ws_pairs, bs_full)
    return out.reshape(m, da)


def _hyproj_kernel(h_ref, *refs, n1):
    n_proj = _ORDER + 1
    w_refs, (cw_ref, cb_ref, o_ref) = refs[:n_proj], refs[n_proj:]
    c = o_ref.shape[-1]
    h = h_ref[0]
    sub = cw_ref.shape[1]
    tiled = lambda a: a.reshape(n1 // sub, sub, c)
    for j in range(n_proj):
        p = jnp.dot(h, w_refs[j][...], preferred_element_type=jnp.float32)
        blocks = [p[k * n1:(k + 1) * n1] for k in range(_NP)]
        row = lax.broadcasted_iota(jnp.int32, blocks[0].shape, 0)
        before_first = jnp.where(row == 0, 0.0, pltpu.roll(blocks[_NP - 1], 1, axis=0))
        after_last = jnp.where(row == n1 - 1, 0.0, pltpu.roll(blocks[0], n1 - 1, axis=0))
        cols = slice(j * c, (j + 1) * c)
        w0, w1, w2 = cw_ref[0, :, cols], cw_ref[1, :, cols], cw_ref[2, :, cols]
        bias = cb_ref[:, cols]
        for k in range(_NP):
            prev = blocks[k - 1] if k > 0 else before_first
            nxt = blocks[k + 1] if k < _NP - 1 else after_last
            y = ((bias + tiled(prev) * w0) + tiled(blocks[k]) * w1) + tiled(nxt) * w2
            o_ref[j, k * n1:(k + 1) * n1] = y.reshape(n1, c).astype(o_ref.dtype)


def _hyproj(h, w_in, layer, conv_w, conv_b, *, batch, seq):
    m, d = h.shape
    n_proj = _ORDER + 1
    c = conv_w.shape[1] // n_proj
    first = (w_in.shape[2] - n_proj * c) // c
    w_spec = lambda j: pl.BlockSpec((None, d, c), lambda b: (layer, 0, first + j), pipeline_mode=pl.Buffered(1))
    sub = _V7X_F32_SUBLANES
    taps = jnp.broadcast_to(conv_w[:, None, :], (_SHORT_K, sub, n_proj * c))
    bias = jnp.broadcast_to(conv_b[None, :], (sub, n_proj * c))
    return pl.pallas_call(
        functools.partial(_hyproj_kernel, n1=seq // _NP),
        grid=(batch,),
        in_specs=[pl.BlockSpec((1, seq, d), lambda b: (b, 0, 0))] + [w_spec(j) for j in range(n_proj)] + [
            pl.BlockSpec((_SHORT_K, sub, n_proj * c), lambda b: (0, 0, 0)),
            pl.BlockSpec((sub, n_proj * c), lambda b: (0, 0)),
        ],
        out_specs=pl.BlockSpec((n_proj, seq, c), lambda b: (0, b, 0)),
        out_shape=jax.ShapeDtypeStruct((n_proj, m, c), jnp.bfloat16),
        compiler_params=_cparams(("arbitrary",)),
        name="hyproj",
    )(h.reshape(batch, seq, d), *([w_in] * n_proj), taps, bias)


@functools.lru_cache(maxsize=None)
def _dft_tables(seq):
    n1 = seq // _NP
    n_fft = 2 * seq
    k1 = np.arange(_HALF)[None, :, None]
    t = _NP * np.arange(n1)[None, None, :] + np.arange(_NP)[:, None, None]
    theta = ((k1 * t) % n_fft) * (2.0 * math.pi / n_fft)
    valid = k1 <= n1
    cos = np.where(valid, np.cos(theta), 0.0)
    sin = np.where(valid, np.sin(theta), 0.0)
    fwd = np.concatenate([cos, -sin], axis=1)
    weight = np.where((k1 == 0) | (k1 == n1), 1.0, 2.0) / n_fft
    inv = np.concatenate([weight * cos, -weight * sin], axis=1).transpose(0, 2, 1)
    return fwd.astype(np.float32), np.ascontiguousarray(inv).astype(np.float32)


def _fft4(cr, ci):
    d0r, d0i = cr[0] + cr[2], ci[0] + ci[2]
    d1r, d1i = cr[1] + cr[3], ci[1] + ci[3]
    d2r, d2i = cr[0] - cr[2], ci[0] - ci[2]
    er, ei = cr[1] - cr[3], ci[1] - ci[3]
    d3r, d3i = ei, -er
    return [(d0r + d1r, d0i + d1i), (d2r + d3r, d2i + d3i), (d0r - d1r, d0i - d1i), (d2r - d3r, d2i - d3i)]


def _fft8(re, im):
    br, bi = [None] * 8, [None] * 8
    for j in range(4):
        br[j], bi[j] = re[j] + re[j + 4], im[j] + im[j + 4]
        dr, di = re[j] - re[j + 4], im[j] - im[j + 4]
        if j == 0:
            br[4], bi[4] = dr, di
        elif j == 1:
            br[5], bi[5] = (dr + di) * _SQRT_HALF, (di - dr) * _SQRT_HALF
        elif j == 2:
            br[6], bi[6] = di, -dr
        else:
            br[7], bi[7] = (di - dr) * _SQRT_HALF, -(dr + di) * _SQRT_HALF
    even = _fft4(br[:4], bi[:4])
    odd = _fft4(br[4:], bi[4:])
    out = [None] * 8
    for k in range(4):
        out[2 * k], out[2 * k + 1] = even[k], odd[k]
    return [o[0] for o in out], [o[1] for o in out]


def _ifft8(re, im):
    o_im, o_re = _fft8(im, re)
    return o_re, o_im


def _filter_kernel(w1_ref, b1_ref, w2_ref, b2_ref, w3_ref, b3_ref, fr_ref, wof_ref, wob_ref, dl_ref,
                   fwd_ref, o_ref, h_scr, af_scr, ab_scr, *, seq):
    n1 = seq // _NP
    fw = _FILTER_WIDTH
    hp = lax.Precision.HIGHEST

    @pl.when(pl.program_id(1) == 0)
    def _():
        nb = _FILTER_BANDS
        r = lax.broadcasted_iota(jnp.int32, (nb, seq), 1)
        pos = (_NP * (r % n1) + r // n1).astype(jnp.float32)
        band_idx = lax.broadcasted_iota(jnp.int32, (nb, seq), 0).astype(jnp.float32)
        band = 1e-4 + band_idx * ((nb - 1 - 1e-4) / (nb - 1))
        ang = (2.0 * math.pi / seq) * pos * band
        first_row = lax.broadcasted_iota(jnp.int32, (fw - 2 * nb, seq), 0) == 0
        tail = jnp.where(first_row, jnp.concatenate([pos] * ((fw - 2 * nb) // nb), axis=0) * (1.0 / (seq - 1)), 0.0)
        z = jnp.concatenate([jnp.cos(ang), -jnp.sin(ang), tail], axis=0)
        freq = fr_ref[0]
        h = jnp.sin(freq * (jnp.dot(w1_ref[0], z, precision=hp, preferred_element_type=jnp.float32) + b1_ref[0]))
        h = jnp.sin(freq * (jnp.dot(w2_ref[0], h, precision=hp, preferred_element_type=jnp.float32) + b2_ref[0]))
        h = jnp.sin(freq * (jnp.dot(w3_ref[0], h, precision=hp, preferred_element_type=jnp.float32) + b3_ref[0]))
        h_scr[...] = h.T

    r = lax.broadcasted_iota(jnp.int32, (seq, 1), 0)
    pos = (_NP * (r % n1) + r // n1).astype(jnp.float32)
    h = h_scr[...].astype(jnp.bfloat16)
    decay = jnp.exp(-(pos * (1.0 / (seq - 1))) * dl_ref[...])
    hf = jnp.dot(h, wof_ref[0].astype(jnp.bfloat16), preferred_element_type=jnp.float32) * decay
    hb = jnp.dot(h, wob_ref[0].astype(jnp.bfloat16), preferred_element_type=jnp.float32) * decay
    hb = jnp.where(pos == 0.0, 0.0, hb)
    hf = hf.astype(jnp.bfloat16)
    hb = hb.astype(jnp.bfloat16)
    for n2 in range(_NP):
        af_scr[n2] = jnp.dot(fwd_ref[n2], hf[n2 * n1:(n2 + 1) * n1], preferred_element_type=jnp.float32)
        ab_scr[n2] = jnp.dot(fwd_ref[n2], hb[n2 * n1:(n2 + 1) * n1], preferred_element_type=jnp.float32)

    rc = _V7X_BF16_SUBLANES

    def body(i, carry):
        r0 = pl.multiple_of(i * rc, rc)
        fr, fi = _fft8([af_scr[n2, pl.ds(r0, rc)] for n2 in range(_NP)],
                       [af_scr[n2, pl.ds(_HALF + r0, rc)] for n2 in range(_NP)])
        gr, gi = _fft8([ab_scr[n2, pl.ds(r0, rc)] for n2 in range(_NP)],
                       [ab_scr[n2, pl.ds(_HALF + r0, rc)] for n2 in range(_NP)])
        for k2 in range(_NP):
            o_ref[0, 0, k2, 0, pl.ds(r0, rc)] = (fr[k2] + gr[k2]).astype(o_ref.dtype)
            o_ref[0, 0, k2, 1, pl.ds(r0, rc)] = (fi[k2] - gi[k2]).astype(o_ref.dtype)
        return carry

    lax.fori_loop(0, _HALF // rc, body, 0)


def _filter_spectra(w1, b1, w2, b2, w3, b3, freq, w_out, fwd_tab, *, seq, ct):
    n_layers = w1.shape[0]
    c = w_out.shape[2] // (2 * _ORDER)
    n_ct = c // ct
    fw = _FILTER_WIDTH
    w1r = jnp.concatenate([w1[:, 1:], w1[:, :1]], axis=1)
    w1t = jnp.pad(w1r, ((0, 0), (0, fw - w1.shape[1]), (0, 0))).transpose(0, 2, 1)
    col = lambda a: a.reshape(n_layers, fw, 1)
    deltas = jnp.abs(jnp.linspace(_MIN_DECAY, _MAX_DECAY, c, dtype=jnp.float32)).reshape(1, c)
    lmap3 = lambda l, s: (l, 0, 0)
    wo_spec = lambda direction: pl.BlockSpec(
        (1, fw, ct), lambda l, s: (l, 0, (2 * (s // n_ct) + direction) * n_ct + s % n_ct))
    return pl.pallas_call(
        functools.partial(_filter_kernel, seq=seq),
        grid=(n_layers, _ORDER * n_ct),
        in_specs=[
            pl.BlockSpec((1, fw, fw), lmap3), pl.BlockSpec((1, fw, 1), lmap3),
            pl.BlockSpec((1, fw, fw), lmap3), pl.BlockSpec((1, fw, 1), lmap3),
            pl.BlockSpec((1, fw, fw), lmap3), pl.BlockSpec((1, fw, 1), lmap3),
            pl.BlockSpec((1, fw, 1), lmap3),
            wo_spec(0), wo_spec(1),
            pl.BlockSpec((1, ct), lambda l, s: (0, s % n_ct)),
            pl.BlockSpec((_NP, 2 * _HALF, seq // _NP), lambda l, s: (0, 0, 0)),
        ],
        out_specs=pl.BlockSpec((1, 1, _NP, 2, _HALF, ct), lambda l, s: (l, s // n_ct, 0, 0, 0, s % n_ct)),
        out_shape=jax.ShapeDtypeStruct((n_layers, _ORDER, _NP, 2, _HALF, c), _STAGE_TWO_DTYPE),
        scratch_shapes=[pltpu.VMEM((seq, fw), jnp.float32),
                        pltpu.VMEM((_NP, 2 * _HALF, ct), jnp.float32),
                        pltpu.VMEM((_NP, 2 * _HALF, ct), jnp.float32)],
        compiler_params=_cparams(("arbitrary", "arbitrary")),
        name="filter_spectra",
    )(w1t, col(b1), w2.transpose(0, 2, 1), col(b2), w3.transpose(0, 2, 1), col(b3), col(freq),
      w_out, w_out, deltas, fwd_tab)


def _hyena_kernel(v_ref, x1_ref, x2_ref, kf_ref, skip_ref, fwd_ref, inv_ref, o_ref, a_scr, c_scr, z_scr, *, seq):
    n1 = seq // _NP
    rc = _V7X_BF16_SUBLANES
    ct = o_ref.shape[-1]

    def conv_order(order, z_in_ref, gate_ref, z_out_ref):
        for n2 in range(_NP):
            zb = z_in_ref[0, n2 * n1:(n2 + 1) * n1].astype(jnp.bfloat16)
            a_scr[n2] = jnp.dot(fwd_ref[n2], zb, preferred_element_type=jnp.float32).astype(a_scr.dtype)

        def body(i, carry):
            r0 = pl.multiple_of(i * rc, rc)
            for l0 in range(0, ct, _V7X_LANES):
                ls = slice(l0, l0 + _V7X_LANES)
                xr, xi = _fft8([a_scr[n2, pl.ds(r0, rc), ls] for n2 in range(_NP)],
                               [a_scr[n2, pl.ds(_HALF + r0, rc), ls] for n2 in range(_NP)])
                yr, yi = [], []
                for k2 in range(_NP):
                    kr = kf_ref[order, k2, 0, pl.ds(r0, rc), ls]
                    ki = kf_ref[order, k2, 1, pl.ds(r0, rc), ls]
                    yr.append(xr[k2] * kr - xi[k2] * ki)
                    yi.append(xr[k2] * ki + xi[k2] * kr)
                cr, ci = _ifft8(yr, yi)
                for t2 in range(_NP):
                    c_scr[t2, pl.ds(r0, rc), ls] = cr[t2].astype(jnp.bfloat16)
                    c_scr[t2, pl.ds(_HALF + r0, rc), ls] = ci[t2].astype(jnp.bfloat16)
            return carry

        lax.fori_loop(0, _HALF // rc, body, 0)

        skip = skip_ref[order:order + 1]
        for t2 in range(_NP):
            rows = slice(t2 * n1, (t2 + 1) * n1)
            y = jnp.dot(inv_ref[t2], c_scr[t2], preferred_element_type=jnp.float32)
            z = z_in_ref[0, rows].astype(jnp.float32)
            gate = gate_ref[0, rows].astype(jnp.float32)
            z_out_ref[0, rows] = (gate * (y + z * skip)).astype(z_out_ref.dtype)

    conv_order(0, v_ref, x1_ref, z_scr)
    conv_order(1, z_scr, x2_ref, o_ref)


def _hyena(proj, kf_all, layer, skip, fwd_tab, inv_tab, *, batch, seq, ct):
    n_proj, m, c = proj.shape
    n1 = seq // _NP
    proj4 = proj.reshape(n_proj, batch, seq, c)
    pspec = lambda j: pl.BlockSpec((None, 1, seq, ct), lambda t, b, j=j: (j, b, 0, t))
    resident = dict(pipeline_mode=pl.Buffered(1))
    out = pl.pallas_call(
        functools.partial(_hyena_kernel, seq=seq),
        grid=(c // ct, batch),
        in_specs=[
            pspec(0), pspec(1), pspec(2),
            pl.BlockSpec((None, _ORDER, _NP, 2, _HALF, ct), lambda t, b: (layer, 0, 0, 0, 0, t), **resident),
            pl.BlockSpec((_ORDER, ct), lambda t, b: (0, t)),
            pl.BlockSpec((_NP, 2 * _HALF, n1), lambda t, b: (0, 0, 0), **resident),
            pl.BlockSpec((_NP, n1, 2 * _HALF), lambda t, b: (0, 0, 0), **resident),
        ],
        out_specs=pl.BlockSpec((1, seq, ct), lambda t, b: (b, 0, t)),
        out_shape=jax.ShapeDtypeStruct((batch, seq, c), jnp.bfloat16),
        scratch_shapes=[pltpu.VMEM((_NP, 2 * _HALF, ct), _STAGE_TWO_DTYPE),
                        pltpu.VMEM((_NP, 2 * _HALF, ct), jnp.bfloat16),
                        pltpu.VMEM((1, seq, ct), jnp.float32)],
        compiler_params=_cparams(("arbitrary", "arbitrary")),
        name="hyena",
    )(proj4, proj4, proj4, kf_all, skip, fwd_tab, inv_tab)
    return out.reshape(m, c)


def kernel(x, ffn1_pre_g, ffn1_w_gate, ffn1_w_up, ffn1_w_down, ffn1_post_g, mix_pre_g, mix_w_in, gmlp_ln_g, gmlp_ln_b, gmlp_w_s, gmlp_b_s, hy_conv_w, hy_conv_b, hy_filt_w1, hy_filt_b1, hy_filt_w2, hy_filt_b2, hy_filt_w3, hy_filt_b3, hy_filt_freq, hy_filt_w_out, hy_skip, mix_w_out, mix_post_g, ffn2_pre_g, ffn2_w_gate, ffn2_w_up, ffn2_w_down, ffn2_post_g):
    batch, seq, d = x.shape
    depth = ffn1_pre_g.shape[0]
    da = gmlp_ln_g.shape[1]
    assert seq % (_NP * _CHUNK) == 0 and seq // _NP + 1 <= _HALF
    m = batch * seq
    ffn_tiles = dict(batch=batch, tm=512, chunk=_V7X_MXU_DIM)
    ct = _V7X_MXU_DIM
    xp = x.reshape(m, d)

    fwd_np, inv_np = _dft_tables(seq)
    fwd_bf16 = jnp.asarray(fwd_np).astype(jnp.bfloat16)
    inv_bf16 = jnp.asarray(inv_np).astype(jnp.bfloat16)
    kf_all = _filter_spectra(hy_filt_w1, hy_filt_b1, hy_filt_w2, hy_filt_b2, hy_filt_w3, hy_filt_b3,
                             hy_filt_freq, hy_filt_w_out, fwd_bf16, seq=seq, ct=ct)

    ffn1_w = (ffn1_w_gate, ffn1_w_up, ffn1_w_down)
    ffn2_w = (ffn2_w_gate, ffn2_w_up, ffn2_w_down)
    w_in = mix_w_in
    w_out = mix_w_out.astype(jnp.bfloat16)

    for l in range(depth):
        xp, h = _ffn(xp, l, ffn1_pre_g, *ffn1_w, ffn1_post_g, g_next=mix_pre_g,
                     relayout="deinterleave" if l == 0 else None, **ffn_tiles)
        ya = _gmlp(h, w_in, l, gmlp_ln_g[l], gmlp_ln_b[l], gmlp_w_s[l], gmlp_b_s[l], batch=batch, seq=seq, jb=8)
        proj = _hyproj(h, w_in, l, hy_conv_w[l], hy_conv_b[l], batch=batch, seq=seq)
        yb = _hyena(proj, kf_all, l, hy_skip[l], fwd_bf16, inv_bf16, batch=batch, seq=seq, ct=proj.shape[-1])
        xp = _ffn(xp, l, ffn2_pre_g, *ffn2_w, ffn2_post_g, mix=(ya, yb, w_out, mix_post_g),
                  relayout="interleave" if l == depth - 1 else None, **ffn_tiles)

    return xp.reshape(batch, seq, d)
```

```python
import functools
import math

import jax
import jax.numpy as jnp
import numpy as np
from jax import lax
from jax.experimental import pallas as pl
from jax.experimental.pallas import tpu as pltpu

_CHUNK = 128
_HEADS_A = 8
_HEAD_DIM_A = 64
_ORDER = 2
_SHORT_K = 3
_FILTER_BANDS = 16
_FILTER_WIDTH = 64
_DECAY_TARGET = 1e-2
_MAX_DECAY = math.log(_DECAY_TARGET) / 0.3
_MIN_DECAY = math.log(_DECAY_TARGET) / 1.5
_HALF_STEP = 0.5
_RMS_EPS = 1e-6
_LN_EPS = 1e-5

_V7X_LANES = 128
_V7X_F32_SUBLANES = 8
_V7X_BF16_SUBLANES = 16
_V7X_MXU_DIM = 256
_V7X_VMEM_LIMIT_BYTES = 60 * 1024 * 1024

_NP = 8
_HALF = 272
_SQRT_HALF = 0.7071067811865476
_STAGE_TWO_DTYPE = jnp.bfloat16


def _cparams(semantics):
    return pltpu.CompilerParams(dimension_semantics=semantics, vmem_limit_bytes=_V7X_VMEM_LIMIT_BYTES)


def _rms_norm(x, g):
    return x * lax.rsqrt(jnp.mean(x * x, axis=-1, keepdims=True) + _RMS_EPS) * g


def _deinterleave_rows(val, slab_scr):
    rows, d = val.shape
    n_slabs = d // _V7X_LANES
    for k in range(n_slabs):
        slab_scr[k] = val[:, k * _V7X_LANES:(k + 1) * _V7X_LANES]
    return jnp.concatenate(
        [jnp.concatenate([slab_scr[k, pl.ds(n2, rows // _NP, stride=_NP), :] for k in range(n_slabs)], axis=1)
         for n2 in range(_NP)], axis=0)


def _interleave_rows(val, slab_scr):
    rows, d = val.shape
    per = rows // _NP
    n_slabs = d // _V7X_LANES
    for n2 in range(_NP):
        for k in range(n_slabs):
            slab_scr[k, pl.ds(n2, per, stride=_NP), :] = val[n2 * per:(n2 + 1) * per, k * _V7X_LANES:(k + 1) * _V7X_LANES]
    return jnp.concatenate([slab_scr[k] for k in range(n_slabs)], axis=1)


def _ffn_kernel(*refs, chunk, mix_in, norm_out, relayout):
    refs = list(refs)
    slab_scr = refs.pop() if relayout else None
    x_ref = refs.pop(0)
    blocked_in = relayout == "interleave"
    load = (lambda r: r[0].reshape(-1, r.shape[-1])) if blocked_in else (lambda r: r[...])
    x = load(x_ref)
    if mix_in:
        ya_ref, yb_ref, wa_ref, wb_ref, gmix_ref = refs[:5]
        refs = refs[5:]
        y = jnp.dot(load(ya_ref), wa_ref[...], preferred_element_type=jnp.float32)
        y = y + jnp.dot(load(yb_ref), wb_ref[...], preferred_element_type=jnp.float32)
        x = x + _rms_norm(y, gmix_ref[...])
    gpre_ref, wg_ref, wu_ref, wd_ref, gpost_ref = refs[:5]
    refs = refs[5:]
    h = _rms_norm(x, gpre_ref[...])
    acc = jnp.zeros(x.shape, jnp.float32)
    dff = wg_ref.shape[1]
    for c0 in range(0, dff, chunk):
        c1 = min(c0 + chunk, dff)
        g = jnp.dot(h, wg_ref[:, c0:c1], preferred_element_type=jnp.float32)
        u = jnp.dot(h, wu_ref[:, c0:c1], preferred_element_type=jnp.float32)
        a = g * jax.nn.sigmoid(g) * u
        acc = acc + jnp.dot(a, wd_ref[c0:c1, :], preferred_element_type=jnp.float32)
    out = x + _rms_norm(acc, gpost_ref[...])
    if relayout == "deinterleave":
        out = _deinterleave_rows(out, slab_scr)
    elif relayout == "interleave":
        out = _interleave_rows(out, slab_scr)
    blocked_out = relayout == "deinterleave"
    if norm_out:
        gnext_ref, o_ref, hn_ref = refs
        hn = _rms_norm(out, gnext_ref[...]).astype(jnp.bfloat16)
        if blocked_out:
            hn_ref[0] = hn.reshape(hn_ref.shape[1:])
        else:
            hn_ref[...] = hn
    else:
        (o_ref,) = refs
    if blocked_out:
        o_ref[0] = out.reshape(o_ref.shape[1:])
    else:
        o_ref[...] = out


def _ffn(x, layer, g_pre, w_gate, w_up, w_down, g_post, *, batch, tm, chunk, mix=None, g_next=None, relayout=None):
    m, d = x.shape
    dff = w_gate.shape[2]
    seq = m // batch
    steps_per_seq = seq // tm
    assert dff % _V7X_MXU_DIM == 0 and chunk % _V7X_MXU_DIM == 0 and seq % tm == 0 and tm % (_NP * _V7X_BF16_SUBLANES) == 0
    row = lambda a: a[layer].reshape(1, d)
    flat_tile = lambda width: pl.BlockSpec((tm, width), lambda i: (i, 0))
    blocked_tile = lambda width: pl.BlockSpec((1, _NP, tm // _NP, width),
                                              lambda i: (i // steps_per_seq, 0, i % steps_per_seq, 0))
    blocked = lambda a: a.reshape(batch, _NP, seq // _NP, a.shape[-1])
    if relayout == "interleave":
        in_tile, put = blocked_tile, blocked
    else:
        in_tile, put = flat_tile, lambda a: a
    vec = pl.BlockSpec((1, d), lambda i: (0, 0))
    resident = lambda r, c, blk=0: pl.BlockSpec((None, r, c), lambda i: (layer, blk, 0), pipeline_mode=pl.Buffered(1))
    args, specs = [put(x)], [in_tile(d)]
    if mix is not None:
        ya, yb, w_out, g_mix = mix
        da = ya.shape[1]
        assert yb.shape[1] == da and w_out.shape[1:] == (2 * da, d)
        args += [put(ya), put(yb), w_out, w_out, row(g_mix)]
        specs += [in_tile(da), in_tile(da), resident(da, d, 0), resident(da, d, 1), vec]
    args += [row(g_pre), w_gate, w_up, w_down, _HALF_STEP * row(g_post)]
    specs += [vec, resident(d, dff), resident(d, dff), resident(dff, d), vec]
    if relayout == "deinterleave":
        out_tile, out_dims = blocked_tile(d), (batch, _NP, seq // _NP, d)
    else:
        out_tile, out_dims = flat_tile(d), (m, d)
    out_shape = [jax.ShapeDtypeStruct(out_dims, jnp.float32)]
    out_specs = [out_tile]
    if g_next is not None:
        args.append(row(g_next))
        specs.append(vec)
        out_shape.append(jax.ShapeDtypeStruct(out_dims, jnp.bfloat16))
        out_specs.append(out_tile)
    outs = pl.pallas_call(
        functools.partial(_ffn_kernel, chunk=chunk, mix_in=mix is not None, norm_out=g_next is not None,
                          relayout=relayout),
        grid=(m // tm,),
        in_specs=specs,
        out_specs=out_specs,
        out_shape=out_shape,
        scratch_shapes=[pltpu.VMEM((d // _V7X_LANES, tm, _V7X_LANES), jnp.float32)] if relayout else [],
        compiler_params=_cparams(("arbitrary",)),
        name="ffn",
    )(*args)
    outs = [o.reshape(m, d) for o in outs]
    return outs if g_next is not None else outs[0]


def _gmlp_kernel(h_ref, w_ref, lng_ref, lnb_ref, ws_ref, bs_ref, o_ref, *, jb, da):
    rows = _CHUNK // _NP
    d = h_ref.shape[-1]
    h = h_ref[...].reshape(_NP * jb * rows, d)
    gelu = lambda p: 0.5 * p * (1.0 + lax.erf(p * _SQRT_HALF))
    u = gelu(jnp.dot(h, w_ref[:, :da], preferred_element_type=jnp.float32))
    v = gelu(jnp.dot(h, w_ref[:, da:], preferred_element_type=jnp.float32))
    mu = jnp.mean(v, axis=-1, keepdims=True)
    vc = v - mu
    var = jnp.mean(vc * vc, axis=-1, keepdims=True)
    vn = (vc * lax.rsqrt(var + _LN_EPS) * lng_ref[...] + lnb_ref[...]).astype(jnp.bfloat16)
    lane = lax.broadcasted_iota(jnp.int32, (_CHUNK, _V7X_LANES), 1)
    first_head = lane < _HEAD_DIM_A
    n_pairs = da // _V7X_LANES
    for j in range(jb):
        starts = [(n2 * jb + j) * rows for n2 in range(_NP)]
        vchunk = jnp.concatenate([vn[s:s + rows] for s in starts], axis=0)
        mixed = []
        for q in range(n_pairs):
            r = jnp.dot(ws_ref[q], vchunk[:, q * _V7X_LANES:(q + 1) * _V7X_LANES],
                        preferred_element_type=jnp.float32)
            mixed.append(jnp.where(first_head, r[:_CHUNK], r[_CHUNK:]))
        mixed = jnp.concatenate(mixed, axis=1) + bs_ref[...]
        for n2 in range(_NP):
            s = starts[n2]
            o_ref[n2, j * rows:(j + 1) * rows] = (u[s:s + rows] * mixed[n2 * rows:(n2 + 1) * rows]).astype(o_ref.dtype)


def _gmlp(h, w_in, layer, ln_g, ln_b, w_s, b_s, *, batch, seq, jb):
    m, d = h.shape
    da = ln_g.shape[0]
    rows = _CHUNK // _NP
    n_chunks = seq // _CHUNK
    assert rows == _V7X_BF16_SUBLANES
    n1 = seq // _NP
    h3 = h.reshape(batch * _NP, n1, d)
    tau = np.arange(_CHUNK)
    perm = np.zeros((_CHUNK, _CHUNK), np.float32)
    perm[(tau % _NP) * rows + tau // _NP, tau] = 1.0
    exact = lax.Precision.HIGHEST
    ws_p = jnp.einsum("pm,hmc,qc->hpq", perm, w_s, perm, precision=exact)
    ws_pairs = ws_p.reshape(_HEADS_A // 2, 2 * _CHUNK, _CHUNK).astype(jnp.bfloat16)
    bs_p = jnp.einsum("hm,pm->hp", b_s, perm, precision=exact)
    bs_full = jnp.repeat(bs_p.T, _HEAD_DIM_A, axis=1)
    out = pl.pallas_call(
        functools.partial(_gmlp_kernel, jb=jb, da=da),
        grid=(batch, n_chunks // jb),
        in_specs=[
            pl.BlockSpec((_NP, jb * rows, d), lambda b, j: (b, j, 0)),
            pl.BlockSpec((None, d, 2 * da), lambda b, j: (layer, 0, 0)),
            pl.BlockSpec((1, da), lambda b, j: (0, 0)),
            pl.BlockSpec((1, da), lambda b, j: (0, 0)),
            pl.BlockSpec((_HEADS_A // 2, 2 * _CHUNK, _CHUNK), lambda b, j: (0, 0, 0)),
            pl.BlockSpec((_CHUNK, da), lambda b, j: (0, 0)),
        ],
        out_specs=pl.BlockSpec((_NP, jb * rows, da), lambda b, j: (b, j, 0)),
        out_shape=jax.ShapeDtypeStruct((batch * _NP, n1, da), jnp.bfloat16),
        compiler_params=_cparams(("arbitrary", "arbitrary")),
        name="gmlp",
    )(h3, w_in, ln_g.reshape(1, da), ln_b.reshape(1, da), ws_pairs, bs_full)
    return out.reshape(m, da)


def _hyproj_kernel(h_ref, *refs, n1):
    n_proj = _ORDER + 1
    w_refs, (cw_ref, cb_ref, o_ref) = refs[:n_proj], refs[n_proj:]
    c = o_ref.shape[-1]
    h = h_ref[0]
    sub = cw_ref.shape[1]
    tiled = lambda a: a.reshape(n1 // sub, sub, c)
    for j in range(n_proj):
        p = jnp.dot(h, w_refs[j][...], preferred_element_type=jnp.float32)
        blocks = [p[k * n1:(k + 1) * n1] for k in range(_NP)]
        row = lax.broadcasted_iota(jnp.int32, blocks[0].shape, 0)
        before_first = jnp.where(row == 0, 0.0, pltpu.roll(blocks[_NP - 1], 1, axis=0))
        after_last = jnp.where(row == n1 - 1, 0.0, pltpu.roll(blocks[0], n1 - 1, axis=0))
        cols = slice(j * c, (j + 1) * c)
        w0, w1, w2 = cw_ref[0, :, cols], cw_ref[1, :, cols], cw_ref[2, :, cols]
        bias = cb_ref[:, cols]
        for k in range(_NP):
            prev = blocks[k - 1] if k > 0 else before_first
            nxt = blocks[k + 1] if k < _NP - 1 else after_last
            y = ((bias + tiled(prev) * w0) + tiled(blocks[k]) * w1) + tiled(nxt) * w2
            o_ref[j, k * n1:(k + 1) * n1] = y.reshape(n1, c).astype(o_ref.dtype)


def _hyproj(h, w_in, layer, conv_w, conv_b, *, batch, seq):
    m, d = h.shape
    n_proj = _ORDER + 1
    c = conv_w.shape[1] // n_proj
    first = (w_in.shape[2] - n_proj * c) // c
    w_spec = lambda j: pl.BlockSpec((None, d, c), lambda b: (layer, 0, first + j), pipeline_mode=pl.Buffered(1))
    sub = _V7X_F32_SUBLANES
    taps = jnp.broadcast_to(conv_w[:, None, :], (_SHORT_K, sub, n_proj * c))
    bias = jnp.broadcast_to(conv_b[None, :], (sub, n_proj * c))
    return pl.pallas_call(
        functools.partial(_hyproj_kernel, n1=seq // _NP),
        grid=(batch,),
        in_specs=[pl.BlockSpec((1, seq, d), lambda b: (b, 0, 0))] + [w_spec(j) for j in range(n_proj)] + [
            pl.BlockSpec((_SHORT_K, sub, n_proj * c), lambda b: (0, 0, 0)),
            pl.BlockSpec((sub, n_proj * c), lambda b: (0, 0)),
        ],
        out_specs=pl.BlockSpec((n_proj, seq, c), lambda b: (0, b, 0)),
        out_shape=jax.ShapeDtypeStruct((n_proj, m, c), jnp.bfloat16),
        compiler_params=_cparams(("arbitrary",)),
        name="hyproj",
    )(h.reshape(batch, seq, d), *([w_in] * n_proj), taps, bias)


@functools.lru_cache(maxsize=None)
def _dft_tables(seq):
    n1 = seq // _NP
    n_fft = 2 * seq
    k1 = np.arange(_HALF)[None, :, None]
    t = _NP * np.arange(n1)[None, None, :] + np.arange(_NP)[:, None, None]
    theta = ((k1 * t) % n_fft) * (2.0 * math.pi / n_fft)
    valid = k1 <= n1
    cos = np.where(valid, np.cos(theta), 0.0)
    sin = np.where(valid, np.sin(theta), 0.0)
    fwd = np.concatenate([cos, -sin], axis=1)
    weight = np.where((k1 == 0) | (k1 == n1), 1.0, 2.0) / n_fft
    inv = np.concatenate([weight * cos, -weight * sin], axis=1).transpose(0, 2, 1)
    return fwd.astype(np.float32), np.ascontiguousarray(inv).astype(np.float32)


def _fft4(cr, ci):
    d0r, d0i = cr[0] + cr[2], ci[0] + ci[2]
    d1r, d1i = cr[1] + cr[3], ci[1] + ci[3]
    d2r, d2i = cr[0] - cr[2], ci[0] - ci[2]
    er, ei = cr[1] - cr[3], ci[1] - ci[3]
    d3r, d3i = ei, -er
    return [(d0r + d1r, d0i + d1i), (d2r + d3r, d2i + d3i), (d0r - d1r, d0i - d1i), (d2r - d3r, d2i - d3i)]


def _fft8(re, im):
    br, bi = [None] * 8, [None] * 8
    for j in range(4):
        br[j], bi[j] = re[j] + re[j + 4], im[j] + im[j + 4]
        dr, di = re[j] - re[j + 4], im[j] - im[j + 4]
        if j == 0:
            br[4], bi[4] = dr, di
        elif j == 1:
            br[5], bi[5] = (dr + di) * _SQRT_HALF, (di - dr) * _SQRT_HALF
        elif j == 2:
            br[6], bi[6] = di, -dr
        else:
            br[7], bi[7] = (di - dr) * _SQRT_HALF, -(dr + di) * _SQRT_HALF
    even = _fft4(br[:4], bi[:4])
    odd = _fft4(br[4:], bi[4:])
    out = [None] * 8
    for k in range(4):
        out[2 * k], out[2 * k + 1] = even[k], odd[k]
    return [o[0] for o in out], [o[1] for o in out]


def _ifft8(re, im):
    o_im, o_re = _fft8(im, re)
    return o_re, o_im


def _filter_kernel(w1_ref, b1_ref, w2_ref, b2_ref, w3_ref, b3_ref, fr_ref, wof_ref, wob_ref, dl_ref,
                   fwd_ref, o_ref, h_scr, af_scr, ab_scr, *, seq):
    n1 = seq // _NP
    fw = _FILTER_WIDTH
    hp = lax.Precision.HIGHEST

    @pl.when(pl.program_id(1) == 0)
    def _():
        nb = _FILTER_BANDS
        r = lax.broadcasted_iota(jnp.int32, (nb, seq), 1)
        pos = (_NP * (r % n1) + r // n1).astype(jnp.float32)
        band_idx = lax.broadcasted_iota(jnp.int32, (nb, seq), 0).astype(jnp.float32)
        band = 1e-4 + band_idx * ((nb - 1 - 1e-4) / (nb - 1))
        ang = (2.0 * math.pi / seq) * pos * band
        first_row = lax.broadcasted_iota(jnp.int32, (fw - 2 * nb, seq), 0) == 0
        tail = jnp.where(first_row, jnp.concatenate([pos] * ((fw - 2 * nb) // nb), axis=0) * (1.0 / (seq - 1)), 0.0)
        z = jnp.concatenate([jnp.cos(ang), -jnp.sin(ang), tail], axis=0)
        freq = fr_ref[0]
        h = jnp.sin(freq * (jnp.dot(w1_ref[0], z, precision=hp, preferred_element_type=jnp.float32) + b1_ref[0]))
        h = jnp.sin(freq * (jnp.dot(w2_ref[0], h, precision=hp, preferred_element_type=jnp.float32) + b2_ref[0]))
        h = jnp.sin(freq * (jnp.dot(w3_ref[0], h, precision=hp, preferred_element_type=jnp.float32) + b3_ref[0]))
        h_scr[...] = h.T

    r = lax.broadcasted_iota(jnp.int32, (seq, 1), 0)
    pos = (_NP * (r % n1) + r // n1).astype(jnp.float32)
    h = h_scr[...].astype(jnp.bfloat16)
    decay = jnp.exp(-(pos * (1.0 / (seq - 1))) * dl_ref[...])
    hf = jnp.dot(h, wof_ref[0].astype(jnp.bfloat16), preferred_element_type=jnp.float32) * decay
    hb = jnp.dot(h, wob_ref[0].astype(jnp.bfloat16), preferred_element_type=jnp.float32) * decay
    hb = jnp.where(pos == 0.0, 0.0, hb)
    hf = hf.astype(jnp.bfloat16)
    hb = hb.astype(jnp.bfloat16)
    for n2 in range(_NP):
        af_scr[n2] = jnp.dot(fwd_ref[n2], hf[n2 * n1:(n2 + 1) * n1], preferred_element_type=jnp.float32)
        ab_scr[n2] = jnp.dot(fwd_ref[n2], hb[n2 * n1:(n2 + 1) * n1], preferred_element_type=jnp.float32)

    rc = _V7X_BF16_SUBLANES

    def body(i, carry):
        r0 = pl.multiple_of(i * rc, rc)
        fr, fi = _fft8([af_scr[n2, pl.ds(r0, rc)] for n2 in range(_NP)],
                       [af_scr[n2, pl.ds(_HALF + r0, rc)] for n2 in range(_NP)])
        gr, gi = _fft8([ab_scr[n2, pl.ds(r0, rc)] for n2 in range(_NP)],
                       [ab_scr[n2, pl.ds(_HALF + r0, rc)] for n2 in range(_NP)])
        for k2 in range(_NP):
            o_ref[0, 0, k2, 0, pl.ds(r0, rc)] = (fr[k2] + gr[k2]).astype(o_ref.dtype)
            o_ref[0, 0, k2, 1, pl.ds(r0, rc)] = (fi[k2] - gi[k2]).astype(o_ref.dtype)
        return carry

    lax.fori_loop(0, _HALF // rc, body, 0)


def _filter_spectra(w1, b1, w2, b2, w3, b3, freq, w_out, fwd_tab, *, seq, ct):
    n_layers = w1.shape[0]
    c = w_out.shape[2] // (2 * _ORDER)
    n_ct = c // ct
    fw = _FILTER_WIDTH
    w1r = jnp.concatenate([w1[:, 1:], w1[:, :1]], axis=1)
    w1t = jnp.pad(w1r, ((0, 0), (0, fw - w1.shape[1]), (0, 0))).transpose(0, 2, 1)
    col = lambda a: a.reshape(n_layers, fw, 1)
    deltas = jnp.abs(jnp.linspace(_MIN_DECAY, _MAX_DECAY, c, dtype=jnp.float32)).reshape(1, c)
    lmap3 = lambda l, s: (l, 0, 0)
    wo_spec = lambda direction: pl.BlockSpec(
        (1, fw, ct), lambda l, s: (l, 0, (2 * (s // n_ct) + direction) * n_ct + s % n_ct))
    return pl.pallas_call(
        functools.partial(_filter_kernel, seq=seq),
        grid=(n_layers, _ORDER * n_ct),
        in_specs=[
            pl.BlockSpec((1, fw, fw), lmap3), pl.BlockSpec((1, fw, 1), lmap3),
            pl.BlockSpec((1, fw, fw), lmap3), pl.BlockSpec((1, fw, 1), lmap3),
            pl.BlockSpec((1, fw, fw), lmap3), pl.BlockSpec((1, fw, 1), lmap3),
            pl.BlockSpec((1, fw, 1), lmap3),
            wo_spec(0), wo_spec(1),
            pl.BlockSpec((1, ct), lambda l, s: (0, s % n_ct)),
            pl.BlockSpec((_NP, 2 * _HALF, seq // _NP), lambda l, s: (0, 0, 0)),
        ],
        out_specs=pl.BlockSpec((1, 1, _NP, 2, _HALF, ct), lambda l, s: (l, s // n_ct, 0, 0, 0, s % n_ct)),
        out_shape=jax.ShapeDtypeStruct((n_layers, _ORDER, _NP, 2, _HALF, c), _STAGE_TWO_DTYPE),
        scratch_shapes=[pltpu.VMEM((seq, fw), jnp.float32),
                        pltpu.VMEM((_NP, 2 * _HALF, ct), jnp.float32),
                        pltpu.VMEM((_NP, 2 * _HALF, ct), jnp.float32)],
        compiler_params=_cparams(("arbitrary", "arbitrary")),
        name="filter_spectra",
    )(w1t, col(b1), w2.transpose(0, 2, 1), col(b2), w3.transpose(0, 2, 1), col(b3), col(freq),
      w_out, w_out, deltas, fwd_tab)


def _hyena_kernel(v_ref, x1_ref, x2_ref, kf_ref, skip_ref, fwd_ref, inv_ref, o_ref, a_scr, c_scr, z_scr, *, seq):
    n1 = seq // _NP
    rc = _V7X_BF16_SUBLANES
    ct = o_ref.shape[-1]

    def conv_order(order, z_in_ref, gate_ref, z_out_ref):
        for n2 in range(_NP):
            zb = z_in_ref[0, n2 * n1:(n2 + 1) * n1].astype(jnp.bfloat16)
            a_scr[n2] = jnp.dot(fwd_ref[n2], zb, preferred_element_type=jnp.float32).astype(a_scr.dtype)

        def body(i, carry):
            r0 = pl.multiple_of(i * rc, rc)
            for l0 in range(0, ct, _V7X_LANES):
                ls = slice(l0, l0 + _V7X_LANES)
                xr, xi = _fft8([a_scr[n2, pl.ds(r0, rc), ls] for n2 in range(_NP)],
                               [a_scr[n2, pl.ds(_HALF + r0, rc), ls] for n2 in range(_NP)])
                yr, yi = [], []
                for k2 in range(_NP):
                    kr = kf_ref[order, k2, 0, pl.ds(r0, rc), ls]
                    ki = kf_ref[order, k2, 1, pl.ds(r0, rc), ls]
                    yr.append(xr[k2] * kr - xi[k2] * ki)
                    yi.append(xr[k2] * ki + xi[k2] * kr)
                cr, ci = _ifft8(yr, yi)
                for t2 in range(_NP):
                    c_scr[t2, pl.ds(r0, rc), ls] = cr[t2].astype(jnp.bfloat16)
                    c_scr[t2, pl.ds(_HALF + r0, rc), ls] = ci[t2].astype(jnp.bfloat16)
            return carry

        lax.fori_loop(0, _HALF // rc, body, 0)

        skip = skip_ref[order:order + 1]
        for t2 in range(_NP):
            rows = slice(t2 * n1, (t2 + 1) * n1)
            y = jnp.dot(inv_ref[t2], c_scr[t2], preferred_element_type=jnp.float32)
            z = z_in_ref[0, rows].astype(jnp.float32)
            gate = gate_ref[0, rows].astype(jnp.float32)
            z_out_ref[0, rows] = (gate * (y + z * skip)).astype(z_out_ref.dtype)

    conv_order(0, v_ref, x1_ref, z_scr)
    conv_order(1, z_scr, x2_ref, o_ref)


def _hyena(proj, kf_all, layer, skip, fwd_tab, inv_tab, *, batch, seq, ct):
    n_proj, m, c = proj.shape
    n1 = seq // _NP
    proj4 = proj.reshape(n_proj, batch, seq, c)
    pspec = lambda j: pl.BlockSpec((None, 1, seq, ct), lambda t, b, j=j: (j, b, 0, t))
    resident = dict(pipeline_mode=pl.Buffered(1))
    out = pl.pallas_call(
        functools.partial(_hyena_kernel, seq=seq),
        grid=(c // ct, batch),
        in_specs=[
            pspec(0), pspec(1), pspec(2),
            pl.BlockSpec((None, _ORDER, _NP, 2, _HALF, ct), lambda t, b: (layer, 0, 0, 0, 0, t), **resident),
            pl.BlockSpec((_ORDER, ct), lambda t, b: (0, t)),
            pl.BlockSpec((_NP, 2 * _HALF, n1), lambda t, b: (0, 0, 0), **resident),
            pl.BlockSpec((_NP, n1, 2 * _HALF), lambda t, b: (0, 0, 0), **resident),
        ],
        out_specs=pl.BlockSpec((1, seq, ct), lambda t, b: (b, 0, t)),
        out_shape=jax.ShapeDtypeStruct((batch, seq, c), jnp.bfloat16),
        scratch_shapes=[pltpu.VMEM((_NP, 2 * _HALF, ct), _STAGE_TWO_DTYPE),
                        pltpu.VMEM((_NP, 2 * _HALF, ct), jnp.bfloat16),
                        pltpu.VMEM((1, seq, ct), jnp.float32)],
        compiler_params=_cparams(("arbitrary", "arbitrary")),
        name="hyena",
    )(proj4, proj4, proj4, kf_all, skip, fwd_tab, inv_tab)
    return out.reshape(m, c)


def kernel(x, ffn1_pre_g, ffn1_w_gate, ffn1_w_up, ffn1_w_down, ffn1_post_g, mix_pre_g, mix_w_in, gmlp_ln_g, gmlp_ln_b, gmlp_w_s, gmlp_b_s, hy_conv_w, hy_conv_b, hy_filt_w1, hy_filt_b1, hy_filt_w2, hy_filt_b2, hy_filt_w3, hy_filt_b3, hy_filt_freq, hy_filt_w_out, hy_skip, mix_w_out, mix_post_g, ffn2_pre_g, ffn2_w_gate, ffn2_w_up, ffn2_w_down, ffn2_post_g):
    batch, seq, d = x.shape
    depth = ffn1_pre_g.shape[0]
    da = gmlp_ln_g.shape[1]
    assert seq % (_NP * _CHUNK) == 0 and seq // _NP + 1 <= _HALF
    m = batch * seq
    ffn_tiles = dict(batch=batch, tm=512, chunk=_V7X_MXU_DIM)
    ct = _V7X_MXU_DIM
    xp = x.reshape(m, d)

    fwd_np, inv_np = _dft_tables(seq)
    fwd_bf16 = jnp.asarray(fwd_np).astype(jnp.bfloat16)
    inv_bf16 = jnp.asarray(inv_np).astype(jnp.bfloat16)
    kf_all = _filter_spectra(hy_filt_w1, hy_filt_b1, hy_filt_w2, hy_filt_b2, hy_filt_w3, hy_filt_b3,
                             hy_filt_freq, hy_filt_w_out, fwd_bf16, seq=seq, ct=ct)

    ffn1_w = (ffn1_w_gate, ffn1_w_up, ffn1_w_down)
    ffn2_w = (ffn2_w_gate, ffn2_w_up, ffn2_w_down)
    w_in = mix_w_in
    w_out = mix_w_out.astype(jnp.bfloat16)

    for l in range(depth):
        xp, h = _ffn(xp, l, ffn1_pre_g, *ffn1_w, ffn1_post_g, g_next=mix_pre_g,
                     relayout="deinterleave" if l == 0 else None, **ffn_tiles)
        ya = _gmlp(h, w_in, l, gmlp_ln_g[l], gmlp_ln_b[l], gmlp_w_s[l], gmlp_b_s[l], batch=batch, seq=seq, jb=8)
        proj = _hyproj(h, w_in, l, hy_conv_w[l], hy_conv_b[l], batch=batch, seq=seq)
        yb = _hyena(proj, kf_all, l, hy_skip[l], fwd_bf16, inv_bf16, batch=batch, seq=seq, ct=proj.shape[-1])
        xp = _ffn(xp, l, ffn2_pre_g, *ffn2_w, ffn2_post_g, mix=(ya, yb, w_out, mix_post_g),
                  relayout="interleave" if l == depth - 1 else None, **ffn_tiles)

    return xp.reshape(batch, seq, d)
```

```python
import functools
import math

import jax
import jax.numpy as jnp
import numpy as np
from jax import lax
from jax.experimental import pallas as pl
from jax.experimental.pallas import tpu as pltpu

_CHUNK = 128
_HEADS_A = 8
_HEAD_DIM_A = 64
_ORDER = 2
_SHORT_K = 3
_FILTER_BANDS = 16
_FILTER_WIDTH = 64
_DECAY_TARGET = 1e-2
_MAX_DECAY = math.log(_DECAY_TARGET) / 0.3
_MIN_DECAY = math.log(_DECAY_TARGET) / 1.5
_HALF_STEP = 0.5
_RMS_EPS = 1e-6
_LN_EPS = 1e-5

_V7X_LANES = 128
_V7X_F32_SUBLANES = 8
_V7X_BF16_SUBLANES = 16
_V7X_MXU_DIM = 256
_V7X_VMEM_LIMIT_BYTES = 62 * 1024 * 1024

_NP = 8
_HALF = 272
_SQRT_HALF = 0.7071067811865476
_STAGE_TWO_DTYPE = jnp.bfloat16


def _cparams(semantics):
    return pltpu.CompilerParams(dimension_semantics=semantics, vmem_limit_bytes=_V7X_VMEM_LIMIT_BYTES)


def _rms_norm(x, g):
    return x * lax.rsqrt(jnp.mean(x * x, axis=-1, keepdims=True) + _RMS_EPS) * g


def _deinterleave_rows(val, slab_scr):
    rows, d = val.shape
    n_slabs = d // _V7X_LANES
    for k in range(n_slabs):
        slab_scr[k] = val[:, k * _V7X_LANES:(k + 1) * _V7X_LANES]
    return jnp.concatenate(
        [jnp.concatenate([slab_scr[k, pl.ds(n2, rows // _NP, stride=_NP), :] for k in range(n_slabs)], axis=1)
         for n2 in range(_NP)], axis=0)


def _interleave_rows(val, slab_scr):
    rows, d = val.shape
    per = rows // _NP
    n_slabs = d // _V7X_LANES
    for n2 in range(_NP):
        for k in range(n_slabs):
            slab_scr[k, pl.ds(n2, per, stride=_NP), :] = val[n2 * per:(n2 + 1) * per, k * _V7X_LANES:(k + 1) * _V7X_LANES]
    return jnp.concatenate([slab_scr[k] for k in range(n_slabs)], axis=1)


def _ffn_kernel(*refs, layer, chunk, mix_in, norm_out, relayout):
    refs = list(refs)
    x_ref = refs.pop(0)
    if mix_in:
        ya_ref, yb_ref, wa_ref, wb_ref, gmix_ref = refs[:5]
        refs = refs[5:]
    gpre_ref, wg_hbm, wu_hbm, wd_hbm, gpost_ref = refs[:5]
    refs = refs[5:]
    gnext_ref = refs.pop(0) if norm_out else None
    o_ref = refs.pop(0)
    hn_ref = refs.pop(0) if norm_out else None
    wg_scr, wu_scr, wd_scr, w_sem = refs[:4]
    slab_scr = refs[4] if relayout else None
    blocked_in = relayout == "interleave"
    blocked_out = relayout == "deinterleave"
    load = (lambda r: r[0].reshape(-1, r.shape[-1])) if blocked_in else (lambda r: r[...])
    dff = wg_scr.shape[1]
    bounds = [(c0, min(c0 + chunk, dff)) for c0 in range(0, dff, chunk)]

    def weight_copies(c):
        cols = pl.ds(bounds[c][0], bounds[c][1] - bounds[c][0])
        return (pltpu.make_async_copy(wg_hbm.at[layer, :, cols], wg_scr.at[:, cols], w_sem.at[c, 0]),
                pltpu.make_async_copy(wu_hbm.at[layer, :, cols], wu_scr.at[:, cols], w_sem.at[c, 1]),
                pltpu.make_async_copy(wd_hbm.at[layer, cols, :], wd_scr.at[cols, :], w_sem.at[c, 2]))

    def tile(first_step):
        if first_step:
            for c in range(len(bounds)):
                for copy in weight_copies(c):
                    copy.start()
        x = load(x_ref)
        if mix_in:
            y = jnp.dot(load(ya_ref), wa_ref[...], preferred_element_type=jnp.float32)
            y = y + jnp.dot(load(yb_ref), wb_ref[...], preferred_element_type=jnp.float32)
            x = x + _rms_norm(y, gmix_ref[...])
        h = _rms_norm(x, gpre_ref[...])
        acc = jnp.zeros(x.shape, jnp.float32)
        for c, (c0, c1) in enumerate(bounds):
            if first_step:
                for copy in weight_copies(c):
                    copy.wait()
            g = jnp.dot(h, wg_scr[:, c0:c1], preferred_element_type=jnp.float32)
            u = jnp.dot(h, wu_scr[:, c0:c1], preferred_element_type=jnp.float32)
            a = g * jax.nn.sigmoid(g) * u
            acc = acc + jnp.dot(a, wd_scr[c0:c1, :], preferred_element_type=jnp.float32)
        out = x + _rms_norm(acc, gpost_ref[...])
        if relayout == "deinterleave":
            out = _deinterleave_rows(out, slab_scr)
        elif relayout == "interleave":
            out = _interleave_rows(out, slab_scr)
        if norm_out:
            hn = _rms_norm(out, gnext_ref[...]).astype(jnp.bfloat16)
            if blocked_out:
                hn_ref[0] = hn.reshape(hn_ref.shape[1:])
            else:
                hn_ref[...] = hn
        if blocked_out:
            o_ref[0] = out.reshape(o_ref.shape[1:])
        else:
            o_ref[...] = out

    first = pl.program_id(0) == 0
    pl.when(first)(functools.partial(tile, True))
    pl.when(jnp.logical_not(first))(functools.partial(tile, False))


def _ffn(x, layer, g_pre, w_gate, w_up, w_down, g_post, *, batch, tm, chunk, mix=None, g_next=None, relayout=None):
    m, d = x.shape
    dff = w_gate.shape[2]
    seq = m // batch
    steps_per_seq = seq // tm
    assert dff % _V7X_MXU_DIM == 0 and chunk % _V7X_MXU_DIM == 0 and seq % tm == 0 and tm % (_NP * _V7X_BF16_SUBLANES) == 0
    row = lambda a: a[layer].reshape(1, d)
    flat_tile = lambda width: pl.BlockSpec((tm, width), lambda i: (i, 0))
    blocked_tile = lambda width: pl.BlockSpec((1, _NP, tm // _NP, width),
                                              lambda i: (i // steps_per_seq, 0, i % steps_per_seq, 0))
    blocked = lambda a: a.reshape(batch, _NP, seq // _NP, a.shape[-1])
    if relayout == "interleave":
        in_tile, put = blocked_tile, blocked
    else:
        in_tile, put = flat_tile, lambda a: a
    vec = pl.BlockSpec((1, d), lambda i: (0, 0))
    resident = lambda r, c, blk=0: pl.BlockSpec((None, r, c), lambda i: (layer, blk, 0), pipeline_mode=pl.Buffered(1))
    args, specs = [put(x)], [in_tile(d)]
    if mix is not None:
        ya, yb, w_out, g_mix = mix
        da = ya.shape[1]
        assert yb.shape[1] == da and w_out.shape[1:] == (2 * da, d)
        args += [put(ya), put(yb), w_out, w_out, row(g_mix)]
        specs += [in_tile(da), in_tile(da), resident(da, d, 0), resident(da, d, 1), vec]
    args += [row(g_pre), w_gate, w_up, w_down, _HALF_STEP * row(g_post)]
    in_hbm = pl.BlockSpec(memory_space=pl.ANY)
    specs += [vec, in_hbm, in_hbm, in_hbm, vec]
    if relayout == "deinterleave":
        out_tile, out_dims = blocked_tile(d), (batch, _NP, seq // _NP, d)
    else:
        out_tile, out_dims = flat_tile(d), (m, d)
    out_shape = [jax.ShapeDtypeStruct(out_dims, jnp.float32)]
    out_specs = [out_tile]
    if g_next is not None:
        args.append(row(g_next))
        specs.append(vec)
        out_shape.append(jax.ShapeDtypeStruct(out_dims, jnp.bfloat16))
        out_specs.append(out_tile)
    n_chunks = -(-dff // chunk)
    scratch = [pltpu.VMEM((d, dff), w_gate.dtype), pltpu.VMEM((d, dff), w_up.dtype), pltpu.VMEM((dff, d), w_down.dtype),
               pltpu.SemaphoreType.DMA((n_chunks, 3))]
    if relayout:
        scratch.append(pltpu.VMEM((d // _V7X_LANES, tm, _V7X_LANES), jnp.float32))
    outs = pl.pallas_call(
        functools.partial(_ffn_kernel, layer=layer, chunk=chunk, mix_in=mix is not None,
                          norm_out=g_next is not None, relayout=relayout),
        grid=(m // tm,),
        in_specs=specs,
        out_specs=out_specs,
        out_shape=out_shape,
        scratch_shapes=scratch,
        compiler_params=_cparams(("arbitrary",)),
        name="ffn",
    )(*args)
    outs = [o.reshape(m, d) for o in outs]
    return outs if g_next is not None else outs[0]


def _gmlp_kernel(h_ref, w_ref, lng_ref, lnb_ref, ws_ref, bs_ref, o_ref, *, jb, da):
    rows = _CHUNK // _NP
    d = h_ref.shape[-1]
    h = h_ref[...].reshape(_NP * jb * rows, d)
    gelu = lambda p: 0.5 * p * (1.0 + lax.erf(p * _SQRT_HALF))
    u = gelu(jnp.dot(h, w_ref[:, :da], preferred_element_type=jnp.float32))
    v = gelu(jnp.dot(h, w_ref[:, da:], preferred_element_type=jnp.float32))
    mu = jnp.mean(v, axis=-1, keepdims=True)
    vc = v - mu
    var = jnp.mean(vc * vc, axis=-1, keepdims=True)
    vn = (vc * lax.rsqrt(var + _LN_EPS) * lng_ref[...] + lnb_ref[...]).astype(jnp.bfloat16)
    lane = lax.broadcasted_iota(jnp.int32, (_CHUNK, _V7X_LANES), 1)
    first_head = lane < _HEAD_DIM_A
    n_pairs = da // _V7X_LANES
    for j in range(jb):
        starts = [(n2 * jb + j) * rows for n2 in range(_NP)]
        vchunk = jnp.concatenate([vn[s:s + rows] for s in starts], axis=0)
        mixed = []
        for q in range(n_pairs):
            r = jnp.dot(ws_ref[q], vchunk[:, q * _V7X_LANES:(q + 1) * _V7X_LANES],
                        preferred_element_type=jnp.float32)
            mixed.append(jnp.where(first_head, r[:_CHUNK], r[_CHUNK:]))
        mixed = jnp.concatenate(mixed, axis=1) + bs_ref[...]
        for n2 in range(_NP):
            s = starts[n2]
            o_ref[n2, j * rows:(j + 1) * rows] = (u[s:s + rows] * mixed[n2 * rows:(n2 + 1) * rows]).astype(o_ref.dtype)


def _gmlp(h, w_in, layer, ln_g, ln_b, w_s, b_s, *, batch, seq, jb):
    m, d = h.shape
    da = ln_g.shape[0]
    rows = _CHUNK // _NP
    n_chunks = seq // _CHUNK
    assert rows == _V7X_BF16_SUBLANES
    n1 = seq // _NP
    h3 = h.reshape(batch * _NP, n1, d)
    tau = np.arange(_CHUNK)
    perm = np.zeros((_CHUNK, _CHUNK), np.float32)
    perm[(tau % _NP) * rows + tau // _NP, tau] = 1.0
    exact = lax.Precision.HIGHEST
    ws_p = jnp.einsum("pm,hmc,qc->hpq", perm, w_s, perm, precision=exact)
    ws_pairs = ws_p.reshape(_HEADS_A // 2, 2 * _CHUNK, _CHUNK).astype(jnp.bfloat16)
    bs_p = jnp.einsum("hm,pm->hp", b_s, perm, precision=exact)
    bs_full = jnp.repeat(bs_p.T, _HEAD_DIM_A, axis=1)
    out = pl.pallas_call(
        functools.partial(_gmlp_kernel, jb=jb, da=da),
        grid=(batch, n_chunks // jb),
        in_specs=[
            pl.BlockSpec((_NP, jb * rows, d), lambda b, j: (b, j, 0)),
            pl.BlockSpec((None, d, 2 * da), lambda b, j: (layer, 0, 0)),
            pl.BlockSpec((1, da), lambda b, j: (0, 0)),
            pl.BlockSpec((1, da), lambda b, j: (0, 0)),
            pl.BlockSpec((_HEADS_A // 2, 2 * _CHUNK, _CHUNK), lambda b, j: (0, 0, 0)),
            pl.BlockSpec((_CHUNK, da), lambda b, j: (0, 0)),
        ],
        out_specs=pl.BlockSpec((_NP, jb * rows, da), lambda b, j: (b, j, 0)),
        out_shape=jax.ShapeDtypeStruct((batch * _NP, n1, da), jnp.bfloat16),
        compiler_params=_cparams(("arbitrary", "arbitrary")),
        name="gmlp",
    )(h3, w_in, ln_g.reshape(1, da), ln_b.reshape(1, da), ws_pairs, bs_full)
    return out.reshape(m, da)


def _hyproj_kernel(h_ref, *refs, n1):
    n_proj = _ORDER + 1
    w_refs, (cw_ref, cb_ref, o_ref) = refs[:n_proj], refs[n_proj:]
    c = o_ref.shape[-1]
    h = h_ref[0]
    sub = cw_ref.shape[1]
    tiled = lambda a: a.reshape(n1 // sub, sub, c)
    for j in range(n_proj):
        p = jnp.dot(h, w_refs[j][...], preferred_element_type=jnp.float32)
        blocks = [p[k * n1:(k + 1) * n1] for k in range(_NP)]
        row = lax.broadcasted_iota(jnp.int32, blocks[0].shape, 0)
        before_first = jnp.where(row == 0, 0.0, pltpu.roll(blocks[_NP - 1], 1, axis=0))
        after_last = jnp.where(row == n1 - 1, 0.0, pltpu.roll(blocks[0], n1 - 1, axis=0))
        cols = slice(j * c, (j + 1) * c)
        w0, w1, w2 = cw_ref[0, :, cols], cw_ref[1, :, cols], cw_ref[2, :, cols]
        bias = cb_ref[:, cols]
        for k in range(_NP):
            prev = blocks[k - 1] if k > 0 else before_first
            nxt = blocks[k + 1] if k < _NP - 1 else after_last
            y = ((bias + tiled(prev) * w0) + tiled(blocks[k]) * w1) + tiled(nxt) * w2
            o_ref[j, k * n1:(k + 1) * n1] = y.reshape(n1, c).astype(o_ref.dtype)


def _hyproj(h, w_in, layer, conv_w, conv_b, *, batch, seq):
    m, d = h.shape
    n_proj = _ORDER + 1
    c = conv_w.shape[1] // n_proj
    first = (w_in.shape[2] - n_proj * c) // c
    w_spec = lambda j: pl.BlockSpec((None, d, c), lambda b: (layer, 0, first + j), pipeline_mode=pl.Buffered(1))
    sub = _V7X_F32_SUBLANES
    taps = jnp.broadcast_to(conv_w[:, None, :], (_SHORT_K, sub, n_proj * c))
    bias = jnp.broadcast_to(conv_b[None, :], (sub, n_proj * c))
    return pl.pallas_call(
        functools.partial(_hyproj_kernel, n1=seq // _NP),
        grid=(batch,),
        in_specs=[pl.BlockSpec((1, seq, d), lambda b: (b, 0, 0))] + [w_spec(j) for j in range(n_proj)] + [
            pl.BlockSpec((_SHORT_K, sub, n_proj * c), lambda b: (0, 0, 0)),
            pl.BlockSpec((sub, n_proj * c), lambda b: (0, 0)),
        ],
        out_specs=pl.BlockSpec((n_proj, seq, c), lambda b: (0, b, 0)),
        out_shape=jax.ShapeDtypeStruct((n_proj, m, c), jnp.bfloat16),
        compiler_params=_cparams(("arbitrary",)),
        name="hyproj",
    )(h.reshape(batch, seq, d), *([w_in] * n_proj), taps, bias)


@functools.lru_cache(maxsize=None)
def _dft_tables(seq):
    n1 = seq // _NP
    n_fft = 2 * seq
    k1 = np.arange(_HALF)[None, :, None]
    t = _NP * np.arange(n1)[None, None, :] + np.arange(_NP)[:, None, None]
    theta = ((k1 * t) % n_fft) * (2.0 * math.pi / n_fft)
    valid = k1 <= n1
    cos = np.where(valid, np.cos(theta), 0.0)
    sin = np.where(valid, np.sin(theta), 0.0)
    fwd = np.concatenate([cos, -sin], axis=1)
    weight = np.where((k1 == 0) | (k1 == n1), 1.0, 2.0) / n_fft
    inv = np.concatenate([weight * cos, -weight * sin], axis=1).transpose(0, 2, 1)
    return fwd.astype(np.float32), np.ascontiguousarray(inv).astype(np.float32)


def _fft4(cr, ci):
    d0r, d0i = cr[0] + cr[2], ci[0] + ci[2]
    d1r, d1i = cr[1] + cr[3], ci[1] + ci[3]
    d2r, d2i = cr[0] - cr[2], ci[0] - ci[2]
    er, ei = cr[1] - cr[3], ci[1] - ci[3]
    d3r, d3i = ei, -er
    return [(d0r + d1r, d0i + d1i), (d2r + d3r, d2i + d3i), (d0r - d1r, d0i - d1i), (d2r - d3r, d2i - d3i)]


def _fft8(re, im):
    br, bi = [None] * 8, [None] * 8
    for j in range(4):
        br[j], bi[j] = re[j] + re[j + 4], im[j] + im[j + 4]
        dr, di = re[j] - re[j + 4], im[j] - im[j + 4]
        if j == 0:
            br[4], bi[4] = dr, di
        elif j == 1:
            br[5], bi[5] = (dr + di) * _SQRT_HALF, (di - dr) * _SQRT_HALF
        elif j == 2:
            br[6], bi[6] = di, -dr
        else:
            br[7], bi[7] = (di - dr) * _SQRT_HALF, -(dr + di) * _SQRT_HALF
    even = _fft4(br[:4], bi[:4])
    odd = _fft4(br[4:], bi[4:])
    out = [None] * 8
    for k in range(4):
        out[2 * k], out[2 * k + 1] = even[k], odd[k]
    return [o[0] for o in out], [o[1] for o in out]


def _ifft8(re, im):
    o_im, o_re = _fft8(im, re)
    return o_re, o_im


def _filter_kernel(w1_ref, b1_ref, w2_ref, b2_ref, w3_ref, b3_ref, fr_ref, wof_ref, wob_ref, dl_ref,
                   fwd_ref, o_ref, h_scr, af_scr, ab_scr, *, seq):
    n1 = seq // _NP
    fw = _FILTER_WIDTH
    hp = lax.Precision.HIGHEST

    @pl.when(pl.program_id(1) == 0)
    def _():
        nb = _FILTER_BANDS
        r = lax.broadcasted_iota(jnp.int32, (nb, seq), 1)
        pos = (_NP * (r % n1) + r // n1).astype(jnp.float32)
        band_idx = lax.broadcasted_iota(jnp.int32, (nb, seq), 0).astype(jnp.float32)
        band = 1e-4 + band_idx * ((nb - 1 - 1e-4) / (nb - 1))
        ang = (2.0 * math.pi / seq) * pos * band
        first_row = lax.broadcasted_iota(jnp.int32, (fw - 2 * nb, seq), 0) == 0
        tail = jnp.where(first_row, jnp.concatenate([pos] * ((fw - 2 * nb) // nb), axis=0) * (1.0 / (seq - 1)), 0.0)
        z = jnp.concatenate([jnp.cos(ang), -jnp.sin(ang), tail], axis=0)
        freq = fr_ref[0]
        h = jnp.sin(freq * (jnp.dot(w1_ref[0], z, precision=hp, preferred_element_type=jnp.float32) + b1_ref[0]))
        h = jnp.sin(freq * (jnp.dot(w2_ref[0], h, precision=hp, preferred_element_type=jnp.float32) + b2_ref[0]))
        h = jnp.sin(freq * (jnp.dot(w3_ref[0], h, precision=hp, preferred_element_type=jnp.float32) + b3_ref[0]))
        h_scr[...] = h.T

    r = lax.broadcasted_iota(jnp.int32, (seq, 1), 0)
    pos = (_NP * (r % n1) + r // n1).astype(jnp.float32)
    h = h_scr[...].astype(jnp.bfloat16)
    decay = jnp.exp(-(pos * (1.0 / (seq - 1))) * dl_ref[...])
    hf = jnp.dot(h, wof_ref[0].astype(jnp.bfloat16), preferred_element_type=jnp.float32) * decay
    hb = jnp.dot(h, wob_ref[0].astype(jnp.bfloat16), preferred_element_type=jnp.float32) * decay
    hb = jnp.where(pos == 0.0, 0.0, hb)
    hf = hf.astype(jnp.bfloat16)
    hb = hb.astype(jnp.bfloat16)
    for n2 in range(_NP):
        af_scr[n2] = jnp.dot(fwd_ref[n2], hf[n2 * n1:(n2 + 1) * n1], preferred_element_type=jnp.float32)
        ab_scr[n2] = jnp.dot(fwd_ref[n2], hb[n2 * n1:(n2 + 1) * n1], preferred_element_type=jnp.float32)

    rc = _V7X_BF16_SUBLANES

    def body(i, carry):
        r0 = pl.multiple_of(i * rc, rc)
        fr, fi = _fft8([af_scr[n2, pl.ds(r0, rc)] for n2 in range(_NP)],
                       [af_scr[n2, pl.ds(_HALF + r0, rc)] for n2 in range(_NP)])
        gr, gi = _fft8([ab_scr[n2, pl.ds(r0, rc)] for n2 in range(_NP)],
                       [ab_scr[n2, pl.ds(_HALF + r0, rc)] for n2 in range(_NP)])
        for k2 in range(_NP):
            o_ref[0, 0, k2, 0, pl.ds(r0, rc)] = (fr[k2] + gr[k2]).astype(o_ref.dtype)
            o_ref[0, 0, k2, 1, pl.ds(r0, rc)] = (fi[k2] - gi[k2]).astype(o_ref.dtype)
        return carry

    lax.fori_loop(0, _HALF // rc, body, 0)


def _filter_spectra(w1, b1, w2, b2, w3, b3, freq, w_out, fwd_tab, *, seq, ct):
    n_layers = w1.shape[0]
    c = w_out.shape[2] // (2 * _ORDER)
    n_ct = c // ct
    fw = _FILTER_WIDTH
    w1r = jnp.concatenate([w1[:, 1:], w1[:, :1]], axis=1)
    w1t = jnp.pad(w1r, ((0, 0), (0, fw - w1.shape[1]), (0, 0))).transpose(0, 2, 1)
    col = lambda a: a.reshape(n_layers, fw, 1)
    deltas = jnp.abs(jnp.linspace(_MIN_DECAY, _MAX_DECAY, c, dtype=jnp.float32)).reshape(1, c)
    lmap3 = lambda l, s: (l, 0, 0)
    wo_spec = lambda direction: pl.BlockSpec(
        (1, fw, ct), lambda l, s: (l, 0, (2 * (s // n_ct) + direction) * n_ct + s % n_ct))
    return pl.pallas_call(
        functools.partial(_filter_kernel, seq=seq),
        grid=(n_layers, _ORDER * n_ct),
        in_specs=[
            pl.BlockSpec((1, fw, fw), lmap3), pl.BlockSpec((1, fw, 1), lmap3),
            pl.BlockSpec((1, fw, fw), lmap3), pl.BlockSpec((1, fw, 1), lmap3),
            pl.BlockSpec((1, fw, fw), lmap3), pl.BlockSpec((1, fw, 1), lmap3),
            pl.BlockSpec((1, fw, 1), lmap3),
            wo_spec(0), wo_spec(1),
            pl.BlockSpec((1, ct), lambda l, s: (0, s % n_ct)),
            pl.BlockSpec((_NP, 2 * _HALF, seq // _NP), lambda l, s: (0, 0, 0)),
        ],
        out_specs=pl.BlockSpec((1, 1, _NP, 2, _HALF, ct), lambda l, s: (l, s // n_ct, 0, 0, 0, s % n_ct)),
        out_shape=jax.ShapeDtypeStruct((n_layers, _ORDER, _NP, 2, _HALF, c), _STAGE_TWO_DTYPE),
        scratch_shapes=[pltpu.VMEM((seq, fw), jnp.float32),
                        pltpu.VMEM((_NP, 2 * _HALF, ct), jnp.float32),
                        pltpu.VMEM((_NP, 2 * _HALF, ct), jnp.float32)],
        compiler_params=_cparams(("arbitrary", "arbitrary")),
        name="filter_spectra",
    )(w1t, col(b1), w2.transpose(0, 2, 1), col(b2), w3.transpose(0, 2, 1), col(b3), col(freq),
      w_out, w_out, deltas, fwd_tab)


def _hyena_kernel(v_ref, x1_ref, x2_ref, kf_ref, skip_ref, fwd_ref, inv_ref, o_ref, a_scr, c_scr, z_scr, *, seq):
    n1 = seq // _NP
    rc = _V7X_BF16_SUBLANES
    ct = o_ref.shape[-1]

    def conv_order(order, z_in_ref, gate_ref, z_out_ref):
        for n2 in range(_NP):
            zb = z_in_ref[0, n2 * n1:(n2 + 1) * n1].astype(jnp.bfloat16)
            a_scr[n2] = jnp.dot(fwd_ref[n2], zb, preferred_element_type=jnp.float32).astype(a_scr.dtype)

        def body(i, carry):
            r0 = pl.multiple_of(i * rc, rc)
            for l0 in range(0, ct, _V7X_LANES):
                ls = slice(l0, l0 + _V7X_LANES)
                xr, xi = _fft8([a_scr[n2, pl.ds(r0, rc), ls] for n2 in range(_NP)],
                               [a_scr[n2, pl.ds(_HALF + r0, rc), ls] for n2 in range(_NP)])
                yr, yi = [], []
                for k2 in range(_NP):
                    kr = kf_ref[order, k2, 0, pl.ds(r0, rc), ls]
                    ki = kf_ref[order, k2, 1, pl.ds(r0, rc), ls]
                    yr.append(xr[k2] * kr - xi[k2] * ki)
                    yi.append(xr[k2] * ki + xi[k2] * kr)
                cr, ci = _ifft8(yr, yi)
                for t2 in range(_NP):
                    c_scr[t2, pl.ds(r0, rc), ls] = cr[t2].astype(jnp.bfloat16)
                    c_scr[t2, pl.ds(_HALF + r0, rc), ls] = ci[t2].astype(jnp.bfloat16)
            return carry

        lax.fori_loop(0, _HALF // rc, body, 0)

        skip = skip_ref[order:order + 1]
        for t2 in range(_NP):
            rows = slice(t2 * n1, (t2 + 1) * n1)
            y = jnp.dot(inv_ref[t2], c_scr[t2], preferred_element_type=jnp.float32)
            z = z_in_ref[0, rows].astype(jnp.float32)
            gate = gate_ref[0, rows].astype(jnp.float32)
            z_out_ref[0, rows] = (gate * (y + z * skip)).astype(z_out_ref.dtype)

    conv_order(0, v_ref, x1_ref, z_scr)
    conv_order(1, z_scr, x2_ref, o_ref)


def _hyena(proj, kf_all, layer, skip, fwd_tab, inv_tab, *, batch, seq, ct):
    n_proj, m, c = proj.shape
    n1 = seq // _NP
    proj4 = proj.reshape(n_proj, batch, seq, c)
    pspec = lambda j: pl.BlockSpec((None, 1, seq, ct), lambda t, b, j=j: (j, b, 0, t))
    resident = dict(pipeline_mode=pl.Buffered(1))
    out = pl.pallas_call(
        functools.partial(_hyena_kernel, seq=seq),
        grid=(c // ct, batch),
        in_specs=[
            pspec(0), pspec(1), pspec(2),
            pl.BlockSpec((None, _ORDER, _NP, 2, _HALF, ct), lambda t, b: (layer, 0, 0, 0, 0, t), **resident),
            pl.BlockSpec((_ORDER, ct), lambda t, b: (0, t)),
            pl.BlockSpec((_NP, 2 * _HALF, n1), lambda t, b: (0, 0, 0), **resident),
            pl.BlockSpec((_NP, n1, 2 * _HALF), lambda t, b: (0, 0, 0), **resident),
        ],
        out_specs=pl.BlockSpec((1, seq, ct), lambda t, b: (b, 0, t)),
        out_shape=jax.ShapeDtypeStruct((batch, seq, c), jnp.bfloat16),
        scratch_shapes=[pltpu.VMEM((_NP, 2 * _HALF, ct), _STAGE_TWO_DTYPE),
                        pltpu.VMEM((_NP, 2 * _HALF, ct), jnp.bfloat16),
                        pltpu.VMEM((1, seq, ct), jnp.float32)],
        compiler_params=_cparams(("arbitrary", "arbitrary")),
        name="hyena",
    )(proj4, proj4, proj4, kf_all, skip, fwd_tab, inv_tab)
    return out.reshape(m, c)


def kernel(x, ffn1_pre_g, ffn1_w_gate, ffn1_w_up, ffn1_w_down, ffn1_post_g, mix_pre_g, mix_w_in, gmlp_ln_g, gmlp_ln_b, gmlp_w_s, gmlp_b_s, hy_conv_w, hy_conv_b, hy_filt_w1, hy_filt_b1, hy_filt_w2, hy_filt_b2, hy_filt_w3, hy_filt_b3, hy_filt_freq, hy_filt_w_out, hy_skip, mix_w_out, mix_post_g, ffn2_pre_g, ffn2_w_gate, ffn2_w_up, ffn2_w_down, ffn2_post_g):
    batch, seq, d = x.shape
    depth = ffn1_pre_g.shape[0]
    da = gmlp_ln_g.shape[1]
    assert seq % (_NP * _CHUNK) == 0 and seq // _NP + 1 <= _HALF
    m = batch * seq
    ffn_tiles = dict(batch=batch, tm=512, chunk=_V7X_MXU_DIM)
    ct = _V7X_MXU_DIM
    xp = x.reshape(m, d)

    fwd_np, inv_np = _dft_tables(seq)
    fwd_bf16 = jnp.asarray(fwd_np).astype(jnp.bfloat16)
    inv_bf16 = jnp.asarray(inv_np).astype(jnp.bfloat16)
    kf_all = _filter_spectra(hy_filt_w1, hy_filt_b1, hy_filt_w2, hy_filt_b2, hy_filt_w3, hy_filt_b3,
                             hy_filt_freq, hy_filt_w_out, fwd_bf16, seq=seq, ct=ct)

    ffn1_w = (ffn1_w_gate, ffn1_w_up, ffn1_w_down)
    ffn2_w = (ffn2_w_gate, ffn2_w_up, ffn2_w_down)
    w_in = mix_w_in
    w_out = mix_w_out.astype(jnp.bfloat16)

    for l in range(depth):
        xp, h = _ffn(xp, l, ffn1_pre_g, *ffn1_w, ffn1_post_g, g_next=mix_pre_g,
                     relayout="deinterleave" if l == 0 else None, **ffn_tiles)
        ya = _gmlp(h, w_in, l, gmlp_ln_g[l], gmlp_ln_b[l], gmlp_w_s[l], gmlp_b_s[l], batch=batch, seq=seq, jb=8)
        proj = _hyproj(h, w_in, l, hy_conv_w[l], hy_conv_b[l], batch=batch, seq=seq)
        yb = _hyena(proj, kf_all, l, hy_skip[l], fwd_bf16, inv_bf16, batch=batch, seq=seq, ct=proj.shape[-1])
        xp = _ffn(xp, l, ffn2_pre_g, *ffn2_w, ffn2_post_g, mix=(ya, yb, w_out, mix_post_g),
                  relayout="interleave" if l == depth - 1 else None, **ffn_tiles)

    return xp.reshape(batch, seq, d)
```

```python
import functools
import math

import jax
import jax.numpy as jnp
import numpy as np
from jax import lax
from jax.experimental import pallas as pl
from jax.experimental.pallas import tpu as pltpu

_CHUNK = 128
_HEADS_A = 8
_HEAD_DIM_A = 64
_ORDER = 2
_SHORT_K = 3
_FILTER_BANDS = 16
_FILTER_WIDTH = 64
_DECAY_TARGET = 1e-2
_MAX_DECAY = math.log(_DECAY_TARGET) / 0.3
_MIN_DECAY = math.log(_DECAY_TARGET) / 1.5
_HALF_STEP = 0.5
_RMS_EPS = 1e-6
_LN_EPS = 1e-5

_V7X_LANES = 128
_V7X_F32_SUBLANES = 8
_V7X_BF16_SUBLANES = 16
_V7X_MXU_DIM = 256
_V7X_VMEM_LIMIT_BYTES = 62 * 1024 * 1024

_NP = 8
_HALF = 272
_SQRT_HALF = 0.7071067811865476
_STAGE_TWO_DTYPE = jnp.bfloat16


def _cparams(semantics):
    return pltpu.CompilerParams(dimension_semantics=semantics, vmem_limit_bytes=_V7X_VMEM_LIMIT_BYTES)


def _rms_norm(x, g):
    return x * lax.rsqrt(jnp.mean(x * x, axis=-1, keepdims=True) + _RMS_EPS) * g


def _deinterleave_rows(val, slab_scr):
    rows, d = val.shape
    n_slabs = d // _V7X_LANES
    for k in range(n_slabs):
        slab_scr[k] = val[:, k * _V7X_LANES:(k + 1) * _V7X_LANES]
    return jnp.concatenate(
        [jnp.concatenate([slab_scr[k, pl.ds(n2, rows // _NP, stride=_NP), :] for k in range(n_slabs)], axis=1)
         for n2 in range(_NP)], axis=0)


def _interleave_rows(val, slab_scr):
    rows, d = val.shape
    per = rows // _NP
    n_slabs = d // _V7X_LANES
    for n2 in range(_NP):
        for k in range(n_slabs):
            slab_scr[k, pl.ds(n2, per, stride=_NP), :] = val[n2 * per:(n2 + 1) * per, k * _V7X_LANES:(k + 1) * _V7X_LANES]
    return jnp.concatenate([slab_scr[k] for k in range(n_slabs)], axis=1)


def _ffn_kernel(*refs, layer, chunk, mix_in, norm_out, relayout):
    refs = list(refs)
    x_ref = refs.pop(0)
    if mix_in:
        ya_ref, yb_ref, wa_ref, wb_ref, gmix_ref = refs[:5]
        refs = refs[5:]
    gpre_ref, wg_hbm, wu_hbm, wd_hbm, gpost_ref = refs[:5]
    refs = refs[5:]
    gnext_ref = refs.pop(0) if norm_out else None
    o_ref = refs.pop(0)
    hn_ref = refs.pop(0) if norm_out else None
    wg_scr, wu_scr, wd_scr, w_sem = refs[:4]
    slab_scr = refs[4] if relayout else None
    blocked_in = relayout == "interleave"
    blocked_out = relayout == "deinterleave"
    load = (lambda r: r[0].reshape(-1, r.shape[-1])) if blocked_in else (lambda r: r[...])
    dff = wg_scr.shape[1]
    bounds = [(c0, min(c0 + chunk, dff)) for c0 in range(0, dff, chunk)]

    def weight_copies(c):
        cols = pl.ds(bounds[c][0], bounds[c][1] - bounds[c][0])
        return (pltpu.make_async_copy(wg_hbm.at[layer, :, cols], wg_scr.at[:, cols], w_sem.at[c, 0]),
                pltpu.make_async_copy(wu_hbm.at[layer, :, cols], wu_scr.at[:, cols], w_sem.at[c, 1]),
                pltpu.make_async_copy(wd_hbm.at[layer, cols, :], wd_scr.at[cols, :], w_sem.at[c, 2]))

    def tile(first_step):
        if first_step:
            for c in range(len(bounds)):
                for copy in weight_copies(c):
                    copy.start()
        x = load(x_ref)
        if mix_in:
            y = jnp.dot(load(ya_ref), wa_ref[...], preferred_element_type=jnp.float32)
            y = y + jnp.dot(load(yb_ref), wb_ref[...], preferred_element_type=jnp.float32)
            x = x + _rms_norm(y, gmix_ref[...])
        h = _rms_norm(x, gpre_ref[...])
        acc = jnp.zeros(x.shape, jnp.float32)
        for c, (c0, c1) in enumerate(bounds):
            if first_step:
                for copy in weight_copies(c):
                    copy.wait()
            g = jnp.dot(h, wg_scr[:, c0:c1], preferred_element_type=jnp.float32)
            u = jnp.dot(h, wu_scr[:, c0:c1], preferred_element_type=jnp.float32)
            a = g * jax.nn.sigmoid(g) * u
            acc = acc + jnp.dot(a, wd_scr[c0:c1, :], preferred_element_type=jnp.float32)
        out = x + _rms_norm(acc, gpost_ref[...])
        if relayout == "deinterleave":
            out = _deinterleave_rows(out, slab_scr)
        elif relayout == "interleave":
            out = _interleave_rows(out, slab_scr)
        if norm_out:
            hn = _rms_norm(out, gnext_ref[...]).astype(jnp.bfloat16)
            if blocked_out:
                hn_ref[0] = hn.reshape(hn_ref.shape[1:])
            else:
                hn_ref[...] = hn
        if blocked_out:
            o_ref[0] = out.reshape(o_ref.shape[1:])
        else:
            o_ref[...] = out

    first = pl.program_id(0) == 0
    pl.when(first)(functools.partial(tile, True))
    pl.when(jnp.logical_not(first))(functools.partial(tile, False))


def _ffn(x, layer, g_pre, w_gate, w_up, w_down, g_post, *, batch, tm, chunk, mix=None, g_next=None, relayout=None):
    m, d = x.shape
    dff = w_gate.shape[2]
    seq = m // batch
    steps_per_seq = seq // tm
    assert dff % _V7X_MXU_DIM == 0 and chunk % _V7X_MXU_DIM == 0 and seq % tm == 0 and tm % (_NP * _V7X_BF16_SUBLANES) == 0
    row = lambda a: a[layer].reshape(1, d)
    flat_tile = lambda width: pl.BlockSpec((tm, width), lambda i: (i, 0))
    blocked_tile = lambda width: pl.BlockSpec((1, _NP, tm // _NP, width),
                                              lambda i: (i // steps_per_seq, 0, i % steps_per_seq, 0))
    blocked = lambda a: a.reshape(batch, _NP, seq // _NP, a.shape[-1])
    if relayout == "interleave":
        in_tile, put = blocked_tile, blocked
    else:
        in_tile, put = flat_tile, lambda a: a
    vec = pl.BlockSpec((1, d), lambda i: (0, 0))
    resident = lambda r, c, blk=0: pl.BlockSpec((None, r, c), lambda i: (layer, blk, 0), pipeline_mode=pl.Buffered(1))
    args, specs = [put(x)], [in_tile(d)]
    if mix is not None:
        ya, yb, w_out, g_mix = mix
        da = ya.shape[1]
        assert yb.shape[1] == da and w_out.shape[1:] == (2 * da, d)
        args += [put(ya), put(yb), w_out, w_out, row(g_mix)]
        specs += [in_tile(da), in_tile(da), resident(da, d, 0), resident(da, d, 1), vec]
    args += [row(g_pre), w_gate, w_up, w_down, _HALF_STEP * row(g_post)]
    in_hbm = pl.BlockSpec(memory_space=pl.ANY)
    specs += [vec, in_hbm, in_hbm, in_hbm, vec]
    if relayout == "deinterleave":
        out_tile, out_dims = blocked_tile(d), (batch, _NP, seq // _NP, d)
    else:
        out_tile, out_dims = flat_tile(d), (m, d)
    out_shape = [jax.ShapeDtypeStruct(out_dims, jnp.float32)]
    out_specs = [out_tile]
    if g_next is not None:
        args.append(row(g_next))
        specs.append(vec)
        out_shape.append(jax.ShapeDtypeStruct(out_dims, jnp.bfloat16))
        out_specs.append(out_tile)
    n_chunks = -(-dff // chunk)
    scratch = [pltpu.VMEM((d, dff), w_gate.dtype), pltpu.VMEM((d, dff), w_up.dtype), pltpu.VMEM((dff, d), w_down.dtype),
               pltpu.SemaphoreType.DMA((n_chunks, 3))]
    if relayout:
        scratch.append(pltpu.VMEM((d // _V7X_LANES, tm, _V7X_LANES), jnp.float32))
    outs = pl.pallas_call(
        functools.partial(_ffn_kernel, layer=layer, chunk=chunk, mix_in=mix is not None,
                          norm_out=g_next is not None, relayout=relayout),
        grid=(m // tm,),
        in_specs=specs,
        out_specs=out_specs,
        out_shape=out_shape,
        scratch_shapes=scratch,
        compiler_params=_cparams(("arbitrary",)),
        name="ffn",
    )(*args)
    outs = [o.reshape(m, d) for o in outs]
    return outs if g_next is not None else outs[0]


def _gmlp_kernel(h_ref, w_ref, lng_ref, lnb_ref, ws_ref, bs_ref, o_ref, *, jb, da):
    rows = _CHUNK // _NP
    d = h_ref.shape[-1]
    h = h_ref[...].reshape(_NP * jb * rows, d)
    gelu = lambda p: 0.5 * p * (1.0 + lax.erf(p * _SQRT_HALF))
    u = gelu(jnp.dot(h, w_ref[:, :da], preferred_element_type=jnp.float32))
    v = gelu(jnp.dot(h, w_ref[:, da:], preferred_element_type=jnp.float32))
    mu = jnp.mean(v, axis=-1, keepdims=True)
    vc = v - mu
    var = jnp.mean(vc * vc, axis=-1, keepdims=True)
    vn = (vc * lax.rsqrt(var + _LN_EPS) * lng_ref[...] + lnb_ref[...]).astype(jnp.bfloat16)
    lane = lax.broadcasted_iota(jnp.int32, (_CHUNK, _V7X_LANES), 1)
    first_head = lane < _HEAD_DIM_A
    n_pairs = da // _V7X_LANES
    for j in range(jb):
        starts = [(n2 * jb + j) * rows for n2 in range(_NP)]
        vchunk = jnp.concatenate([vn[s:s + rows] for s in starts], axis=0)
        mixed = []
        for q in range(n_pairs):
            r = jnp.dot(ws_ref[q], vchunk[:, q * _V7X_LANES:(q + 1) * _V7X_LANES],
                        preferred_element_type=jnp.float32)
            mixed.append(jnp.where(first_head, r[:_CHUNK], r[_CHUNK:]))
        mixed = jnp.concatenate(mixed, axis=1) + bs_ref[...]
        for n2 in range(_NP):
            s = starts[n2]
            o_ref[n2, j * rows:(j + 1) * rows] = (u[s:s + rows] * mixed[n2 * rows:(n2 + 1) * rows]).astype(o_ref.dtype)


def _gmlp(h, w_in, layer, ln_g, ln_b, w_s, b_s, *, batch, seq, jb):
    m, d = h.shape
    da = ln_g.shape[0]
    rows = _CHUNK // _NP
    n_chunks = seq // _CHUNK
    assert rows == _V7X_BF16_SUBLANES
    n1 = seq // _NP
    h3 = h.reshape(batch * _NP, n1, d)
    tau = np.arange(_CHUNK)
    perm = np.zeros((_CHUNK, _CHUNK), np.float32)
    perm[(tau % _NP) * rows + tau // _NP, tau] = 1.0
    exact = lax.Precision.HIGHEST
    ws_p = jnp.einsum("pm,hmc,qc->hpq", perm, w_s, perm, precision=exact)
    ws_pairs = ws_p.reshape(_HEADS_A // 2, 2 * _CHUNK, _CHUNK).astype(jnp.bfloat16)
    bs_p = jnp.einsum("hm,pm->hp", b_s, perm, precision=exact)
    bs_full = jnp.repeat(bs_p.T, _HEAD_DIM_A, axis=1)
    out = pl.pallas_call(
        functools.partial(_gmlp_kernel, jb=jb, da=da),
        grid=(batch, n_chunks // jb),
        in_specs=[
            pl.BlockSpec((_NP, jb * rows, d), lambda b, j: (b, j, 0)),
            pl.BlockSpec((None, d, 2 * da), lambda b, j: (layer, 0, 0)),
            pl.BlockSpec((1, da), lambda b, j: (0, 0)),
            pl.BlockSpec((1, da), lambda b, j: (0, 0)),
            pl.BlockSpec((_HEADS_A // 2, 2 * _CHUNK, _CHUNK), lambda b, j: (0, 0, 0)),
            pl.BlockSpec((_CHUNK, da), lambda b, j: (0, 0)),
        ],
        out_specs=pl.BlockSpec((_NP, jb * rows, da), lambda b, j: (b, j, 0)),
        out_shape=jax.ShapeDtypeStruct((batch * _NP, n1, da), jnp.bfloat16),
        compiler_params=_cparams(("arbitrary", "arbitrary")),
        name="gmlp",
    )(h3, w_in, ln_g.reshape(1, da), ln_b.reshape(1, da), ws_pairs, bs_full)
    return out.reshape(m, da)


def _hyproj_kernel(h_ref, *refs, n1):
    n_proj = _ORDER + 1
    w_refs, (cw_ref, cb_ref, o_ref) = refs[:n_proj], refs[n_proj:]
    c = o_ref.shape[-1]
    h = h_ref[0]
    sub = cw_ref.shape[1]
    tiled = lambda a: a.reshape(n1 // sub, sub, c)
    for j in range(n_proj):
        p = jnp.dot(h, w_refs[j][...], preferred_element_type=jnp.float32)
        blocks = [p[k * n1:(k + 1) * n1] for k in range(_NP)]
        row = lax.broadcasted_iota(jnp.int32, blocks[0].shape, 0)
        before_first = jnp.where(row == 0, 0.0, pltpu.roll(blocks[_NP - 1], 1, axis=0))
        after_last = jnp.where(row == n1 - 1, 0.0, pltpu.roll(blocks[0], n1 - 1, axis=0))
        cols = slice(j * c, (j + 1) * c)
        w0, w1, w2 = cw_ref[0, :, cols], cw_ref[1, :, cols], cw_ref[2, :, cols]
        bias = cb_ref[:, cols]
        for k in range(_NP):
            prev = blocks[k - 1] if k > 0 else before_first
            nxt = blocks[k + 1] if k < _NP - 1 else after_last
            y = ((bias + tiled(prev) * w0) + tiled(blocks[k]) * w1) + tiled(nxt) * w2
            o_ref[j, k * n1:(k + 1) * n1] = y.reshape(n1, c).astype(o_ref.dtype)


def _hyproj(h, w_in, layer, conv_w, conv_b, *, batch, seq):
    m, d = h.shape
    n_proj = _ORDER + 1
    c = conv_w.shape[1] // n_proj
    first = (w_in.shape[2] - n_proj * c) // c
    w_spec = lambda j: pl.BlockSpec((None, d, c), lambda b: (layer, 0, first + j), pipeline_mode=pl.Buffered(1))
    sub = _V7X_F32_SUBLANES
    taps = jnp.broadcast_to(conv_w[:, None, :], (_SHORT_K, sub, n_proj * c))
    bias = jnp.broadcast_to(conv_b[None, :], (sub, n_proj * c))
    return pl.pallas_call(
        functools.partial(_hyproj_kernel, n1=seq // _NP),
        grid=(batch,),
        in_specs=[pl.BlockSpec((1, seq, d), lambda b: (b, 0, 0))] + [w_spec(j) for j in range(n_proj)] + [
            pl.BlockSpec((_SHORT_K, sub, n_proj * c), lambda b: (0, 0, 0)),
            pl.BlockSpec((sub, n_proj * c), lambda b: (0, 0)),
        ],
        out_specs=pl.BlockSpec((n_proj, seq, c), lambda b: (0, b, 0)),
        out_shape=jax.ShapeDtypeStruct((n_proj, m, c), jnp.bfloat16),
        compiler_params=_cparams(("arbitrary",)),
        name="hyproj",
    )(h.reshape(batch, seq, d), *([w_in] * n_proj), taps, bias)


@functools.lru_cache(maxsize=None)
def _dft_tables(seq):
    n1 = seq // _NP
    n_fft = 2 * seq
    k1 = np.arange(_HALF)[None, :, None]
    t = _NP * np.arange(n1)[None, None, :] + np.arange(_NP)[:, None, None]
    theta = ((k1 * t) % n_fft) * (2.0 * math.pi / n_fft)
    valid = k1 <= n1
    cos = np.where(valid, np.cos(theta), 0.0)
    sin = np.where(valid, np.sin(theta), 0.0)
    fwd = np.concatenate([cos, -sin], axis=1)
    weight = np.where((k1 == 0) | (k1 == n1), 1.0, 2.0) / n_fft
    inv = np.concatenate([weight * cos, -weight * sin], axis=1).transpose(0, 2, 1)
    return fwd.astype(np.float32), np.ascontiguousarray(inv).astype(np.float32)


def _fft4(cr, ci):
    d0r, d0i = cr[0] + cr[2], ci[0] + ci[2]
    d1r, d1i = cr[1] + cr[3], ci[1] + ci[3]
    d2r, d2i = cr[0] - cr[2], ci[0] - ci[2]
    er, ei = cr[1] - cr[3], ci[1] - ci[3]
    d3r, d3i = ei, -er
    return [(d0r + d1r, d0i + d1i), (d2r + d3r, d2i + d3i), (d0r - d1r, d0i - d1i), (d2r - d3r, d2i - d3i)]


def _fft8(re, im):
    br, bi = [None] * 8, [None] * 8
    for j in range(4):
        br[j], bi[j] = re[j] + re[j + 4], im[j] + im[j + 4]
        dr, di = re[j] - re[j + 4], im[j] - im[j + 4]
        if j == 0:
            br[4], bi[4] = dr, di
        elif j == 1:
            br[5], bi[5] = (dr + di) * _SQRT_HALF, (di - dr) * _SQRT_HALF
        elif j == 2:
            br[6], bi[6] = di, -dr
        else:
            br[7], bi[7] = (di - dr) * _SQRT_HALF, -(dr + di) * _SQRT_HALF
    even = _fft4(br[:4], bi[:4])
    odd = _fft4(br[4:], bi[4:])
    out = [None] * 8
    for k in range(4):
        out[2 * k], out[2 * k + 1] = even[k], odd[k]
    return [o[0] for o in out], [o[1] for o in out]


def _ifft8(re, im):
    o_im, o_re = _fft8(im, re)
    return o_re, o_im


def _filter_kernel(w1_ref, b1_ref, w2_ref, b2_ref, w3_ref, b3_ref, fr_ref, wof_ref, wob_ref, dl_ref,
                   fwd_ref, o_ref, h_scr, af_scr, ab_scr, *, seq):
    n1 = seq // _NP
    fw = _FILTER_WIDTH
    hp = lax.Precision.HIGHEST

    @pl.when(pl.program_id(1) == 0)
    def _():
        nb = _FILTER_BANDS
        r = lax.broadcasted_iota(jnp.int32, (nb, seq), 1)
        pos = (_NP * (r % n1) + r // n1).astype(jnp.float32)
        band_idx = lax.broadcasted_iota(jnp.int32, (nb, seq), 0).astype(jnp.float32)
        band = 1e-4 + band_idx * ((nb - 1 - 1e-4) / (nb - 1))
        ang = (2.0 * math.pi / seq) * pos * band
        first_row = lax.broadcasted_iota(jnp.int32, (fw - 2 * nb, seq), 0) == 0
        tail = jnp.where(first_row, jnp.concatenate([pos] * ((fw - 2 * nb) // nb), axis=0) * (1.0 / (seq - 1)), 0.0)
        z = jnp.concatenate([jnp.cos(ang), -jnp.sin(ang), tail], axis=0)
        freq = fr_ref[0]
        h = jnp.sin(freq * (jnp.dot(w1_ref[0], z, precision=hp, preferred_element_type=jnp.float32) + b1_ref[0]))
        h = jnp.sin(freq * (jnp.dot(w2_ref[0], h, precision=hp, preferred_element_type=jnp.float32) + b2_ref[0]))
        h = jnp.sin(freq * (jnp.dot(w3_ref[0], h, precision=hp, preferred_element_type=jnp.float32) + b3_ref[0]))
        h_scr[...] = h.T

    r = lax.broadcasted_iota(jnp.int32, (seq, 1), 0)
    pos = (_NP * (r % n1) + r // n1).astype(jnp.float32)
    h = h_scr[...].astype(jnp.bfloat16)
    decay = jnp.exp(-(pos * (1.0 / (seq - 1))) * dl_ref[...])
    hf = jnp.dot(h, wof_ref[0].astype(jnp.bfloat16), preferred_element_type=jnp.float32) * decay
    hb = jnp.dot(h, wob_ref[0].astype(jnp.bfloat16), preferred_element_type=jnp.float32) * decay
    hb = jnp.where(pos == 0.0, 0.0, hb)
    hf = hf.astype(jnp.bfloat16)
    hb = hb.astype(jnp.bfloat16)
    for n2 in range(_NP):
        af_scr[n2] = jnp.dot(fwd_ref[n2], hf[n2 * n1:(n2 + 1) * n1], preferred_element_type=jnp.float32)
        ab_scr[n2] = jnp.dot(fwd_ref[n2], hb[n2 * n1:(n2 + 1) * n1], preferred_element_type=jnp.float32)

    rc = _V7X_BF16_SUBLANES

    def body(i, carry):
        r0 = pl.multiple_of(i * rc, rc)
        back = [(_NP - n2) % _NP for n2 in range(_NP)]
        kr, ki = _fft8([af_scr[n2, pl.ds(r0, rc)] + ab_scr[back[n2], pl.ds(r0, rc)] for n2 in range(_NP)],
                       [af_scr[n2, pl.ds(_HALF + r0, rc)] - ab_scr[back[n2], pl.ds(_HALF + r0, rc)] for n2 in range(_NP)])
        for k2 in range(_NP):
            o_ref[0, 0, k2, 0, pl.ds(r0, rc)] = kr[k2].astype(o_ref.dtype)
            o_ref[0, 0, k2, 1, pl.ds(r0, rc)] = ki[k2].astype(o_ref.dtype)
        return carry

    lax.fori_loop(0, _HALF // rc, body, 0)


def _filter_spectra(w1, b1, w2, b2, w3, b3, freq, w_out, fwd_tab, *, seq, ct):
    n_layers = w1.shape[0]
    c = w_out.shape[2] // (2 * _ORDER)
    n_ct = c // ct
    fw = _FILTER_WIDTH
    w1r = jnp.concatenate([w1[:, 1:], w1[:, :1]], axis=1)
    w1t = jnp.pad(w1r, ((0, 0), (0, fw - w1.shape[1]), (0, 0))).transpose(0, 2, 1)
    col = lambda a: a.reshape(n_layers, fw, 1)
    deltas = jnp.abs(jnp.linspace(_MIN_DECAY, _MAX_DECAY, c, dtype=jnp.float32)).reshape(1, c)
    lmap3 = lambda l, s: (l, 0, 0)
    wo_spec = lambda direction: pl.BlockSpec(
        (1, fw, ct), lambda l, s: (l, 0, (2 * (s // n_ct) + direction) * n_ct + s % n_ct))
    return pl.pallas_call(
        functools.partial(_filter_kernel, seq=seq),
        grid=(n_layers, _ORDER * n_ct),
        in_specs=[
            pl.BlockSpec((1, fw, fw), lmap3), pl.BlockSpec((1, fw, 1), lmap3),
            pl.BlockSpec((1, fw, fw), lmap3), pl.BlockSpec((1, fw, 1), lmap3),
            pl.BlockSpec((1, fw, fw), lmap3), pl.BlockSpec((1, fw, 1), lmap3),
            pl.BlockSpec((1, fw, 1), lmap3),
            wo_spec(0), wo_spec(1),
            pl.BlockSpec((1, ct), lambda l, s: (0, s % n_ct)),
            pl.BlockSpec((_NP, 2 * _HALF, seq // _NP), lambda l, s: (0, 0, 0)),
        ],
        out_specs=pl.BlockSpec((1, 1, _NP, 2, _HALF, ct), lambda l, s: (l, s // n_ct, 0, 0, 0, s % n_ct)),
        out_shape=jax.ShapeDtypeStruct((n_layers, _ORDER, _NP, 2, _HALF, c), _STAGE_TWO_DTYPE),
        scratch_shapes=[pltpu.VMEM((seq, fw), jnp.float32),
                        pltpu.VMEM((_NP, 2 * _HALF, ct), jnp.float32),
                        pltpu.VMEM((_NP, 2 * _HALF, ct), jnp.float32)],
        compiler_params=_cparams(("arbitrary", "arbitrary")),
        name="filter_spectra",
    )(w1t, col(b1), w2.transpose(0, 2, 1), col(b2), w3.transpose(0, 2, 1), col(b3), col(freq),
      w_out, w_out, deltas, fwd_tab)


def _hyena_kernel(v_ref, x1_ref, x2_ref, kf_ref, skip_ref, fwd_ref, inv_ref, o_ref, a_scr, c_scr, z_scr, *, seq):
    n1 = seq // _NP
    rc = _V7X_BF16_SUBLANES
    ct = o_ref.shape[-1]

    def conv_order(order, z_in_ref, gate_ref, z_out_ref):
        for n2 in range(_NP):
            zb = z_in_ref[0, n2 * n1:(n2 + 1) * n1].astype(jnp.bfloat16)
            a_scr[n2] = jnp.dot(fwd_ref[n2], zb, preferred_element_type=jnp.float32).astype(a_scr.dtype)

        def body(i, carry):
            r0 = pl.multiple_of(i * rc, rc)
            for l0 in range(0, ct, _V7X_LANES):
                ls = slice(l0, l0 + _V7X_LANES)
                xr, xi = _fft8([a_scr[n2, pl.ds(r0, rc), ls] for n2 in range(_NP)],
                               [a_scr[n2, pl.ds(_HALF + r0, rc), ls] for n2 in range(_NP)])
                yr, yi = [], []
                for k2 in range(_NP):
                    kr = kf_ref[order, k2, 0, pl.ds(r0, rc), ls]
                    ki = kf_ref[order, k2, 1, pl.ds(r0, rc), ls]
                    yr.append(xr[k2] * kr - xi[k2] * ki)
                    yi.append(xr[k2] * ki + xi[k2] * kr)
                cr, ci = _ifft8(yr, yi)
                for t2 in range(_NP):
                    c_scr[t2, pl.ds(r0, rc), ls] = cr[t2].astype(jnp.bfloat16)
                    c_scr[t2, pl.ds(_HALF + r0, rc), ls] = ci[t2].astype(jnp.bfloat16)
            return carry

        lax.fori_loop(0, _HALF // rc, body, 0)

        skip = skip_ref[order:order + 1]
        for t2 in range(_NP):
            rows = slice(t2 * n1, (t2 + 1) * n1)
            y = jnp.dot(inv_ref[t2], c_scr[t2], preferred_element_type=jnp.float32)
            z = z_in_ref[0, rows].astype(jnp.float32)
            gate = gate_ref[0, rows].astype(jnp.float32)
            z_out_ref[0, rows] = (gate * (y + z * skip)).astype(z_out_ref.dtype)

    conv_order(0, v_ref, x1_ref, z_scr)
    conv_order(1, z_scr, x2_ref, o_ref)


def _hyena(proj, kf_all, layer, skip, fwd_tab, inv_tab, *, batch, seq, ct):
    n_proj, m, c = proj.shape
    n1 = seq // _NP
    proj4 = proj.reshape(n_proj, batch, seq, c)
    pspec = lambda j: pl.BlockSpec((None, 1, seq, ct), lambda t, b, j=j: (j, b, 0, t))
    resident = dict(pipeline_mode=pl.Buffered(1))
    out = pl.pallas_call(
        functools.partial(_hyena_kernel, seq=seq),
        grid=(c // ct, batch),
        in_specs=[
            pspec(0), pspec(1), pspec(2),
            pl.BlockSpec((None, _ORDER, _NP, 2, _HALF, ct), lambda t, b: (layer, 0, 0, 0, 0, t), **resident),
            pl.BlockSpec((_ORDER, ct), lambda t, b: (0, t)),
            pl.BlockSpec((_NP, 2 * _HALF, n1), lambda t, b: (0, 0, 0), **resident),
            pl.BlockSpec((_NP, n1, 2 * _HALF), lambda t, b: (0, 0, 0), **resident),
        ],
        out_specs=pl.BlockSpec((1, seq, ct), lambda t, b: (b, 0, t)),
        out_shape=jax.ShapeDtypeStruct((batch, seq, c), jnp.bfloat16),
        scratch_shapes=[pltpu.VMEM((_NP, 2 * _HALF, ct), _STAGE_TWO_DTYPE),
                        pltpu.VMEM((_NP, 2 * _HALF, ct), jnp.bfloat16),
                        pltpu.VMEM((1, seq, ct), jnp.float32)],
        compiler_params=_cparams(("arbitrary", "arbitrary")),
        name="hyena",
    )(proj4, proj4, proj4, kf_all, skip, fwd_tab, inv_tab)
    return out.reshape(m, c)


def kernel(x, ffn1_pre_g, ffn1_w_gate, ffn1_w_up, ffn1_w_down, ffn1_post_g, mix_pre_g, mix_w_in, gmlp_ln_g, gmlp_ln_b, gmlp_w_s, gmlp_b_s, hy_conv_w, hy_conv_b, hy_filt_w1, hy_filt_b1, hy_filt_w2, hy_filt_b2, hy_filt_w3, hy_filt_b3, hy_filt_freq, hy_filt_w_out, hy_skip, mix_w_out, mix_post_g, ffn2_pre_g, ffn2_w_gate, ffn2_w_up, ffn2_w_down, ffn2_post_g):
    batch, seq, d = x.shape
    depth = ffn1_pre_g.shape[0]
    da = gmlp_ln_g.shape[1]
    assert seq % (_NP * _CHUNK) == 0 and seq // _NP + 1 <= _HALF
    m = batch * seq
    ffn_tiles = dict(batch=batch, tm=512, chunk=_V7X_MXU_DIM)
    ct = _V7X_MXU_DIM
    xp = x.reshape(m, d)

    fwd_np, inv_np = _dft_tables(seq)
    fwd_bf16 = jnp.asarray(fwd_np).astype(jnp.bfloat16)
    inv_bf16 = jnp.asarray(inv_np).astype(jnp.bfloat16)
    kf_all = _filter_spectra(hy_filt_w1, hy_filt_b1, hy_filt_w2, hy_filt_b2, hy_filt_w3, hy_filt_b3,
                             hy_filt_freq, hy_filt_w_out, fwd_bf16, seq=seq, ct=ct)

    ffn1_w = (ffn1_w_gate, ffn1_w_up, ffn1_w_down)
    ffn2_w = (ffn2_w_gate, ffn2_w_up, ffn2_w_down)
    w_in = mix_w_in
    w_out = mix_w_out.astype(jnp.bfloat16)

    for l in range(depth):
        xp, h = _ffn(xp, l, ffn1_pre_g, *ffn1_w, ffn1_post_g, g_next=mix_pre_g,
                     relayout="deinterleave" if l == 0 else None, **ffn_tiles)
        ya = _gmlp(h, w_in, l, gmlp_ln_g[l], gmlp_ln_b[l], gmlp_w_s[l], gmlp_b_s[l], batch=batch, seq=seq, jb=seq // _CHUNK)
        proj = _hyproj(h, w_in, l, hy_conv_w[l], hy_conv_b[l], batch=batch, seq=seq)
        yb = _hyena(proj, kf_all, l, hy_skip[l], fwd_bf16, inv_bf16, batch=batch, seq=seq, ct=proj.shape[-1])
        xp = _ffn(xp, l, ffn2_pre_g, *ffn2_w, ffn2_post_g, mix=(ya, yb, w_out, mix_post_g),
                  relayout="interleave" if l == depth - 1 else None, **ffn_tiles)

    return xp.reshape(batch, seq, d)
```

```python
import functools
import math

import jax
import jax.numpy as jnp
import numpy as np
from jax import lax
from jax.experimental import pallas as pl
from jax.experimental.pallas import tpu as pltpu

_CHUNK = 128
_HEADS_A = 8
_HEAD_DIM_A = 64
_ORDER = 2
_SHORT_K = 3
_FILTER_BANDS = 16
_FILTER_WIDTH = 64
_DECAY_TARGET = 1e-2
_MAX_DECAY = math.log(_DECAY_TARGET) / 0.3
_MIN_DECAY = math.log(_DECAY_TARGET) / 1.5
_HALF_STEP = 0.5
_RMS_EPS = 1e-6
_LN_EPS = 1e-5

_V7X_LANES = 128
_V7X_F32_SUBLANES = 8
_V7X_BF16_SUBLANES = 16
_V7X_MXU_DIM = 256
_V7X_VMEM_LIMIT_BYTES = 62 * 1024 * 1024

_NP = 8
_HALF = 272
_SQRT_HALF = 0.7071067811865476
_STAGE_TWO_DTYPE = jnp.bfloat16


def _cparams(semantics):
    return pltpu.CompilerParams(dimension_semantics=semantics, vmem_limit_bytes=_V7X_VMEM_LIMIT_BYTES)


def _rms_norm(x, g):
    return x * lax.rsqrt(jnp.mean(x * x, axis=-1, keepdims=True) + _RMS_EPS) * g


def _deinterleave_rows(val, slab_scr):
    rows, d = val.shape
    n_slabs = d // _V7X_LANES
    for k in range(n_slabs):
        slab_scr[k] = val[:, k * _V7X_LANES:(k + 1) * _V7X_LANES]
    return jnp.concatenate(
        [jnp.concatenate([slab_scr[k, pl.ds(n2, rows // _NP, stride=_NP), :] for k in range(n_slabs)], axis=1)
         for n2 in range(_NP)], axis=0)


def _interleave_rows(val, slab_scr):
    rows, d = val.shape
    per = rows // _NP
    n_slabs = d // _V7X_LANES
    for n2 in range(_NP):
        for k in range(n_slabs):
            slab_scr[k, pl.ds(n2, per, stride=_NP), :] = val[n2 * per:(n2 + 1) * per, k * _V7X_LANES:(k + 1) * _V7X_LANES]
    return jnp.concatenate([slab_scr[k] for k in range(n_slabs)], axis=1)


def _ffn_kernel(*refs, layer, chunk, mix_in, norm_out, relayout):
    refs = list(refs)
    x_ref = refs.pop(0)
    if mix_in:
        ya_ref, yb_ref, wa_ref, wb_ref, gmix_ref = refs[:5]
        refs = refs[5:]
    gpre_ref, wg_hbm, wu_hbm, wd_hbm, gpost_ref = refs[:5]
    refs = refs[5:]
    gnext_ref = refs.pop(0) if norm_out else None
    o_ref = refs.pop(0)
    hn_ref = refs.pop(0) if norm_out else None
    wg_scr, wu_scr, wd_scr, w_sem = refs[:4]
    slab_scr = refs[4] if relayout else None
    blocked_in = relayout == "interleave"
    blocked_out = relayout == "deinterleave"
    load = (lambda r: r[0].reshape(-1, r.shape[-1])) if blocked_in else (lambda r: r[...])
    dff = wg_scr.shape[1]
    bounds = [(c0, min(c0 + chunk, dff)) for c0 in range(0, dff, chunk)]

    def weight_copies(c):
        cols = pl.ds(bounds[c][0], bounds[c][1] - bounds[c][0])
        return (pltpu.make_async_copy(wg_hbm.at[layer, :, cols], wg_scr.at[:, cols], w_sem.at[c, 0]),
                pltpu.make_async_copy(wu_hbm.at[layer, :, cols], wu_scr.at[:, cols], w_sem.at[c, 1]),
                pltpu.make_async_copy(wd_hbm.at[layer, cols, :], wd_scr.at[cols, :], w_sem.at[c, 2]))

    def tile(first_step):
        if first_step:
            for c in range(len(bounds)):
                for copy in weight_copies(c):
                    copy.start()
        x = load(x_ref)
        if mix_in:
            y = jnp.dot(load(ya_ref), wa_ref[...], preferred_element_type=jnp.float32)
            y = y + jnp.dot(load(yb_ref), wb_ref[...], preferred_element_type=jnp.float32)
            x = x + _rms_norm(y, gmix_ref[...])
        h = _rms_norm(x, gpre_ref[...])
        acc = jnp.zeros(x.shape, jnp.float32)
        for c, (c0, c1) in enumerate(bounds):
            if first_step:
                for copy in weight_copies(c):
                    copy.wait()
            g = jnp.dot(h, wg_scr[:, c0:c1], preferred_element_type=jnp.float32)
            u = jnp.dot(h, wu_scr[:, c0:c1], preferred_element_type=jnp.float32)
            a = g * jax.nn.sigmoid(g) * u
            acc = acc + jnp.dot(a, wd_scr[c0:c1, :], preferred_element_type=jnp.float32)
        out = x + _rms_norm(acc, gpost_ref[...])
        if relayout == "deinterleave":
            out = _deinterleave_rows(out, slab_scr)
        elif relayout == "interleave":
            out = _interleave_rows(out, slab_scr)
        if norm_out:
            hn = _rms_norm(out, gnext_ref[...]).astype(jnp.bfloat16)
            if blocked_out:
                hn_ref[0] = hn.reshape(hn_ref.shape[1:])
            else:
                hn_ref[...] = hn
        if blocked_out:
            o_ref[0] = out.reshape(o_ref.shape[1:])
        else:
            o_ref[...] = out

    first = pl.program_id(0) == 0
    pl.when(first)(functools.partial(tile, True))
    pl.when(jnp.logical_not(first))(functools.partial(tile, False))


def _ffn(x, layer, g_pre, w_gate, w_up, w_down, g_post, *, batch, tm, chunk, mix=None, g_next=None, relayout=None):
    m, d = x.shape
    dff = w_gate.shape[2]
    seq = m // batch
    steps_per_seq = seq // tm
    assert dff % _V7X_MXU_DIM == 0 and chunk % _V7X_MXU_DIM == 0 and seq % tm == 0 and tm % (_NP * _V7X_BF16_SUBLANES) == 0
    row = lambda a: a[layer].reshape(1, d)
    flat_tile = lambda width: pl.BlockSpec((tm, width), lambda i: (i, 0))
    blocked_tile = lambda width: pl.BlockSpec((1, _NP, tm // _NP, width),
                                              lambda i: (i // steps_per_seq, 0, i % steps_per_seq, 0))
    blocked = lambda a: a.reshape(batch, _NP, seq // _NP, a.shape[-1])
    if relayout == "interleave":
        in_tile, put = blocked_tile, blocked
    else:
        in_tile, put = flat_tile, lambda a: a
    vec = pl.BlockSpec((1, d), lambda i: (0, 0))
    resident = lambda r, c, blk=0: pl.BlockSpec((None, r, c), lambda i: (layer, blk, 0), pipeline_mode=pl.Buffered(1))
    args, specs = [put(x)], [in_tile(d)]
    if mix is not None:
        ya, yb, w_out, g_mix = mix
        da = ya.shape[1]
        assert yb.shape[1] == da and w_out.shape[1:] == (2 * da, d)
        args += [put(ya), put(yb), w_out, w_out, row(g_mix)]
        specs += [in_tile(da), in_tile(da), resident(da, d, 0), resident(da, d, 1), vec]
    args += [row(g_pre), w_gate, w_up, w_down, _HALF_STEP * row(g_post)]
    in_hbm = pl.BlockSpec(memory_space=pl.ANY)
    specs += [vec, in_hbm, in_hbm, in_hbm, vec]
    if relayout == "deinterleave":
        out_tile, out_dims = blocked_tile(d), (batch, _NP, seq // _NP, d)
    else:
        out_tile, out_dims = flat_tile(d), (m, d)
    out_shape = [jax.ShapeDtypeStruct(out_dims, jnp.float32)]
    out_specs = [out_tile]
    if g_next is not None:
        args.append(row(g_next))
        specs.append(vec)
        out_shape.append(jax.ShapeDtypeStruct(out_dims, jnp.bfloat16))
        out_specs.append(out_tile)
    n_chunks = -(-dff // chunk)
    scratch = [pltpu.VMEM((d, dff), w_gate.dtype), pltpu.VMEM((d, dff), w_up.dtype), pltpu.VMEM((dff, d), w_down.dtype),
               pltpu.SemaphoreType.DMA((n_chunks, 3))]
    if relayout:
        scratch.append(pltpu.VMEM((d // _V7X_LANES, tm, _V7X_LANES), jnp.float32))
    outs = pl.pallas_call(
        functools.partial(_ffn_kernel, layer=layer, chunk=chunk, mix_in=mix is not None,
                          norm_out=g_next is not None, relayout=relayout),
        grid=(m // tm,),
        in_specs=specs,
        out_specs=out_specs,
        out_shape=out_shape,
        scratch_shapes=scratch,
        compiler_params=_cparams(("arbitrary",)),
        name="ffn",
    )(*args)
    outs = [o.reshape(m, d) for o in outs]
    return outs if g_next is not None else outs[0]


def _gmlp_kernel(h_ref, w_ref, lng_ref, lnb_ref, ws_ref, bs_ref, o_ref, *, jb, da):
    rows = _CHUNK // _NP
    d = h_ref.shape[-1]
    h = h_ref[...].reshape(_NP * jb * rows, d)
    gelu = lambda p: 0.5 * p * (1.0 + lax.erf(p * _SQRT_HALF))
    u = gelu(jnp.dot(h, w_ref[:, :da], preferred_element_type=jnp.float32))
    v = gelu(jnp.dot(h, w_ref[:, da:], preferred_element_type=jnp.float32))
    mu = jnp.mean(v, axis=-1, keepdims=True)
    vc = v - mu
    var = jnp.mean(vc * vc, axis=-1, keepdims=True)
    vn = (vc * lax.rsqrt(var + _LN_EPS) * lng_ref[...] + lnb_ref[...]).astype(jnp.bfloat16)
    lane = lax.broadcasted_iota(jnp.int32, (_CHUNK, _V7X_LANES), 1)
    first_head = lane < _HEAD_DIM_A
    n_pairs = da // _V7X_LANES
    for j in range(jb):
        starts = [(n2 * jb + j) * rows for n2 in range(_NP)]
        vchunk = jnp.concatenate([vn[s:s + rows] for s in starts], axis=0)
        mixed = []
        for q in range(n_pairs):
            r = jnp.dot(ws_ref[q], vchunk[:, q * _V7X_LANES:(q + 1) * _V7X_LANES],
                        preferred_element_type=jnp.float32)
            mixed.append(jnp.where(first_head, r[:_CHUNK], r[_CHUNK:]))
        mixed = jnp.concatenate(mixed, axis=1) + bs_ref[...]
        for n2 in range(_NP):
            s = starts[n2]
            o_ref[n2, j * rows:(j + 1) * rows] = (u[s:s + rows] * mixed[n2 * rows:(n2 + 1) * rows]).astype(o_ref.dtype)


def _gmlp(h, w_in, layer, ln_g, ln_b, w_s, b_s, *, batch, seq, jb):
    m, d = h.shape
    da = ln_g.shape[0]
    rows = _CHUNK // _NP
    n_chunks = seq // _CHUNK
    assert rows == _V7X_BF16_SUBLANES
    n1 = seq // _NP
    h3 = h.reshape(batch * _NP, n1, d)
    tau = np.arange(_CHUNK)
    perm = np.zeros((_CHUNK, _CHUNK), np.float32)
    perm[(tau % _NP) * rows + tau // _NP, tau] = 1.0
    exact = lax.Precision.HIGHEST
    ws_p = jnp.einsum("pm,hmc,qc->hpq", perm, w_s, perm, precision=exact)
    ws_pairs = ws_p.reshape(_HEADS_A // 2, 2 * _CHUNK, _CHUNK).astype(jnp.bfloat16)
    bs_p = jnp.einsum("hm,pm->hp", b_s, perm, precision=exact)
    bs_full = jnp.repeat(bs_p.T, _HEAD_DIM_A, axis=1)
    out = pl.pallas_call(
        functools.partial(_gmlp_kernel, jb=jb, da=da),
        grid=(batch, n_chunks // jb),
        in_specs=[
            pl.BlockSpec((_NP, jb * rows, d), lambda b, j: (b, j, 0)),
            pl.BlockSpec((None, d, 2 * da), lambda b, j: (layer, 0, 0)),
            pl.BlockSpec((1, da), lambda b, j: (0, 0)),
            pl.BlockSpec((1, da), lambda b, j: (0, 0)),
            pl.BlockSpec((_HEADS_A // 2, 2 * _CHUNK, _CHUNK), lambda b, j: (0, 0, 0)),
            pl.BlockSpec((_CHUNK, da), lambda b, j: (0, 0)),
        ],
        out_specs=pl.BlockSpec((_NP, jb * rows, da), lambda b, j: (b, j, 0)),
        out_shape=jax.ShapeDtypeStruct((batch * _NP, n1, da), jnp.bfloat16),
        compiler_params=_cparams(("arbitrary", "arbitrary")),
        name="gmlp",
    )(h3, w_in, ln_g.reshape(1, da), ln_b.reshape(1, da), ws_pairs, bs_full)
    return out.reshape(m, da)


def _hyproj_kernel(h_ref, *refs, n1):
    n_proj = _ORDER + 1
    w_refs, (cw_ref, cb_ref, o_ref) = refs[:n_proj], refs[n_proj:]
    c = o_ref.shape[-1]
    h = h_ref[0]
    sub = cw_ref.shape[1]
    tiled = lambda a: a.reshape(n1 // sub, sub, c)
    for j in range(n_proj):
        p = jnp.dot(h, w_refs[j][...], preferred_element_type=jnp.float32)
        blocks = [p[k * n1:(k + 1) * n1] for k in range(_NP)]
        row = lax.broadcasted_iota(jnp.int32, blocks[0].shape, 0)
        before_first = jnp.where(row == 0, 0.0, pltpu.roll(blocks[_NP - 1], 1, axis=0))
        after_last = jnp.where(row == n1 - 1, 0.0, pltpu.roll(blocks[0], n1 - 1, axis=0))
        cols = slice(j * c, (j + 1) * c)
        w0, w1, w2 = cw_ref[0, :, cols], cw_ref[1, :, cols], cw_ref[2, :, cols]
        bias = cb_ref[:, cols]
        for k in range(_NP):
            prev = blocks[k - 1] if k > 0 else before_first
            nxt = blocks[k + 1] if k < _NP - 1 else after_last
            y = ((bias + tiled(prev) * w0) + tiled(blocks[k]) * w1) + tiled(nxt) * w2
            o_ref[j, k * n1:(k + 1) * n1] = y.reshape(n1, c).astype(o_ref.dtype)


def _hyproj(h, w_in, layer, conv_w, conv_b, *, batch, seq):
    m, d = h.shape
    n_proj = _ORDER + 1
    c = conv_w.shape[1] // n_proj
    first = (w_in.shape[2] - n_proj * c) // c
    w_spec = lambda j: pl.BlockSpec((None, d, c), lambda b: (layer, 0, first + j), pipeline_mode=pl.Buffered(1))
    sub = _V7X_F32_SUBLANES
    taps = jnp.broadcast_to(conv_w[:, None, :], (_SHORT_K, sub, n_proj * c))
    bias = jnp.broadcast_to(conv_b[None, :], (sub, n_proj * c))
    return pl.pallas_call(
        functools.partial(_hyproj_kernel, n1=seq // _NP),
        grid=(batch,),
        in_specs=[pl.BlockSpec((1, seq, d), lambda b: (b, 0, 0))] + [w_spec(j) for j in range(n_proj)] + [
            pl.BlockSpec((_SHORT_K, sub, n_proj * c), lambda b: (0, 0, 0)),
            pl.BlockSpec((sub, n_proj * c), lambda b: (0, 0)),
        ],
        out_specs=pl.BlockSpec((n_proj, seq, c), lambda b: (0, b, 0)),
        out_shape=jax.ShapeDtypeStruct((n_proj, m, c), jnp.bfloat16),
        compiler_params=_cparams(("arbitrary",)),
        name="hyproj",
    )(h.reshape(batch, seq, d), *([w_in] * n_proj), taps, bias)


@functools.lru_cache(maxsize=None)
def _dft_tables(seq):
    n1 = seq // _NP
    n_fft = 2 * seq
    k1 = np.arange(_HALF)[None, :, None]
    t = _NP * np.arange(n1)[None, None, :] + np.arange(_NP)[:, None, None]
    theta = ((k1 * t) % n_fft) * (2.0 * math.pi / n_fft)
    valid = k1 <= n1
    cos = np.where(valid, np.cos(theta), 0.0)
    sin = np.where(valid, np.sin(theta), 0.0)
    fwd = np.concatenate([cos, -sin], axis=1)
    weight = np.where(k1 == 0, 1.0, 2.0) / n_fft
    inv_re = (weight * cos)[:, :n1]
    inv_im = (-weight * sin)[:, :n1]
    inv_im[:, 0, :] = ((-1.0) ** np.arange(n1))[None, :] / n_fft
    inv = np.concatenate([inv_re, inv_im], axis=1).transpose(0, 2, 1)
    return fwd.astype(np.float32), np.ascontiguousarray(inv).astype(np.float32)


def _fft4(cr, ci):
    d0r, d0i = cr[0] + cr[2], ci[0] + ci[2]
    d1r, d1i = cr[1] + cr[3], ci[1] + ci[3]
    d2r, d2i = cr[0] - cr[2], ci[0] - ci[2]
    er, ei = cr[1] - cr[3], ci[1] - ci[3]
    d3r, d3i = ei, -er
    return [(d0r + d1r, d0i + d1i), (d2r + d3r, d2i + d3i), (d0r - d1r, d0i - d1i), (d2r - d3r, d2i - d3i)]


def _fft8(re, im):
    br, bi = [None] * 8, [None] * 8
    for j in range(4):
        br[j], bi[j] = re[j] + re[j + 4], im[j] + im[j + 4]
        dr, di = re[j] - re[j + 4], im[j] - im[j + 4]
        if j == 0:
            br[4], bi[4] = dr, di
        elif j == 1:
            br[5], bi[5] = (dr + di) * _SQRT_HALF, (di - dr) * _SQRT_HALF
        elif j == 2:
            br[6], bi[6] = di, -dr
        else:
            br[7], bi[7] = (di - dr) * _SQRT_HALF, -(dr + di) * _SQRT_HALF
    even = _fft4(br[:4], bi[:4])
    odd = _fft4(br[4:], bi[4:])
    out = [None] * 8
    for k in range(4):
        out[2 * k], out[2 * k + 1] = even[k], odd[k]
    return [o[0] for o in out], [o[1] for o in out]


def _ifft8(re, im):
    o_im, o_re = _fft8(im, re)
    return o_re, o_im


def _filter_kernel(w1_ref, b1_ref, w2_ref, b2_ref, w3_ref, b3_ref, fr_ref, wof_ref, wob_ref, dl_ref,
                   fwd_ref, o_ref, h_scr, af_scr, ab_scr, *, seq):
    n1 = seq // _NP
    fw = _FILTER_WIDTH
    hp = lax.Precision.HIGHEST

    @pl.when(pl.program_id(1) == 0)
    def _():
        nb = _FILTER_BANDS
        r = lax.broadcasted_iota(jnp.int32, (nb, seq), 1)
        pos = (_NP * (r % n1) + r // n1).astype(jnp.float32)
        band_idx = lax.broadcasted_iota(jnp.int32, (nb, seq), 0).astype(jnp.float32)
        band = 1e-4 + band_idx * ((nb - 1 - 1e-4) / (nb - 1))
        ang = (2.0 * math.pi / seq) * pos * band
        first_row = lax.broadcasted_iota(jnp.int32, (fw - 2 * nb, seq), 0) == 0
        tail = jnp.where(first_row, jnp.concatenate([pos] * ((fw - 2 * nb) // nb), axis=0) * (1.0 / (seq - 1)), 0.0)
        z = jnp.concatenate([jnp.cos(ang), -jnp.sin(ang), tail], axis=0)
        freq = fr_ref[0]
        h = jnp.sin(freq * (jnp.dot(w1_ref[0], z, precision=hp, preferred_element_type=jnp.float32) + b1_ref[0]))
        h = jnp.sin(freq * (jnp.dot(w2_ref[0], h, precision=hp, preferred_element_type=jnp.float32) + b2_ref[0]))
        h = jnp.sin(freq * (jnp.dot(w3_ref[0], h, precision=hp, preferred_element_type=jnp.float32) + b3_ref[0]))
        h_scr[...] = h.T

    r = lax.broadcasted_iota(jnp.int32, (seq, 1), 0)
    pos = (_NP * (r % n1) + r // n1).astype(jnp.float32)
    h = h_scr[...].astype(jnp.bfloat16)
    decay = jnp.exp(-(pos * (1.0 / (seq - 1))) * dl_ref[...])
    hf = jnp.dot(h, wof_ref[0].astype(jnp.bfloat16), preferred_element_type=jnp.float32) * decay
    hb = jnp.dot(h, wob_ref[0].astype(jnp.bfloat16), preferred_element_type=jnp.float32) * decay
    hb = jnp.where(pos == 0.0, 0.0, hb)
    hf = hf.astype(jnp.bfloat16)
    hb = hb.astype(jnp.bfloat16)
    for n2 in range(_NP):
        af_scr[n2] = jnp.dot(fwd_ref[n2], hf[n2 * n1:(n2 + 1) * n1], preferred_element_type=jnp.float32)
        ab_scr[n2] = jnp.dot(fwd_ref[n2], hb[n2 * n1:(n2 + 1) * n1], preferred_element_type=jnp.float32)

    rc = _V7X_BF16_SUBLANES

    def body(i, carry):
        r0 = pl.multiple_of(i * rc, rc)
        back = [(_NP - n2) % _NP for n2 in range(_NP)]
        kr, ki = _fft8([af_scr[n2, pl.ds(r0, rc)] + ab_scr[back[n2], pl.ds(r0, rc)] for n2 in range(_NP)],
                       [af_scr[n2, pl.ds(_HALF + r0, rc)] - ab_scr[back[n2], pl.ds(_HALF + r0, rc)] for n2 in range(_NP)])
        for k2 in range(_NP):
            o_ref[0, 0, k2, 0, pl.ds(r0, rc)] = kr[k2].astype(o_ref.dtype)
            o_ref[0, 0, k2, 1, pl.ds(r0, rc)] = ki[k2].astype(o_ref.dtype)
        return carry

    lax.fori_loop(0, _HALF // rc, body, 0)


def _filter_spectra(w1, b1, w2, b2, w3, b3, freq, w_out, fwd_tab, *, seq, ct):
    n_layers = w1.shape[0]
    c = w_out.shape[2] // (2 * _ORDER)
    n_ct = c // ct
    fw = _FILTER_WIDTH
    w1r = jnp.concatenate([w1[:, 1:], w1[:, :1]], axis=1)
    w1t = jnp.pad(w1r, ((0, 0), (0, fw - w1.shape[1]), (0, 0))).transpose(0, 2, 1)
    col = lambda a: a.reshape(n_layers, fw, 1)
    deltas = jnp.abs(jnp.linspace(_MIN_DECAY, _MAX_DECAY, c, dtype=jnp.float32)).reshape(1, c)
    lmap3 = lambda l, s: (l, 0, 0)
    wo_spec = lambda direction: pl.BlockSpec(
        (1, fw, ct), lambda l, s: (l, 0, (2 * (s // n_ct) + direction) * n_ct + s % n_ct))
    return pl.pallas_call(
        functools.partial(_filter_kernel, seq=seq),
        grid=(n_layers, _ORDER * n_ct),
        in_specs=[
            pl.BlockSpec((1, fw, fw), lmap3), pl.BlockSpec((1, fw, 1), lmap3),
            pl.BlockSpec((1, fw, fw), lmap3), pl.BlockSpec((1, fw, 1), lmap3),
            pl.BlockSpec((1, fw, fw), lmap3), pl.BlockSpec((1, fw, 1), lmap3),
            pl.BlockSpec((1, fw, 1), lmap3),
            wo_spec(0), wo_spec(1),
            pl.BlockSpec((1, ct), lambda l, s: (0, s % n_ct)),
            pl.BlockSpec((_NP, 2 * _HALF, seq // _NP), lambda l, s: (0, 0, 0)),
        ],
        out_specs=pl.BlockSpec((1, 1, _NP, 2, _HALF, ct), lambda l, s: (l, s // n_ct, 0, 0, 0, s % n_ct)),
        out_shape=jax.ShapeDtypeStruct((n_layers, _ORDER, _NP, 2, _HALF, c), _STAGE_TWO_DTYPE),
        scratch_shapes=[pltpu.VMEM((seq, fw), jnp.float32),
                        pltpu.VMEM((_NP, 2 * _HALF, ct), jnp.float32),
                        pltpu.VMEM((_NP, 2 * _HALF, ct), jnp.float32)],
        compiler_params=_cparams(("arbitrary", "arbitrary")),
        name="filter_spectra",
    )(w1t, col(b1), w2.transpose(0, 2, 1), col(b2), w3.transpose(0, 2, 1), col(b3), col(freq),
      w_out, w_out, deltas, fwd_tab)


def _hyena_kernel(v_ref, x1_ref, x2_ref, kf_ref, skip_ref, fwd_ref, inv_ref, o_ref, a_scr, c_scr, z_scr, *, seq):
    n1 = seq // _NP
    rc = _V7X_BF16_SUBLANES
    ct = o_ref.shape[-1]

    def conv_order(order, z_in_ref, gate_ref, z_out_ref):
        for n2 in range(_NP):
            zb = z_in_ref[0, n2 * n1:(n2 + 1) * n1].astype(jnp.bfloat16)
            a_scr[n2] = jnp.dot(fwd_ref[n2], zb, preferred_element_type=jnp.float32).astype(a_scr.dtype)

        def spectrum_rows(r0, ls):
            xr, xi = _fft8([a_scr[n2, pl.ds(r0, rc), ls] for n2 in range(_NP)],
                           [a_scr[n2, pl.ds(_HALF + r0, rc), ls] for n2 in range(_NP)])
            yr, yi = [], []
            for k2 in range(_NP):
                kr = kf_ref[order, k2, 0, pl.ds(r0, rc), ls]
                ki = kf_ref[order, k2, 1, pl.ds(r0, rc), ls]
                yr.append(xr[k2] * kr - xi[k2] * ki)
                yi.append(xr[k2] * ki + xi[k2] * kr)
            return _ifft8(yr, yi)

        def body(i, carry):
            r0 = pl.multiple_of(i * rc, rc)
            for l0 in range(0, ct, _V7X_LANES):
                ls = slice(l0, l0 + _V7X_LANES)
                cr, ci = spectrum_rows(r0, ls)
                for t2 in range(_NP):
                    c_scr[t2, pl.ds(r0, rc), ls] = cr[t2].astype(jnp.bfloat16)
                    c_scr[t2, pl.ds(n1 + r0, rc), ls] = ci[t2].astype(jnp.bfloat16)
            return carry

        first_row = lax.broadcasted_iota(jnp.int32, (rc, _V7X_LANES), 0) == 0
        for l0 in range(0, ct, _V7X_LANES):
            ls = slice(l0, l0 + _V7X_LANES)
            nr, ni = spectrum_rows(n1, ls)
            cr, ci = spectrum_rows(0, ls)
            for t2 in range(_NP):
                ang = math.pi * t2 / _NP
                nyq = nr[t2] * math.cos(ang) - ni[t2] * math.sin(ang)
                c_scr[t2, 0:rc, ls] = cr[t2].astype(jnp.bfloat16)
                c_scr[t2, n1:n1 + rc, ls] = jnp.where(first_row, nyq, ci[t2]).astype(jnp.bfloat16)
        lax.fori_loop(1, n1 // rc, body, 0)

        skip = skip_ref[order:order + 1]
        for t2 in range(_NP):
            rows = slice(t2 * n1, (t2 + 1) * n1)
            y = jnp.dot(inv_ref[t2], c_scr[t2], preferred_element_type=jnp.float32)
            z = z_in_ref[0, rows].astype(jnp.float32)
            gate = gate_ref[0, rows].astype(jnp.float32)
            z_out_ref[0, rows] = (gate * (y + z * skip)).astype(z_out_ref.dtype)

    conv_order(0, v_ref, x1_ref, z_scr)
    conv_order(1, z_scr, x2_ref, o_ref)


def _hyena(proj, kf_all, layer, skip, fwd_tab, inv_tab, *, batch, seq, ct):
    n_proj, m, c = proj.shape
    n1 = seq // _NP
    proj4 = proj.reshape(n_proj, batch, seq, c)
    pspec = lambda j: pl.BlockSpec((None, 1, seq, ct), lambda t, b, j=j: (j, b, 0, t))
    resident = dict(pipeline_mode=pl.Buffered(1))
    out = pl.pallas_call(
        functools.partial(_hyena_kernel, seq=seq),
        grid=(c // ct, batch),
        in_specs=[
            pspec(0), pspec(1), pspec(2),
            pl.BlockSpec((None, _ORDER, _NP, 2, _HALF, ct), lambda t, b: (layer, 0, 0, 0, 0, t), **resident),
            pl.BlockSpec((_ORDER, ct), lambda t, b: (0, t)),
            pl.BlockSpec((_NP, 2 * _HALF, n1), lambda t, b: (0, 0, 0), **resident),
            pl.BlockSpec((_NP, n1, 2 * n1), lambda t, b: (0, 0, 0), **resident),
        ],
        out_specs=pl.BlockSpec((1, seq, ct), lambda t, b: (b, 0, t)),
        out_shape=jax.ShapeDtypeStruct((batch, seq, c), jnp.bfloat16),
        scratch_shapes=[pltpu.VMEM((_NP, 2 * _HALF, ct), _STAGE_TWO_DTYPE),
                        pltpu.VMEM((_NP, 2 * n1, ct), jnp.bfloat16),
                        pltpu.VMEM((1, seq, ct), jnp.float32)],
        compiler_params=_cparams(("arbitrary", "arbitrary")),
        name="hyena",
    )(proj4, proj4, proj4, kf_all, skip, fwd_tab, inv_tab)
    return out.reshape(m, c)


def kernel(x, ffn1_pre_g, ffn1_w_gate, ffn1_w_up, ffn1_w_down, ffn1_post_g, mix_pre_g, mix_w_in, gmlp_ln_g, gmlp_ln_b, gmlp_w_s, gmlp_b_s, hy_conv_w, hy_conv_b, hy_filt_w1, hy_filt_b1, hy_filt_w2, hy_filt_b2, hy_filt_w3, hy_filt_b3, hy_filt_freq, hy_filt_w_out, hy_skip, mix_w_out, mix_post_g, ffn2_pre_g, ffn2_w_gate, ffn2_w_up, ffn2_w_down, ffn2_post_g):
    batch, seq, d = x.shape
    depth = ffn1_pre_g.shape[0]
    da = gmlp_ln_g.shape[1]
    assert seq % (_NP * _CHUNK) == 0 and seq // _NP + 1 <= _HALF
    m = batch * seq
    ffn_tiles = dict(batch=batch, tm=512, chunk=_V7X_MXU_DIM)
    ct = _V7X_MXU_DIM
    xp = x.reshape(m, d)

    fwd_np, inv_np = _dft_tables(seq)
    fwd_bf16 = jnp.asarray(fwd_np).astype(jnp.bfloat16)
    inv_bf16 = jnp.asarray(inv_np).astype(jnp.bfloat16)
    kf_all = _filter_spectra(hy_filt_w1, hy_filt_b1, hy_filt_w2, hy_filt_b2, hy_filt_w3, hy_filt_b3,
                             hy_filt_freq, hy_filt_w_out, fwd_bf16, seq=seq, ct=ct)

    ffn1_w = (ffn1_w_gate, ffn1_w_up, ffn1_w_down)
    ffn2_w = (ffn2_w_gate, ffn2_w_up, ffn2_w_down)
    w_in = mix_w_in
    w_out = mix_w_out.astype(jnp.bfloat16)

    for l in range(depth):
        xp, h = _ffn(xp, l, ffn1_pre_g, *ffn1_w, ffn1_post_g, g_next=mix_pre_g,
                     relayout="deinterleave" if l == 0 else None, **ffn_tiles)
        ya = _gmlp(h, w_in, l, gmlp_ln_g[l], gmlp_ln_b[l], gmlp_w_s[l], gmlp_b_s[l], batch=batch, seq=seq, jb=seq // _CHUNK)
        proj = _hyproj(h, w_in, l, hy_conv_w[l], hy_conv_b[l], batch=batch, seq=seq)
        yb = _hyena(proj, kf_all, l, hy_skip[l], fwd_bf16, inv_bf16, batch=batch, seq=seq, ct=proj.shape[-1])
        xp = _ffn(xp, l, ffn2_pre_g, *ffn2_w, ffn2_post_g, mix=(ya, yb, w_out, mix_post_g),
                  relayout="interleave" if l == depth - 1 else None, **ffn_tiles)

    return xp.reshape(batch, seq, d)
```

```python
import functools
import math

import jax
import jax.numpy as jnp
import numpy as np
from jax import lax
from jax.experimental import pallas as pl
from jax.experimental.pallas import tpu as pltpu

_CHUNK = 128
_HEADS_A = 8
_HEAD_DIM_A = 64
_ORDER = 2
_SHORT_K = 3
_FILTER_BANDS = 16
_FILTER_WIDTH = 64
_DECAY_TARGET = 1e-2
_MAX_DECAY = math.log(_DECAY_TARGET) / 0.3
_MIN_DECAY = math.log(_DECAY_TARGET) / 1.5
_HALF_STEP = 0.5
_RMS_EPS = 1e-6
_LN_EPS = 1e-5

_V7X_LANES = 128
_V7X_F32_SUBLANES = 8
_V7X_BF16_SUBLANES = 16
_V7X_MXU_DIM = 256
_V7X_VMEM_LIMIT_BYTES = 62 * 1024 * 1024

_NP = 8
_HALF = 272
_SQRT_HALF = 0.7071067811865476
_STAGE_TWO_DTYPE = jnp.float32


def _cparams(semantics):
    return pltpu.CompilerParams(dimension_semantics=semantics, vmem_limit_bytes=_V7X_VMEM_LIMIT_BYTES)


def _rms_norm(x, g):
    return x * lax.rsqrt(jnp.mean(x * x, axis=-1, keepdims=True) + _RMS_EPS) * g


def _deinterleave_rows(val, slab_scr):
    rows, d = val.shape
    n_slabs = d // _V7X_LANES
    for k in range(n_slabs):
        slab_scr[k] = val[:, k * _V7X_LANES:(k + 1) * _V7X_LANES]
    return jnp.concatenate(
        [jnp.concatenate([slab_scr[k, pl.ds(n2, rows // _NP, stride=_NP), :] for k in range(n_slabs)], axis=1)
         for n2 in range(_NP)], axis=0)


def _interleave_rows(val, slab_scr):
    rows, d = val.shape
    per = rows // _NP
    n_slabs = d // _V7X_LANES
    for n2 in range(_NP):
        for k in range(n_slabs):
            slab_scr[k, pl.ds(n2, per, stride=_NP), :] = val[n2 * per:(n2 + 1) * per, k * _V7X_LANES:(k + 1) * _V7X_LANES]
    return jnp.concatenate([slab_scr[k] for k in range(n_slabs)], axis=1)


def _ffn_kernel(*refs, layer, chunk, mix_in, norm_out, relayout):
    refs = list(refs)
    x_ref = refs.pop(0)
    if mix_in:
        ya_ref, yb_ref, wa_ref, wb_ref, gmix_ref = refs[:5]
        refs = refs[5:]
    gpre_ref, wg_hbm, wu_hbm, wd_hbm, gpost_ref = refs[:5]
    refs = refs[5:]
    gnext_ref = refs.pop(0) if norm_out else None
    o_ref = refs.pop(0)
    hn_ref = refs.pop(0) if norm_out else None
    wg_scr, wu_scr, wd_scr, w_sem = refs[:4]
    slab_scr = refs[4] if relayout else None
    blocked_in = relayout == "interleave"
    blocked_out = relayout == "deinterleave"
    load = (lambda r: r[0].reshape(-1, r.shape[-1])) if blocked_in else (lambda r: r[...])
    dff = wg_scr.shape[1]
    bounds = [(c0, min(c0 + chunk, dff)) for c0 in range(0, dff, chunk)]

    def weight_copies(c):
        cols = pl.ds(bounds[c][0], bounds[c][1] - bounds[c][0])
        return (pltpu.make_async_copy(wg_hbm.at[layer, :, cols], wg_scr.at[:, cols], w_sem.at[c, 0]),
                pltpu.make_async_copy(wu_hbm.at[layer, :, cols], wu_scr.at[:, cols], w_sem.at[c, 1]),
                pltpu.make_async_copy(wd_hbm.at[layer, cols, :], wd_scr.at[cols, :], w_sem.at[c, 2]))

    def tile(first_step):
        if first_step:
            for c in range(len(bounds)):
                for copy in weight_copies(c):
                    copy.start()
        x = load(x_ref)
        if mix_in:
            y = jnp.dot(load(ya_ref), wa_ref[...], preferred_element_type=jnp.float32)
            y = y + jnp.dot(load(yb_ref), wb_ref[...], preferred_element_type=jnp.float32)
            x = x + _rms_norm(y, gmix_ref[...])
        h = _rms_norm(x, gpre_ref[...])
        acc = jnp.zeros(x.shape, jnp.float32)
        for c, (c0, c1) in enumerate(bounds):
            if first_step:
                for copy in weight_copies(c):
                    copy.wait()
            g = jnp.dot(h, wg_scr[:, c0:c1], preferred_element_type=jnp.float32)
            u = jnp.dot(h, wu_scr[:, c0:c1], preferred_element_type=jnp.float32)
            a = g * jax.nn.sigmoid(g) * u
            acc = acc + jnp.dot(a, wd_scr[c0:c1, :], preferred_element_type=jnp.float32)
        out = x + _rms_norm(acc, gpost_ref[...])
        if relayout == "deinterleave":
            out = _deinterleave_rows(out, slab_scr)
        elif relayout == "interleave":
            out = _interleave_rows(out, slab_scr)
        if norm_out:
            hn = _rms_norm(out, gnext_ref[...]).astype(jnp.bfloat16)
            if blocked_out:
                hn_ref[0] = hn.reshape(hn_ref.shape[1:])
            else:
                hn_ref[...] = hn
        if blocked_out:
            o_ref[0] = out.reshape(o_ref.shape[1:])
        else:
            o_ref[...] = out

    first = pl.program_id(0) == 0
    pl.when(first)(functools.partial(tile, True))
    pl.when(jnp.logical_not(first))(functools.partial(tile, False))


def _ffn(x, layer, g_pre, w_gate, w_up, w_down, g_post, *, batch, tm, chunk, mix=None, g_next=None, relayout=None):
    m, d = x.shape
    dff = w_gate.shape[2]
    seq = m // batch
    steps_per_seq = seq // tm
    assert dff % _V7X_MXU_DIM == 0 and chunk % _V7X_MXU_DIM == 0 and seq % tm == 0 and tm % (_NP * _V7X_BF16_SUBLANES) == 0
    row = lambda a: a[layer].reshape(1, d)
    flat_tile = lambda width: pl.BlockSpec((tm, width), lambda i: (i, 0))
    blocked_tile = lambda width: pl.BlockSpec((1, _NP, tm // _NP, width),
                                              lambda i: (i // steps_per_seq, 0, i % steps_per_seq, 0))
    blocked = lambda a: a.reshape(batch, _NP, seq // _NP, a.shape[-1])
    if relayout == "interleave":
        in_tile, put = blocked_tile, blocked
    else:
        in_tile, put = flat_tile, lambda a: a
    vec = pl.BlockSpec((1, d), lambda i: (0, 0))
    resident = lambda r, c, blk=0: pl.BlockSpec((None, r, c), lambda i: (layer, blk, 0), pipeline_mode=pl.Buffered(1))
    args, specs = [put(x)], [in_tile(d)]
    if mix is not None:
        ya, yb, w_out, g_mix = mix
        da = ya.shape[1]
        assert yb.shape[1] == da and w_out.shape[1:] == (2 * da, d)
        args += [put(ya), put(yb), w_out, w_out, row(g_mix)]
        specs += [in_tile(da), in_tile(da), resident(da, d, 0), resident(da, d, 1), vec]
    args += [row(g_pre), w_gate, w_up, w_down, _HALF_STEP * row(g_post)]
    in_hbm = pl.BlockSpec(memory_space=pl.ANY)
    specs += [vec, in_hbm, in_hbm, in_hbm, vec]
    if relayout == "deinterleave":
        out_tile, out_dims = blocked_tile(d), (batch, _NP, seq // _NP, d)
    else:
        out_tile, out_dims = flat_tile(d), (m, d)
    out_shape = [jax.ShapeDtypeStruct(out_dims, jnp.float32)]
    out_specs = [out_tile]
    if g_next is not None:
        args.append(row(g_next))
        specs.append(vec)
        out_shape.append(jax.ShapeDtypeStruct(out_dims, jnp.bfloat16))
        out_specs.append(out_tile)
    n_chunks = -(-dff // chunk)
    scratch = [pltpu.VMEM((d, dff), w_gate.dtype), pltpu.VMEM((d, dff), w_up.dtype), pltpu.VMEM((dff, d), w_down.dtype),
               pltpu.SemaphoreType.DMA((n_chunks, 3))]
    if relayout:
        scratch.append(pltpu.VMEM((d // _V7X_LANES, tm, _V7X_LANES), jnp.float32))
    outs = pl.pallas_call(
        functools.partial(_ffn_kernel, layer=layer, chunk=chunk, mix_in=mix is not None,
                          norm_out=g_next is not None, relayout=relayout),
        grid=(m // tm,),
        in_specs=specs,
        out_specs=out_specs,
        out_shape=out_shape,
        scratch_shapes=scratch,
        compiler_params=_cparams(("arbitrary",)),
        name="ffn",
    )(*args)
    outs = [o.reshape(m, d) for o in outs]
    return outs if g_next is not None else outs[0]


def _gmlp_kernel(h_ref, w_ref, lng_ref, lnb_ref, ws_ref, bs_ref, o_ref, *, jb, da):
    rows = _CHUNK // _NP
    d = h_ref.shape[-1]
    h = h_ref[...].reshape(_NP * jb * rows, d)
    gelu = lambda p: 0.5 * p * (1.0 + lax.erf(p * _SQRT_HALF))
    u = gelu(jnp.dot(h, w_ref[:, :da], preferred_element_type=jnp.float32))
    v = gelu(jnp.dot(h, w_ref[:, da:], preferred_element_type=jnp.float32))
    mu = jnp.mean(v, axis=-1, keepdims=True)
    vc = v - mu
    var = jnp.mean(vc * vc, axis=-1, keepdims=True)
    vn = (vc * lax.rsqrt(var + _LN_EPS) * lng_ref[...] + lnb_ref[...]).astype(jnp.bfloat16)
    lane = lax.broadcasted_iota(jnp.int32, (_CHUNK, _V7X_LANES), 1)
    first_head = lane < _HEAD_DIM_A
    n_pairs = da // _V7X_LANES
    for j in range(jb):
        starts = [(n2 * jb + j) * rows for n2 in range(_NP)]
        vchunk = jnp.concatenate([vn[s:s + rows] for s in starts], axis=0)
        mixed = []
        for q in range(n_pairs):
            r = jnp.dot(ws_ref[q], vchunk[:, q * _V7X_LANES:(q + 1) * _V7X_LANES],
                        preferred_element_type=jnp.float32)
            mixed.append(jnp.where(first_head, r[:_CHUNK], r[_CHUNK:]))
        mixed = jnp.concatenate(mixed, axis=1) + bs_ref[...]
        for n2 in range(_NP):
            s = starts[n2]
            o_ref[n2, j * rows:(j + 1) * rows] = (u[s:s + rows] * mixed[n2 * rows:(n2 + 1) * rows]).astype(o_ref.dtype)


def _gmlp(h, w_in, layer, ln_g, ln_b, w_s, b_s, *, batch, seq, jb):
    m, d = h.shape
    da = ln_g.shape[0]
    rows = _CHUNK // _NP
    n_chunks = seq // _CHUNK
    assert rows == _V7X_BF16_SUBLANES
    n1 = seq // _NP
    h3 = h.reshape(batch * _NP, n1, d)
    tau = np.arange(_CHUNK)
    perm = np.zeros((_CHUNK, _CHUNK), np.float32)
    perm[(tau % _NP) * rows + tau // _NP, tau] = 1.0
    exact = lax.Precision.HIGHEST
    ws_p = jnp.einsum("pm,hmc,qc->hpq", perm, w_s, perm, precision=exact)
    ws_pairs = ws_p.reshape(_HEADS_A // 2, 2 * _CHUNK, _CHUNK).astype(jnp.bfloat16)
    bs_p = jnp.einsum("hm,pm->hp", b_s, perm, precision=exact)
    bs_full = jnp.repeat(bs_p.T, _HEAD_DIM_A, axis=1)
    out = pl.pallas_call(
        functools.partial(_gmlp_kernel, jb=jb, da=da),
        grid=(batch, n_chunks // jb),
        in_specs=[
            pl.BlockSpec((_NP, jb * rows, d), lambda b, j: (b, j, 0)),
            pl.BlockSpec((None, d, 2 * da), lambda b, j: (layer, 0, 0)),
            pl.BlockSpec((1, da), lambda b, j: (0, 0)),
            pl.BlockSpec((1, da), lambda b, j: (0, 0)),
            pl.BlockSpec((_HEADS_A // 2, 2 * _CHUNK, _CHUNK), lambda b, j: (0, 0, 0)),
            pl.BlockSpec((_CHUNK, da), lambda b, j: (0, 0)),
        ],
        out_specs=pl.BlockSpec((_NP, jb * rows, da), lambda b, j: (b, j, 0)),
        out_shape=jax.ShapeDtypeStruct((batch * _NP, n1, da), jnp.bfloat16),
        compiler_params=_cparams(("arbitrary", "arbitrary")),
        name="gmlp",
    )(h3, w_in, ln_g.reshape(1, da), ln_b.reshape(1, da), ws_pairs, bs_full)
    return out.reshape(m, da)


def _hyproj_kernel(h_ref, *refs, n1):
    n_proj = _ORDER + 1
    w_refs, (cw_ref, cb_ref, o_ref) = refs[:n_proj], refs[n_proj:]
    c = o_ref.shape[-1]
    h = h_ref[0]
    sub = cw_ref.shape[1]
    tiled = lambda a: a.reshape(n1 // sub, sub, c)
    for j in range(n_proj):
        p = jnp.dot(h, w_refs[j][...], preferred_element_type=jnp.float32)
        blocks = [p[k * n1:(k + 1) * n1] for k in range(_NP)]
        row = lax.broadcasted_iota(jnp.int32, blocks[0].shape, 0)
        before_first = jnp.where(row == 0, 0.0, pltpu.roll(blocks[_NP - 1], 1, axis=0))
        after_last = jnp.where(row == n1 - 1, 0.0, pltpu.roll(blocks[0], n1 - 1, axis=0))
        cols = slice(j * c, (j + 1) * c)
        w0, w1, w2 = cw_ref[0, :, cols], cw_ref[1, :, cols], cw_ref[2, :, cols]
        bias = cb_ref[:, cols]
        for k in range(_NP):
            prev = blocks[k - 1] if k > 0 else before_first
            nxt = blocks[k + 1] if k < _NP - 1 else after_last
            y = ((bias + tiled(prev) * w0) + tiled(blocks[k]) * w1) + tiled(nxt) * w2
            o_ref[j, k * n1:(k + 1) * n1] = y.reshape(n1, c).astype(o_ref.dtype)


def _hyproj(h, w_in, layer, conv_w, conv_b, *, batch, seq):
    m, d = h.shape
    n_proj = _ORDER + 1
    c = conv_w.shape[1] // n_proj
    first = (w_in.shape[2] - n_proj * c) // c
    w_spec = lambda j: pl.BlockSpec((None, d, c), lambda b: (layer, 0, first + j), pipeline_mode=pl.Buffered(1))
    sub = _V7X_F32_SUBLANES
    taps = jnp.broadcast_to(conv_w[:, None, :], (_SHORT_K, sub, n_proj * c))
    bias = jnp.broadcast_to(conv_b[None, :], (sub, n_proj * c))
    return pl.pallas_call(
        functools.partial(_hyproj_kernel, n1=seq // _NP),
        grid=(batch,),
        in_specs=[pl.BlockSpec((1, seq, d), lambda b: (b, 0, 0))] + [w_spec(j) for j in range(n_proj)] + [
            pl.BlockSpec((_SHORT_K, sub, n_proj * c), lambda b: (0, 0, 0)),
            pl.BlockSpec((sub, n_proj * c), lambda b: (0, 0)),
        ],
        out_specs=pl.BlockSpec((n_proj, seq, c), lambda b: (0, b, 0)),
        out_shape=jax.ShapeDtypeStruct((n_proj, m, c), jnp.bfloat16),
        compiler_params=_cparams(("arbitrary",)),
        name="hyproj",
    )(h.reshape(batch, seq, d), *([w_in] * n_proj), taps, bias)


@functools.lru_cache(maxsize=None)
def _dft_tables(seq):
    n1 = seq // _NP
    n_fft = 2 * seq
    k1 = np.arange(_HALF)[None, :, None]
    t = _NP * np.arange(n1)[None, None, :] + np.arange(_NP)[:, None, None]
    theta = ((k1 * t) % n_fft) * (2.0 * math.pi / n_fft)
    valid = k1 <= n1
    cos = np.where(valid, np.cos(theta), 0.0)
    sin = np.where(valid, np.sin(theta), 0.0)
    fwd = np.concatenate([cos, -sin], axis=1)
    weight = np.where(k1 == 0, 1.0, 2.0) / n_fft
    inv_re = (weight * cos)[:, :n1]
    inv_im = (-weight * sin)[:, :n1]
    inv_im[:, 0, :] = ((-1.0) ** np.arange(n1))[None, :] / n_fft
    inv = np.concatenate([inv_re, inv_im], axis=1).transpose(0, 2, 1)
    return fwd.astype(np.float32), np.ascontiguousarray(inv).astype(np.float32)


def _fft4(cr, ci):
    d0r, d0i = cr[0] + cr[2], ci[0] + ci[2]
    d1r, d1i = cr[1] + cr[3], ci[1] + ci[3]
    d2r, d2i = cr[0] - cr[2], ci[0] - ci[2]
    er, ei = cr[1] - cr[3], ci[1] - ci[3]
    d3r, d3i = ei, -er
    return [(d0r + d1r, d0i + d1i), (d2r + d3r, d2i + d3i), (d0r - d1r, d0i - d1i), (d2r - d3r, d2i - d3i)]


def _fft8(re, im):
    br, bi = [None] * 8, [None] * 8
    for j in range(4):
        br[j], bi[j] = re[j] + re[j + 4], im[j] + im[j + 4]
        dr, di = re[j] - re[j + 4], im[j] - im[j + 4]
        if j == 0:
            br[4], bi[4] = dr, di
        elif j == 1:
            br[5], bi[5] = (dr + di) * _SQRT_HALF, (di - dr) * _SQRT_HALF
        elif j == 2:
            br[6], bi[6] = di, -dr
        else:
            br[7], bi[7] = (di - dr) * _SQRT_HALF, -(dr + di) * _SQRT_HALF
    even = _fft4(br[:4], bi[:4])
    odd = _fft4(br[4:], bi[4:])
    out = [None] * 8
    for k in range(4):
        out[2 * k], out[2 * k + 1] = even[k], odd[k]
    return [o[0] for o in out], [o[1] for o in out]


def _ifft8(re, im):
    o_im, o_re = _fft8(im, re)
    return o_re, o_im


def _filter_kernel(w1_ref, b1_ref, w2_ref, b2_ref, w3_ref, b3_ref, fr_ref, wof_ref, wob_ref, dl_ref,
                   fwd_ref, o_ref, h_scr, af_scr, ab_scr, *, seq):
    n1 = seq // _NP
    fw = _FILTER_WIDTH
    hp = lax.Precision.HIGHEST

    @pl.when(pl.program_id(1) == 0)
    def _():
        nb = _FILTER_BANDS
        r = lax.broadcasted_iota(jnp.int32, (nb, seq), 1)
        pos = (_NP * (r % n1) + r // n1).astype(jnp.float32)
        band_idx = lax.broadcasted_iota(jnp.int32, (nb, seq), 0).astype(jnp.float32)
        band = 1e-4 + band_idx * ((nb - 1 - 1e-4) / (nb - 1))
        ang = (2.0 * math.pi / seq) * pos * band
        first_row = lax.broadcasted_iota(jnp.int32, (fw - 2 * nb, seq), 0) == 0
        tail = jnp.where(first_row, jnp.concatenate([pos] * ((fw - 2 * nb) // nb), axis=0) * (1.0 / (seq - 1)), 0.0)
        z = jnp.concatenate([jnp.cos(ang), -jnp.sin(ang), tail], axis=0)
        freq = fr_ref[0]
        h = jnp.sin(freq * (jnp.dot(w1_ref[0], z, precision=hp, preferred_element_type=jnp.float32) + b1_ref[0]))
        h = jnp.sin(freq * (jnp.dot(w2_ref[0], h, precision=hp, preferred_element_type=jnp.float32) + b2_ref[0]))
        h = jnp.sin(freq * (jnp.dot(w3_ref[0], h, precision=hp, preferred_element_type=jnp.float32) + b3_ref[0]))
        h_scr[...] = h.T

    r = lax.broadcasted_iota(jnp.int32, (seq, 1), 0)
    pos = (_NP * (r % n1) + r // n1).astype(jnp.float32)
    h = h_scr[...].astype(jnp.bfloat16)
    decay = jnp.exp(-(pos * (1.0 / (seq - 1))) * dl_ref[...])
    hf = jnp.dot(h, wof_ref[0].astype(jnp.bfloat16), preferred_element_type=jnp.float32) * decay
    hb = jnp.dot(h, wob_ref[0].astype(jnp.bfloat16), preferred_element_type=jnp.float32) * decay
    hb = jnp.where(pos == 0.0, 0.0, hb)
    hf = hf.astype(jnp.bfloat16)
    hb = hb.astype(jnp.bfloat16)
    for n2 in range(_NP):
        af_scr[n2] = jnp.dot(fwd_ref[n2], hf[n2 * n1:(n2 + 1) * n1], preferred_element_type=jnp.float32)
        ab_scr[n2] = jnp.dot(fwd_ref[n2], hb[n2 * n1:(n2 + 1) * n1], preferred_element_type=jnp.float32)

    rc = _V7X_BF16_SUBLANES

    def body(i, carry):
        r0 = pl.multiple_of(i * rc, rc)
        back = [(_NP - n2) % _NP for n2 in range(_NP)]
        kr, ki = _fft8([af_scr[n2, pl.ds(r0, rc)] + ab_scr[back[n2], pl.ds(r0, rc)] for n2 in range(_NP)],
                       [af_scr[n2, pl.ds(_HALF + r0, rc)] - ab_scr[back[n2], pl.ds(_HALF + r0, rc)] for n2 in range(_NP)])
        for k2 in range(_NP):
            o_ref[0, 0, k2, 0, pl.ds(r0, rc)] = kr[k2].astype(o_ref.dtype)
            o_ref[0, 0, k2, 1, pl.ds(r0, rc)] = ki[k2].astype(o_ref.dtype)
        return carry

    lax.fori_loop(0, _HALF // rc, body, 0)


def _filter_spectra(w1, b1, w2, b2, w3, b3, freq, w_out, fwd_tab, *, seq, ct):
    n_layers = w1.shape[0]
    c = w_out.shape[2] // (2 * _ORDER)
    n_ct = c // ct
    fw = _FILTER_WIDTH
    w1r = jnp.concatenate([w1[:, 1:], w1[:, :1]], axis=1)
    w1t = jnp.pad(w1r, ((0, 0), (0, fw - w1.shape[1]), (0, 0))).transpose(0, 2, 1)
    col = lambda a: a.reshape(n_layers, fw, 1)
    deltas = jnp.abs(jnp.linspace(_MIN_DECAY, _MAX_DECAY, c, dtype=jnp.float32)).reshape(1, c)
    lmap3 = lambda l, s: (l, 0, 0)
    wo_spec = lambda direction: pl.BlockSpec(
        (1, fw, ct), lambda l, s: (l, 0, (2 * (s // n_ct) + direction) * n_ct + s % n_ct))
    return pl.pallas_call(
        functools.partial(_filter_kernel, seq=seq),
        grid=(n_layers, _ORDER * n_ct),
        in_specs=[
            pl.BlockSpec((1, fw, fw), lmap3), pl.BlockSpec((1, fw, 1), lmap3),
            pl.BlockSpec((1, fw, fw), lmap3), pl.BlockSpec((1, fw, 1), lmap3),
            pl.BlockSpec((1, fw, fw), lmap3), pl.BlockSpec((1, fw, 1), lmap3),
            pl.BlockSpec((1, fw, 1), lmap3),
            wo_spec(0), wo_spec(1),
            pl.BlockSpec((1, ct), lambda l, s: (0, s % n_ct)),
            pl.BlockSpec((_NP, 2 * _HALF, seq // _NP), lambda l, s: (0, 0, 0)),
        ],
        out_specs=pl.BlockSpec((1, 1, _NP, 2, _HALF, ct), lambda l, s: (l, s // n_ct, 0, 0, 0, s % n_ct)),
        out_shape=jax.ShapeDtypeStruct((n_layers, _ORDER, _NP, 2, _HALF, c), _STAGE_TWO_DTYPE),
        scratch_shapes=[pltpu.VMEM((seq, fw), jnp.float32),
                        pltpu.VMEM((_NP, 2 * _HALF, ct), jnp.float32),
                        pltpu.VMEM((_NP, 2 * _HALF, ct), jnp.float32)],
        compiler_params=_cparams(("arbitrary", "arbitrary")),
        name="filter_spectra",
    )(w1t, col(b1), w2.transpose(0, 2, 1), col(b2), w3.transpose(0, 2, 1), col(b3), col(freq),
      w_out, w_out, deltas, fwd_tab)


def _hyena_kernel(v_ref, x1_ref, x2_ref, kf_ref, skip_ref, fwd_ref, inv_ref, o_ref, a_scr, c_scr, z_scr, *, seq):
    n1 = seq // _NP
    rc = _V7X_BF16_SUBLANES
    ct = o_ref.shape[-1]

    def conv_order(order, z_in_ref, gate_ref, z_out_ref):
        for n2 in range(_NP):
            zb = z_in_ref[0, n2 * n1:(n2 + 1) * n1].astype(jnp.bfloat16)
            a_scr[n2] = jnp.dot(fwd_ref[n2], zb, preferred_element_type=jnp.float32).astype(a_scr.dtype)

        def spectrum_rows(r0, ls):
            xr, xi = _fft8([a_scr[n2, pl.ds(r0, rc), ls] for n2 in range(_NP)],
                           [a_scr[n2, pl.ds(_HALF + r0, rc), ls] for n2 in range(_NP)])
            yr, yi = [], []
            for k2 in range(_NP):
                kr = kf_ref[order, k2, 0, pl.ds(r0, rc), ls]
                ki = kf_ref[order, k2, 1, pl.ds(r0, rc), ls]
                yr.append(xr[k2] * kr - xi[k2] * ki)
                yi.append(xr[k2] * ki + xi[k2] * kr)
            return _ifft8(yr, yi)

        def body(i, carry):
            r0 = pl.multiple_of(i * rc, rc)
            for l0 in range(0, ct, _V7X_LANES):
                ls = slice(l0, l0 + _V7X_LANES)
                cr, ci = spectrum_rows(r0, ls)
                for t2 in range(_NP):
                    c_scr[t2, pl.ds(r0, rc), ls] = cr[t2].astype(jnp.bfloat16)
                    c_scr[t2, pl.ds(n1 + r0, rc), ls] = ci[t2].astype(jnp.bfloat16)
            return carry

        first_row = lax.broadcasted_iota(jnp.int32, (rc, _V7X_LANES), 0) == 0
        for l0 in range(0, ct, _V7X_LANES):
            ls = slice(l0, l0 + _V7X_LANES)
            nr, ni = spectrum_rows(n1, ls)
            cr, ci = spectrum_rows(0, ls)
            for t2 in range(_NP):
                ang = math.pi * t2 / _NP
                nyq = nr[t2] * math.cos(ang) - ni[t2] * math.sin(ang)
                c_scr[t2, 0:rc, ls] = cr[t2].astype(jnp.bfloat16)
                c_scr[t2, n1:n1 + rc, ls] = jnp.where(first_row, nyq, ci[t2]).astype(jnp.bfloat16)
        lax.fori_loop(1, n1 // rc, body, 0)

        skip = skip_ref[order:order + 1]
        for t2 in range(_NP):
            rows = slice(t2 * n1, (t2 + 1) * n1)
            y = jnp.dot(inv_ref[t2], c_scr[t2], preferred_element_type=jnp.float32)
            z = z_in_ref[0, rows].astype(jnp.float32)
            gate = gate_ref[0, rows].astype(jnp.float32)
            z_out_ref[0, rows] = (gate * (y + z * skip)).astype(z_out_ref.dtype)

    conv_order(0, v_ref, x1_ref, z_scr)
    conv_order(1, z_scr, x2_ref, o_ref)


def _hyena(proj, kf_all, layer, skip, fwd_tab, inv_tab, *, batch, seq, ct):
    n_proj, m, c = proj.shape
    n1 = seq // _NP
    proj4 = proj.reshape(n_proj, batch, seq, c)
    pspec = lambda j: pl.BlockSpec((None, 1, seq, ct), lambda t, b, j=j: (j, b, 0, t))
    resident = dict(pipeline_mode=pl.Buffered(1))
    out = pl.pallas_call(
        functools.partial(_hyena_kernel, seq=seq),
        grid=(c // ct, batch),
        in_specs=[
            pspec(0), pspec(1), pspec(2),
            pl.BlockSpec((None, _ORDER, _NP, 2, _HALF, ct), lambda t, b: (layer, 0, 0, 0, 0, t), **resident),
            pl.BlockSpec((_ORDER, ct), lambda t, b: (0, t)),
            pl.BlockSpec((_NP, 2 * _HALF, n1), lambda t, b: (0, 0, 0), **resident),
            pl.BlockSpec((_NP, n1, 2 * n1), lambda t, b: (0, 0, 0), **resident),
        ],
        out_specs=pl.BlockSpec((1, seq, ct), lambda t, b: (b, 0, t)),
        out_shape=jax.ShapeDtypeStruct((batch, seq, c), jnp.bfloat16),
        scratch_shapes=[pltpu.VMEM((_NP, 2 * _HALF, ct), _STAGE_TWO_DTYPE),
                        pltpu.VMEM((_NP, 2 * n1, ct), jnp.bfloat16),
                        pltpu.VMEM((1, seq, ct), jnp.float32)],
        compiler_params=_cparams(("arbitrary", "arbitrary")),
        name="hyena",
    )(proj4, proj4, proj4, kf_all, skip, fwd_tab, inv_tab)
    return out.reshape(m, c)


def kernel(x, ffn1_pre_g, ffn1_w_gate, ffn1_w_up, ffn1_w_down, ffn1_post_g, mix_pre_g, mix_w_in, gmlp_ln_g, gmlp_ln_b, gmlp_w_s, gmlp_b_s, hy_conv_w, hy_conv_b, hy_filt_w1, hy_filt_b1, hy_filt_w2, hy_filt_b2, hy_filt_w3, hy_filt_b3, hy_filt_freq, hy_filt_w_out, hy_skip, mix_w_out, mix_post_g, ffn2_pre_g, ffn2_w_gate, ffn2_w_up, ffn2_w_down, ffn2_post_g):
    batch, seq, d = x.shape
    depth = ffn1_pre_g.shape[0]
    da = gmlp_ln_g.shape[1]
    assert seq % (_NP * _CHUNK) == 0 and seq // _NP + 1 <= _HALF
    m = batch * seq
    ffn_tiles = dict(batch=batch, tm=512, chunk=_V7X_MXU_DIM)
    ct = _V7X_MXU_DIM
    xp = x.reshape(m, d)

    fwd_np, inv_np = _dft_tables(seq)
    fwd_bf16 = jnp.asarray(fwd_np).astype(jnp.bfloat16)
    inv_bf16 = jnp.asarray(inv_np).astype(jnp.bfloat16)
    kf_all = _filter_spectra(hy_filt_w1, hy_filt_b1, hy_filt_w2, hy_filt_b2, hy_filt_w3, hy_filt_b3,
                             hy_filt_freq, hy_filt_w_out, fwd_bf16, seq=seq, ct=ct)

    ffn1_w = (ffn1_w_gate, ffn1_w_up, ffn1_w_down)
    ffn2_w = (ffn2_w_gate, ffn2_w_up, ffn2_w_down)
    w_in = mix_w_in
    w_out = mix_w_out.astype(jnp.bfloat16)

    for l in range(depth):
        xp, h = _ffn(xp, l, ffn1_pre_g, *ffn1_w, ffn1_post_g, g_next=mix_pre_g,
                     relayout="deinterleave" if l == 0 else None, **ffn_tiles)
        ya = _gmlp(h, w_in, l, gmlp_ln_g[l], gmlp_ln_b[l], gmlp_w_s[l], gmlp_b_s[l], batch=batch, seq=seq, jb=seq // _CHUNK)
        proj = _hyproj(h, w_in, l, hy_conv_w[l], hy_conv_b[l], batch=batch, seq=seq)
        yb = _hyena(proj, kf_all, l, hy_skip[l], fwd_bf16, inv_bf16, batch=batch, seq=seq, ct=proj.shape[-1])
        xp = _ffn(xp, l, ffn2_pre_g, *ffn2_w, ffn2_post_g, mix=(ya, yb, w_out, mix_post_g),
                  relayout="interleave" if l == depth - 1 else None, **ffn_tiles)

    return xp.reshape(batch, seq, d)
```

```python
import functools
import math

import jax
import jax.numpy as jnp
import numpy as np
from jax import lax
from jax.experimental import pallas as pl
from jax.experimental.pallas import tpu as pltpu

_CHUNK = 128
_HEADS_A = 8
_HEAD_DIM_A = 64
_ORDER = 2
_SHORT_K = 3
_FILTER_BANDS = 16
_FILTER_WIDTH = 64
_DECAY_TARGET = 1e-2
_MAX_DECAY = math.log(_DECAY_TARGET) / 0.3
_MIN_DECAY = math.log(_DECAY_TARGET) / 1.5
_HALF_STEP = 0.5
_RMS_EPS = 1e-6
_LN_EPS = 1e-5

_V7X_LANES = 128
_V7X_F32_SUBLANES = 8
_V7X_BF16_SUBLANES = 16
_V7X_MXU_DIM = 256
_V7X_VMEM_LIMIT_BYTES = 62 * 1024 * 1024

_NP = 8
_HALF = 272
_SQRT_HALF = 0.7071067811865476
_STAGE_TWO_DTYPE = jnp.float32


def _cparams(semantics):
    return pltpu.CompilerParams(dimension_semantics=semantics, vmem_limit_bytes=_V7X_VMEM_LIMIT_BYTES)


def _rms_norm(x, g):
    return x * lax.rsqrt(jnp.mean(x * x, axis=-1, keepdims=True) + _RMS_EPS) * g


def _deinterleave_rows(val, slab_scr):
    rows, d = val.shape
    n_slabs = d // _V7X_LANES
    for k in range(n_slabs):
        slab_scr[k] = val[:, k * _V7X_LANES:(k + 1) * _V7X_LANES]
    return jnp.concatenate(
        [jnp.concatenate([slab_scr[k, pl.ds(n2, rows // _NP, stride=_NP), :] for k in range(n_slabs)], axis=1)
         for n2 in range(_NP)], axis=0)


def _interleave_rows(val, slab_scr):
    rows, d = val.shape
    per = rows // _NP
    n_slabs = d // _V7X_LANES
    for n2 in range(_NP):
        for k in range(n_slabs):
            slab_scr[k, pl.ds(n2, per, stride=_NP), :] = val[n2 * per:(n2 + 1) * per, k * _V7X_LANES:(k + 1) * _V7X_LANES]
    return jnp.concatenate([slab_scr[k] for k in range(n_slabs)], axis=1)


def _ffn_kernel(*refs, layer, chunk, mix_in, norm_out, relayout):
    refs = list(refs)
    x_ref = refs.pop(0)
    if mix_in:
        ya_ref, yb_ref, wa_ref, wb_ref, gmix_ref = refs[:5]
        refs = refs[5:]
    gpre_ref, wg_hbm, wu_hbm, wd_hbm, gpost_ref = refs[:5]
    refs = refs[5:]
    gnext_ref = refs.pop(0) if norm_out else None
    o_ref = refs.pop(0)
    hn_ref = refs.pop(0) if norm_out else None
    wg_scr, wu_scr, wd_scr, w_sem = refs[:4]
    slab_scr = refs[4] if relayout else None
    blocked_in = relayout == "interleave"
    blocked_out = relayout == "deinterleave"
    load = (lambda r: r[0].reshape(-1, r.shape[-1])) if blocked_in else (lambda r: r[...])
    dff = wg_scr.shape[1]
    bounds = [(c0, min(c0 + chunk, dff)) for c0 in range(0, dff, chunk)]

    def weight_copies(c):
        cols = pl.ds(bounds[c][0], bounds[c][1] - bounds[c][0])
        return (pltpu.make_async_copy(wg_hbm.at[layer, :, cols], wg_scr.at[:, cols], w_sem.at[c, 0]),
                pltpu.make_async_copy(wu_hbm.at[layer, :, cols], wu_scr.at[:, cols], w_sem.at[c, 1]),
                pltpu.make_async_copy(wd_hbm.at[layer, cols, :], wd_scr.at[cols, :], w_sem.at[c, 2]))

    def tile(first_step):
        if first_step:
            for c in range(len(bounds)):
                for copy in weight_copies(c):
                    copy.start()
        x = load(x_ref)
        if mix_in:
            y = jnp.dot(load(ya_ref), wa_ref[...], preferred_element_type=jnp.float32)
            y = y + jnp.dot(load(yb_ref), wb_ref[...], preferred_element_type=jnp.float32)
            x = x + _rms_norm(y, gmix_ref[...])
        h = _rms_norm(x, gpre_ref[...])
        acc = jnp.zeros(x.shape, jnp.float32)
        for c, (c0, c1) in enumerate(bounds):
            if first_step:
                for copy in weight_copies(c):
                    copy.wait()
            g = jnp.dot(h, wg_scr[:, c0:c1], preferred_element_type=jnp.float32)
            u = jnp.dot(h, wu_scr[:, c0:c1], preferred_element_type=jnp.float32)
            a = g * jax.nn.sigmoid(g) * u
            acc = acc + jnp.dot(a, wd_scr[c0:c1, :], preferred_element_type=jnp.float32)
        out = x + _rms_norm(acc, gpost_ref[...])
        if relayout == "deinterleave":
            out = _deinterleave_rows(out, slab_scr)
        elif relayout == "interleave":
            out = _interleave_rows(out, slab_scr)
        if norm_out:
            hn = _rms_norm(out, gnext_ref[...]).astype(jnp.bfloat16)
            if blocked_out:
                hn_ref[0] = hn.reshape(hn_ref.shape[1:])
            else:
                hn_ref[...] = hn
        if blocked_out:
            o_ref[0] = out.reshape(o_ref.shape[1:])
        else:
            o_ref[...] = out

    first = pl.program_id(0) == 0
    pl.when(first)(functools.partial(tile, True))
    pl.when(jnp.logical_not(first))(functools.partial(tile, False))


def _ffn(x, layer, g_pre, w_gate, w_up, w_down, g_post, *, batch, tm, chunk, mix=None, g_next=None, relayout=None):
    m, d = x.shape
    dff = w_gate.shape[2]
    seq = m // batch
    steps_per_seq = seq // tm
    assert dff % _V7X_MXU_DIM == 0 and chunk % _V7X_MXU_DIM == 0 and seq % tm == 0 and tm % (_NP * _V7X_BF16_SUBLANES) == 0
    row = lambda a: a[layer].reshape(1, d)
    flat_tile = lambda width: pl.BlockSpec((tm, width), lambda i: (i, 0))
    blocked_tile = lambda width: pl.BlockSpec((1, _NP, tm // _NP, width),
                                              lambda i: (i // steps_per_seq, 0, i % steps_per_seq, 0))
    blocked = lambda a: a.reshape(batch, _NP, seq // _NP, a.shape[-1])
    if relayout == "interleave":
        in_tile, put = blocked_tile, blocked
    else:
        in_tile, put = flat_tile, lambda a: a
    vec = pl.BlockSpec((1, d), lambda i: (0, 0))
    resident = lambda r, c, blk=0: pl.BlockSpec((None, r, c), lambda i: (layer, blk, 0), pipeline_mode=pl.Buffered(1))
    args, specs = [put(x)], [in_tile(d)]
    if mix is not None:
        ya, yb, w_out, g_mix = mix
        da = ya.shape[1]
        assert yb.shape[1] == da and w_out.shape[1:] == (2 * da, d)
        args += [put(ya), put(yb), w_out, w_out, row(g_mix)]
        specs += [in_tile(da), in_tile(da), resident(da, d, 0), resident(da, d, 1), vec]
    args += [row(g_pre), w_gate, w_up, w_down, _HALF_STEP * row(g_post)]
    in_hbm = pl.BlockSpec(memory_space=pl.ANY)
    specs += [vec, in_hbm, in_hbm, in_hbm, vec]
    if relayout == "deinterleave":
        out_tile, out_dims = blocked_tile(d), (batch, _NP, seq // _NP, d)
    else:
        out_tile, out_dims = flat_tile(d), (m, d)
    out_shape = [jax.ShapeDtypeStruct(out_dims, jnp.float32)]
    out_specs = [out_tile]
    if g_next is not None:
        args.append(row(g_next))
        specs.append(vec)
        out_shape.append(jax.ShapeDtypeStruct(out_dims, jnp.bfloat16))
        out_specs.append(out_tile)
    n_chunks = -(-dff // chunk)
    scratch = [pltpu.VMEM((d, dff), w_gate.dtype), pltpu.VMEM((d, dff), w_up.dtype), pltpu.VMEM((dff, d), w_down.dtype),
               pltpu.SemaphoreType.DMA((n_chunks, 3))]
    if relayout:
        scratch.append(pltpu.VMEM((d // _V7X_LANES, tm, _V7X_LANES), jnp.float32))
    outs = pl.pallas_call(
        functools.partial(_ffn_kernel, layer=layer, chunk=chunk, mix_in=mix is not None,
                          norm_out=g_next is not None, relayout=relayout),
        grid=(m // tm,),
        in_specs=specs,
        out_specs=out_specs,
        out_shape=out_shape,
        scratch_shapes=scratch,
        compiler_params=_cparams(("arbitrary",)),
        name="ffn",
    )(*args)
    outs = [o.reshape(m, d) for o in outs]
    return outs if g_next is not None else outs[0]


def _gmlp_kernel(h_ref, w_ref, lng_ref, lnb_ref, ws_ref, bs_ref, o_ref, *, jb, da):
    rows = _CHUNK // _NP
    d = h_ref.shape[-1]
    h = h_ref[...].reshape(_NP * jb * rows, d)
    gelu = lambda p: 0.5 * p * (1.0 + lax.erf(p * _SQRT_HALF))
    u = gelu(jnp.dot(h, w_ref[:, :da], preferred_element_type=jnp.float32))
    v = gelu(jnp.dot(h, w_ref[:, da:], preferred_element_type=jnp.float32))
    mu = jnp.mean(v, axis=-1, keepdims=True)
    vc = v - mu
    var = jnp.mean(vc * vc, axis=-1, keepdims=True)
    vn = (vc * lax.rsqrt(var + _LN_EPS) * lng_ref[...] + lnb_ref[...]).astype(jnp.bfloat16)
    lane = lax.broadcasted_iota(jnp.int32, (_CHUNK, _V7X_LANES), 1)
    first_head = lane < _HEAD_DIM_A
    n_pairs = da // _V7X_LANES
    for j in range(jb):
        starts = [(n2 * jb + j) * rows for n2 in range(_NP)]
        vchunk = jnp.concatenate([vn[s:s + rows] for s in starts], axis=0)
        mixed = []
        for q in range(n_pairs):
            r = jnp.dot(ws_ref[q], vchunk[:, q * _V7X_LANES:(q + 1) * _V7X_LANES],
                        preferred_element_type=jnp.float32)
            mixed.append(jnp.where(first_head, r[:_CHUNK], r[_CHUNK:]))
        mixed = jnp.concatenate(mixed, axis=1) + bs_ref[...]
        for n2 in range(_NP):
            s = starts[n2]
            o_ref[n2, j * rows:(j + 1) * rows] = (u[s:s + rows] * mixed[n2 * rows:(n2 + 1) * rows]).astype(o_ref.dtype)


def _gmlp(h, w_in, layer, ln_g, ln_b, w_s, b_s, *, batch, seq, jb):
    m, d = h.shape
    da = ln_g.shape[0]
    rows = _CHUNK // _NP
    n_chunks = seq // _CHUNK
    assert rows == _V7X_BF16_SUBLANES
    n1 = seq // _NP
    h3 = h.reshape(batch * _NP, n1, d)
    tau = np.arange(_CHUNK)
    perm = np.zeros((_CHUNK, _CHUNK), np.float32)
    perm[(tau % _NP) * rows + tau // _NP, tau] = 1.0
    exact = lax.Precision.HIGHEST
    ws_p = jnp.einsum("pm,hmc,qc->hpq", perm, w_s, perm, precision=exact)
    ws_pairs = ws_p.reshape(_HEADS_A // 2, 2 * _CHUNK, _CHUNK).astype(jnp.bfloat16)
    bs_p = jnp.einsum("hm,pm->hp", b_s, perm, precision=exact)
    bs_full = jnp.repeat(bs_p.T, _HEAD_DIM_A, axis=1)
    out = pl.pallas_call(
        functools.partial(_gmlp_kernel, jb=jb, da=da),
        grid=(batch, n_chunks // jb),
        in_specs=[
            pl.BlockSpec((_NP, jb * rows, d), lambda b, j: (b, j, 0)),
            pl.BlockSpec((None, d, 2 * da), lambda b, j: (layer, 0, 0)),
            pl.BlockSpec((1, da), lambda b, j: (0, 0)),
            pl.BlockSpec((1, da), lambda b, j: (0, 0)),
            pl.BlockSpec((_HEADS_A // 2, 2 * _CHUNK, _CHUNK), lambda b, j: (0, 0, 0)),
            pl.BlockSpec((_CHUNK, da), lambda b, j: (0, 0)),
        ],
        out_specs=pl.BlockSpec((_NP, jb * rows, da), lambda b, j: (b, j, 0)),
        out_shape=jax.ShapeDtypeStruct((batch * _NP, n1, da), jnp.bfloat16),
        compiler_params=_cparams(("arbitrary", "arbitrary")),
        name="gmlp",
    )(h3, w_in, ln_g.reshape(1, da), ln_b.reshape(1, da), ws_pairs, bs_full)
    return out.reshape(m, da)


def _hyproj_kernel(h_ref, *refs, n1):
    n_proj = _ORDER + 1
    w_refs, (cw_ref, cb_ref, o_ref) = refs[:n_proj], refs[n_proj:]
    c = o_ref.shape[-1]
    h = h_ref[0]
    sub = cw_ref.shape[1]
    tiled = lambda a: a.reshape(n1 // sub, sub, c)
    for j in range(n_proj):
        p = jnp.dot(h, w_refs[j][...], preferred_element_type=jnp.float32)
        blocks = [p[k * n1:(k + 1) * n1] for k in range(_NP)]
        row = lax.broadcasted_iota(jnp.int32, blocks[0].shape, 0)
        before_first = jnp.where(row == 0, 0.0, pltpu.roll(blocks[_NP - 1], 1, axis=0))
        after_last = jnp.where(row == n1 - 1, 0.0, pltpu.roll(blocks[0], n1 - 1, axis=0))
        cols = slice(j * c, (j + 1) * c)
        w0, w1, w2 = cw_ref[0, :, cols], cw_ref[1, :, cols], cw_ref[2, :, cols]
        bias = cb_ref[:, cols]
        for k in range(_NP):
            prev = blocks[k - 1] if k > 0 else before_first
            nxt = blocks[k + 1] if k < _NP - 1 else after_last
            y = ((bias + tiled(prev) * w0) + tiled(blocks[k]) * w1) + tiled(nxt) * w2
            o_ref[j, k * n1:(k + 1) * n1] = y.reshape(n1, c).astype(o_ref.dtype)


def _hyproj(h, w_in, layer, conv_w, conv_b, *, batch, seq):
    m, d = h.shape
    n_proj = _ORDER + 1
    c = conv_w.shape[1] // n_proj
    first = (w_in.shape[2] - n_proj * c) // c
    w_spec = lambda j: pl.BlockSpec((None, d, c), lambda b: (layer, 0, first + j), pipeline_mode=pl.Buffered(1))
    sub = _V7X_F32_SUBLANES
    taps = jnp.broadcast_to(conv_w[:, None, :], (_SHORT_K, sub, n_proj * c))
    bias = jnp.broadcast_to(conv_b[None, :], (sub, n_proj * c))
    return pl.pallas_call(
        functools.partial(_hyproj_kernel, n1=seq // _NP),
        grid=(batch,),
        in_specs=[pl.BlockSpec((1, seq, d), lambda b: (b, 0, 0))] + [w_spec(j) for j in range(n_proj)] + [
            pl.BlockSpec((_SHORT_K, sub, n_proj * c), lambda b: (0, 0, 0)),
            pl.BlockSpec((sub, n_proj * c), lambda b: (0, 0)),
        ],
        out_specs=pl.BlockSpec((n_proj, seq, c), lambda b: (0, b, 0)),
        out_shape=jax.ShapeDtypeStruct((n_proj, m, c), jnp.bfloat16),
        compiler_params=_cparams(("arbitrary",)),
        name="hyproj",
    )(h.reshape(batch, seq, d), *([w_in] * n_proj), taps, bias)


@functools.lru_cache(maxsize=None)
def _dft_tables(seq):
    n1 = seq // _NP
    n_fft = 2 * seq
    k1 = np.arange(_HALF)[None, :, None]
    t = _NP * np.arange(n1)[None, None, :] + np.arange(_NP)[:, None, None]
    theta = ((k1 * t) % n_fft) * (2.0 * math.pi / n_fft)
    valid = k1 <= n1
    cos = np.where(valid, np.cos(theta), 0.0)
    sin = np.where(valid, np.sin(theta), 0.0)
    fwd = np.concatenate([cos, -sin], axis=1)
    weight = np.where(k1 == 0, 1.0, 2.0) / n_fft
    inv_re = (weight * cos)[:, :n1]
    inv_im = (-weight * sin)[:, :n1]
    inv_im[:, 0, :] = ((-1.0) ** np.arange(n1))[None, :] / n_fft
    inv = np.concatenate([inv_re, inv_im], axis=1).transpose(0, 2, 1)
    return fwd.astype(np.float32), np.ascontiguousarray(inv).astype(np.float32)


def _fft4(c0, c2_sum, c2_diff, c1, c3):
    (d0r, d0i), (d2r, d2i) = c2_sum, c2_diff
    d1r, d1i = c1[0] + c3[0], c1[1] + c3[1]
    er, ei = c1[0] - c3[0], c1[1] - c3[1]
    return [(d0r + d1r, d0i + d1i), (d2r + ei, d2i - er), (d0r - d1r, d0i - d1i), (d2r - ei, d2i + er)]


def _fft8(re, im):
    s = [(re[j] + re[j + 4], im[j] + im[j + 4]) for j in range(4)]
    d = [(re[j] - re[j + 4], im[j] - im[j + 4]) for j in range(4)]
    even = _fft4(s[0], (s[0][0] + s[2][0], s[0][1] + s[2][1]), (s[0][0] - s[2][0], s[0][1] - s[2][1]), s[1], s[3])
    o1 = ((d[1][0] + d[1][1]) * _SQRT_HALF, (d[1][1] - d[1][0]) * _SQRT_HALF)
    o3 = ((d[3][1] - d[3][0]) * _SQRT_HALF, (d[3][0] + d[3][1]) * -_SQRT_HALF)
    odd = _fft4(d[0], (d[0][0] + d[2][1], d[0][1] - d[2][0]), (d[0][0] - d[2][1], d[0][1] + d[2][0]), o1, o3)
    out = [None] * 8
    for k in range(4):
        out[2 * k], out[2 * k + 1] = even[k], odd[k]
    return [o[0] for o in out], [o[1] for o in out]


def _ifft8(re, im):
    o_im, o_re = _fft8(im, re)
    return o_re, o_im


def _filter_kernel(w1_ref, b1_ref, w2_ref, b2_ref, w3_ref, b3_ref, fr_ref, wof_ref, wob_ref, dl_ref,
                   fwd_ref, o_ref, h_scr, af_scr, ab_scr, *, seq):
    n1 = seq // _NP
    fw = _FILTER_WIDTH
    hp = lax.Precision.HIGHEST

    @pl.when(pl.program_id(1) == 0)
    def _():
        nb = _FILTER_BANDS
        r = lax.broadcasted_iota(jnp.int32, (nb, seq), 1)
        pos = (_NP * (r % n1) + r // n1).astype(jnp.float32)
        band_idx = lax.broadcasted_iota(jnp.int32, (nb, seq), 0).astype(jnp.float32)
        band = 1e-4 + band_idx * ((nb - 1 - 1e-4) / (nb - 1))
        ang = (2.0 * math.pi / seq) * pos * band
        first_row = lax.broadcasted_iota(jnp.int32, (fw - 2 * nb, seq), 0) == 0
        tail = jnp.where(first_row, jnp.concatenate([pos] * ((fw - 2 * nb) // nb), axis=0) * (1.0 / (seq - 1)), 0.0)
        z = jnp.concatenate([jnp.cos(ang), -jnp.sin(ang), tail], axis=0)
        freq = fr_ref[0]
        h = jnp.sin(freq * (jnp.dot(w1_ref[0], z, precision=hp, preferred_element_type=jnp.float32) + b1_ref[0]))
        h = jnp.sin(freq * (jnp.dot(w2_ref[0], h, precision=hp, preferred_element_type=jnp.float32) + b2_ref[0]))
        h = jnp.sin(freq * (jnp.dot(w3_ref[0], h, precision=hp, preferred_element_type=jnp.float32) + b3_ref[0]))
        h_scr[...] = h.T

    r = lax.broadcasted_iota(jnp.int32, (seq, 1), 0)
    pos = (_NP * (r % n1) + r // n1).astype(jnp.float32)
    h = h_scr[...].astype(jnp.bfloat16)
    decay = jnp.exp(-(pos * (1.0 / (seq - 1))) * dl_ref[...])
    hf = jnp.dot(h, wof_ref[0].astype(jnp.bfloat16), preferred_element_type=jnp.float32) * decay
    hb = jnp.dot(h, wob_ref[0].astype(jnp.bfloat16), preferred_element_type=jnp.float32) * decay
    hb = jnp.where(pos == 0.0, 0.0, hb)
    hf = hf.astype(jnp.bfloat16)
    hb = hb.astype(jnp.bfloat16)
    for n2 in range(_NP):
        af_scr[n2] = jnp.dot(fwd_ref[n2], hf[n2 * n1:(n2 + 1) * n1], preferred_element_type=jnp.float32)
        ab_scr[n2] = jnp.dot(fwd_ref[n2], hb[n2 * n1:(n2 + 1) * n1], preferred_element_type=jnp.float32)

    rc = _V7X_BF16_SUBLANES

    def body(i, carry):
        r0 = pl.multiple_of(i * rc, rc)
        back = [(_NP - n2) % _NP for n2 in range(_NP)]
        kr, ki = _fft8([af_scr[n2, pl.ds(r0, rc)] + ab_scr[back[n2], pl.ds(r0, rc)] for n2 in range(_NP)],
                       [af_scr[n2, pl.ds(_HALF + r0, rc)] - ab_scr[back[n2], pl.ds(_HALF + r0, rc)] for n2 in range(_NP)])
        for k2 in range(_NP):
            o_ref[0, 0, k2, 0, pl.ds(r0, rc)] = kr[k2].astype(o_ref.dtype)
            o_ref[0, 0, k2, 1, pl.ds(r0, rc)] = ki[k2].astype(o_ref.dtype)
        return carry

    lax.fori_loop(0, _HALF // rc, body, 0)


def _filter_spectra(w1, b1, w2, b2, w3, b3, freq, w_out, fwd_tab, *, seq, ct):
    n_layers = w1.shape[0]
    c = w_out.shape[2] // (2 * _ORDER)
    n_ct = c // ct
    fw = _FILTER_WIDTH
    w1r = jnp.concatenate([w1[:, 1:], w1[:, :1]], axis=1)
    w1t = jnp.pad(w1r, ((0, 0), (0, fw - w1.shape[1]), (0, 0))).transpose(0, 2, 1)
    col = lambda a: a.reshape(n_layers, fw, 1)
    deltas = jnp.abs(jnp.linspace(_MIN_DECAY, _MAX_DECAY, c, dtype=jnp.float32)).reshape(1, c)
    lmap3 = lambda l, s: (l, 0, 0)
    wo_spec = lambda direction: pl.BlockSpec(
        (1, fw, ct), lambda l, s: (l, 0, (2 * (s // n_ct) + direction) * n_ct + s % n_ct))
    return pl.pallas_call(
        functools.partial(_filter_kernel, seq=seq),
        grid=(n_layers, _ORDER * n_ct),
        in_specs=[
            pl.BlockSpec((1, fw, fw), lmap3), pl.BlockSpec((1, fw, 1), lmap3),
            pl.BlockSpec((1, fw, fw), lmap3), pl.BlockSpec((1, fw, 1), lmap3),
            pl.BlockSpec((1, fw, fw), lmap3), pl.BlockSpec((1, fw, 1), lmap3),
            pl.BlockSpec((1, fw, 1), lmap3),
            wo_spec(0), wo_spec(1),
            pl.BlockSpec((1, ct), lambda l, s: (0, s % n_ct)),
            pl.BlockSpec((_NP, 2 * _HALF, seq // _NP), lambda l, s: (0, 0, 0)),
        ],
        out_specs=pl.BlockSpec((1, 1, _NP, 2, _HALF, ct), lambda l, s: (l, s // n_ct, 0, 0, 0, s % n_ct)),
        out_shape=jax.ShapeDtypeStruct((n_layers, _ORDER, _NP, 2, _HALF, c), _STAGE_TWO_DTYPE),
        scratch_shapes=[pltpu.VMEM((seq, fw), jnp.float32),
                        pltpu.VMEM((_NP, 2 * _HALF, ct), jnp.float32),
                        pltpu.VMEM((_NP, 2 * _HALF, ct), jnp.float32)],
        compiler_params=_cparams(("arbitrary", "arbitrary")),
        name="filter_spectra",
    )(w1t, col(b1), w2.transpose(0, 2, 1), col(b2), w3.transpose(0, 2, 1), col(b3), col(freq),
      w_out, w_out, deltas, fwd_tab)


def _hyena_kernel(v_ref, x1_ref, x2_ref, kf_ref, skip_ref, fwd_ref, inv_ref, o_ref, a_scr, c_scr, z_scr, *, seq):
    n1 = seq // _NP
    rc = _V7X_BF16_SUBLANES
    ct = o_ref.shape[-1]

    def conv_order(order, z_in_ref, gate_ref, z_out_ref):
        for n2 in range(_NP):
            zb = z_in_ref[0, n2 * n1:(n2 + 1) * n1].astype(jnp.bfloat16)
            a_scr[n2] = jnp.dot(fwd_ref[n2], zb, preferred_element_type=jnp.float32).astype(a_scr.dtype)

        def spectrum_rows(r0, ls):
            xr, xi = _fft8([a_scr[n2, pl.ds(r0, rc), ls] for n2 in range(_NP)],
                           [a_scr[n2, pl.ds(_HALF + r0, rc), ls] for n2 in range(_NP)])
            yr, yi = [], []
            for k2 in range(_NP):
                kr = kf_ref[order, k2, 0, pl.ds(r0, rc), ls]
                ki = kf_ref[order, k2, 1, pl.ds(r0, rc), ls]
                yr.append(xr[k2] * kr - xi[k2] * ki)
                yi.append(xr[k2] * ki + xi[k2] * kr)
            return _ifft8(yr, yi)

        def body(i, carry):
            r0 = pl.multiple_of(i * rc, rc)
            for l0 in range(0, ct, _V7X_LANES):
                ls = slice(l0, l0 + _V7X_LANES)
                cr, ci = spectrum_rows(r0, ls)
                for t2 in range(_NP):
                    c_scr[t2, pl.ds(r0, rc), ls] = cr[t2].astype(jnp.bfloat16)
                    c_scr[t2, pl.ds(n1 + r0, rc), ls] = ci[t2].astype(jnp.bfloat16)
            return carry

        first_row = lax.broadcasted_iota(jnp.int32, (rc, _V7X_LANES), 0) == 0
        for l0 in range(0, ct, _V7X_LANES):
            ls = slice(l0, l0 + _V7X_LANES)
            nr, ni = spectrum_rows(n1, ls)
            cr, ci = spectrum_rows(0, ls)
            for t2 in range(_NP):
                ang = math.pi * t2 / _NP
                nyq = nr[t2] * math.cos(ang) - ni[t2] * math.sin(ang)
                c_scr[t2, 0:rc, ls] = cr[t2].astype(jnp.bfloat16)
                c_scr[t2, n1:n1 + rc, ls] = jnp.where(first_row, nyq, ci[t2]).astype(jnp.bfloat16)
        lax.fori_loop(1, n1 // rc, body, 0)

        skip = skip_ref[order:order + 1]
        for t2 in range(_NP):
            rows = slice(t2 * n1, (t2 + 1) * n1)
            y = jnp.dot(inv_ref[t2], c_scr[t2], preferred_element_type=jnp.float32)
            z = z_in_ref[0, rows].astype(jnp.float32)
            gate = gate_ref[0, rows].astype(jnp.float32)
            z_out_ref[0, rows] = (gate * (y + z * skip)).astype(z_out_ref.dtype)

    conv_order(0, v_ref, x1_ref, z_scr)
    conv_order(1, z_scr, x2_ref, o_ref)


def _hyena(proj, kf_all, layer, skip, fwd_tab, inv_tab, *, batch, seq, ct):
    n_proj, m, c = proj.shape
    n1 = seq // _NP
    proj4 = proj.reshape(n_proj, batch, seq, c)
    pspec = lambda j: pl.BlockSpec((None, 1, seq, ct), lambda t, b, j=j: (j, b, 0, t))
    resident = dict(pipeline_mode=pl.Buffered(1))
    out = pl.pallas_call(
        functools.partial(_hyena_kernel, seq=seq),
        grid=(c // ct, batch),
        in_specs=[
            pspec(0), pspec(1), pspec(2),
            pl.BlockSpec((None, _ORDER, _NP, 2, _HALF, ct), lambda t, b: (layer, 0, 0, 0, 0, t), **resident),
            pl.BlockSpec((_ORDER, ct), lambda t, b: (0, t)),
            pl.BlockSpec((_NP, 2 * _HALF, n1), lambda t, b: (0, 0, 0), **resident),
            pl.BlockSpec((_NP, n1, 2 * n1), lambda t, b: (0, 0, 0), **resident),
        ],
        out_specs=pl.BlockSpec((1, seq, ct), lambda t, b: (b, 0, t)),
        out_shape=jax.ShapeDtypeStruct((batch, seq, c), jnp.bfloat16),
        scratch_shapes=[pltpu.VMEM((_NP, 2 * _HALF, ct), _STAGE_TWO_DTYPE),
                        pltpu.VMEM((_NP, 2 * n1, ct), jnp.bfloat16),
                        pltpu.VMEM((1, seq, ct), jnp.float32)],
        compiler_params=_cparams(("arbitrary", "arbitrary")),
        name="hyena",
    )(proj4, proj4, proj4, kf_all, skip, fwd_tab, inv_tab)
    return out.reshape(m, c)


def kernel(x, ffn1_pre_g, ffn1_w_gate, ffn1_w_up, ffn1_w_down, ffn1_post_g, mix_pre_g, mix_w_in, gmlp_ln_g, gmlp_ln_b, gmlp_w_s, gmlp_b_s, hy_conv_w, hy_conv_b, hy_filt_w1, hy_filt_b1, hy_filt_w2, hy_filt_b2, hy_filt_w3, hy_filt_b3, hy_filt_freq, hy_filt_w_out, hy_skip, mix_w_out, mix_post_g, ffn2_pre_g, ffn2_w_gate, ffn2_w_up, ffn2_w_down, ffn2_post_g):
    batch, seq, d = x.shape
    depth = ffn1_pre_g.shape[0]
    da = gmlp_ln_g.shape[1]
    assert seq % (_NP * _CHUNK) == 0 and seq // _NP + 1 <= _HALF
    m = batch * seq
    ffn_tiles = dict(batch=batch, tm=512, chunk=_V7X_MXU_DIM)
    ct = _V7X_MXU_DIM
    xp = x.reshape(m, d)

    fwd_np, inv_np = _dft_tables(seq)
    fwd_bf16 = jnp.asarray(fwd_np).astype(jnp.bfloat16)
    inv_bf16 = jnp.asarray(inv_np).astype(jnp.bfloat16)
    kf_all = _filter_spectra(hy_filt_w1, hy_filt_b1, hy_filt_w2, hy_filt_b2, hy_filt_w3, hy_filt_b3,
                             hy_filt_freq, hy_filt_w_out, fwd_bf16, seq=seq, ct=ct)

    ffn1_w = (ffn1_w_gate, ffn1_w_up, ffn1_w_down)
    ffn2_w = (ffn2_w_gate, ffn2_w_up, ffn2_w_down)
    w_in = mix_w_in
    w_out = mix_w_out.astype(jnp.bfloat16)

    for l in range(depth):
        xp, h = _ffn(xp, l, ffn1_pre_g, *ffn1_w, ffn1_post_g, g_next=mix_pre_g,
                     relayout="deinterleave" if l == 0 else None, **ffn_tiles)
        ya = _gmlp(h, w_in, l, gmlp_ln_g[l], gmlp_ln_b[l], gmlp_w_s[l], gmlp_b_s[l], batch=batch, seq=seq, jb=seq // _CHUNK)
        proj = _hyproj(h, w_in, l, hy_conv_w[l], hy_conv_b[l], batch=batch, seq=seq)
        yb = _hyena(proj, kf_all, l, hy_skip[l], fwd_bf16, inv_bf16, batch=batch, seq=seq, ct=proj.shape[-1])
        xp = _ffn(xp, l, ffn2_pre_g, *ffn2_w, ffn2_post_g, mix=(ya, yb, w_out, mix_post_g),
                  relayout="interleave" if l == depth - 1 else None, **ffn_tiles)

    return xp.reshape(batch, seq, d)
```

```python
import functools
import math

import jax
import jax.numpy as jnp
import numpy as np
from jax import lax
from jax.experimental import pallas as pl
from jax.experimental.pallas import tpu as pltpu

_CHUNK = 128
_HEADS_A = 8
_HEAD_DIM_A = 64
_ORDER = 2
_SHORT_K = 3
_FILTER_BANDS = 16
_FILTER_WIDTH = 64
_DECAY_TARGET = 1e-2
_MAX_DECAY = math.log(_DECAY_TARGET) / 0.3
_MIN_DECAY = math.log(_DECAY_TARGET) / 1.5
_HALF_STEP = 0.5
_RMS_EPS = 1e-6
_LN_EPS = 1e-5

_V7X_LANES = 128
_V7X_F32_SUBLANES = 8
_V7X_BF16_SUBLANES = 16
_V7X_MXU_DIM = 256
_V7X_VMEM_LIMIT_BYTES = 62 * 1024 * 1024

_NP = 8
_HALF = 272
_SQRT_HALF = 0.7071067811865476
_STAGE_TWO_DTYPE = jnp.float32


def _cparams(semantics):
    return pltpu.CompilerParams(dimension_semantics=semantics, vmem_limit_bytes=_V7X_VMEM_LIMIT_BYTES)


def _rms_norm(x, g):
    return x * lax.rsqrt(jnp.mean(x * x, axis=-1, keepdims=True) + _RMS_EPS) * g


def _deinterleave_rows(val, slab_scr):
    rows, d = val.shape
    n_slabs = d // _V7X_LANES
    for k in range(n_slabs):
        slab_scr[k] = val[:, k * _V7X_LANES:(k + 1) * _V7X_LANES]
    return jnp.concatenate(
        [jnp.concatenate([slab_scr[k, pl.ds(n2, rows // _NP, stride=_NP), :] for k in range(n_slabs)], axis=1)
         for n2 in range(_NP)], axis=0)


def _interleave_rows(val, slab_scr):
    rows, d = val.shape
    per = rows // _NP
    n_slabs = d // _V7X_LANES
    for n2 in range(_NP):
        for k in range(n_slabs):
            slab_scr[k, pl.ds(n2, per, stride=_NP), :] = val[n2 * per:(n2 + 1) * per, k * _V7X_LANES:(k + 1) * _V7X_LANES]
    return jnp.concatenate([slab_scr[k] for k in range(n_slabs)], axis=1)


def _ffn_kernel(*refs, layer, chunk, mix_in, norm_out, relayout):
    refs = list(refs)
    x_ref = refs.pop(0)
    if mix_in:
        ya_ref, yb_ref, wa_ref, wb_ref, gmix_ref = refs[:5]
        refs = refs[5:]
    gpre_ref, wg_hbm, wu_hbm, wd_hbm, gpost_ref = refs[:5]
    refs = refs[5:]
    gnext_ref = refs.pop(0) if norm_out else None
    o_ref = refs.pop(0)
    hn_ref = refs.pop(0) if norm_out else None
    wg_scr, wu_scr, wd_scr, w_sem = refs[:4]
    slab_scr = refs[4] if relayout else None
    blocked_in = relayout == "interleave"
    blocked_out = relayout == "deinterleave"
    load = (lambda r: r[0].reshape(-1, r.shape[-1])) if blocked_in else (lambda r: r[...])
    dff = wg_scr.shape[1]
    bounds = [(c0, min(c0 + chunk, dff)) for c0 in range(0, dff, chunk)]

    def weight_copies(c):
        cols = pl.ds(bounds[c][0], bounds[c][1] - bounds[c][0])
        return (pltpu.make_async_copy(wg_hbm.at[layer, :, cols], wg_scr.at[:, cols], w_sem.at[c, 0]),
                pltpu.make_async_copy(wu_hbm.at[layer, :, cols], wu_scr.at[:, cols], w_sem.at[c, 1]),
                pltpu.make_async_copy(wd_hbm.at[layer, cols, :], wd_scr.at[cols, :], w_sem.at[c, 2]))

    def tile(first_step):
        if first_step:
            for c in range(len(bounds)):
                for copy in weight_copies(c):
                    copy.start(priority=c % 2)
        x = load(x_ref)
        if mix_in:
            y = jnp.dot(load(ya_ref), wa_ref[...], preferred_element_type=jnp.float32)
            y = y + jnp.dot(load(yb_ref), wb_ref[...], preferred_element_type=jnp.float32)
            x = x + _rms_norm(y, gmix_ref[...])
        h = _rms_norm(x, gpre_ref[...])
        acc = jnp.zeros(x.shape, jnp.float32)
        for c, (c0, c1) in enumerate(bounds):
            if first_step:
                for copy in weight_copies(c):
                    copy.wait()
            g = jnp.dot(h, wg_scr[:, c0:c1], preferred_element_type=jnp.float32)
            u = jnp.dot(h, wu_scr[:, c0:c1], preferred_element_type=jnp.float32)
            a = g * jax.nn.sigmoid(g) * u
            acc = acc + jnp.dot(a, wd_scr[c0:c1, :], preferred_element_type=jnp.float32)
        out = x + _rms_norm(acc, gpost_ref[...])
        if relayout == "deinterleave":
            out = _deinterleave_rows(out, slab_scr)
        elif relayout == "interleave":
            out = _interleave_rows(out, slab_scr)
        if norm_out:
            hn = _rms_norm(out, gnext_ref[...]).astype(jnp.bfloat16)
            if blocked_out:
                hn_ref[0] = hn.reshape(hn_ref.shape[1:])
            else:
                hn_ref[...] = hn
        if blocked_out:
            o_ref[0] = out.reshape(o_ref.shape[1:])
        else:
            o_ref[...] = out

    first = pl.program_id(0) == 0
    pl.when(first)(functools.partial(tile, True))
    pl.when(jnp.logical_not(first))(functools.partial(tile, False))


def _ffn(x, layer, g_pre, w_gate, w_up, w_down, g_post, *, batch, tm, chunk, mix=None, g_next=None, relayout=None):
    m, d = x.shape
    dff = w_gate.shape[2]
    seq = m // batch
    steps_per_seq = seq // tm
    assert dff % _V7X_MXU_DIM == 0 and chunk % _V7X_MXU_DIM == 0 and seq % tm == 0 and tm % (_NP * _V7X_BF16_SUBLANES) == 0
    row = lambda a: a[layer].reshape(1, d)
    flat_tile = lambda width: pl.BlockSpec((tm, width), lambda i: (i, 0))
    blocked_tile = lambda width: pl.BlockSpec((1, _NP, tm // _NP, width),
                                              lambda i: (i // steps_per_seq, 0, i % steps_per_seq, 0))
    blocked = lambda a: a.reshape(batch, _NP, seq // _NP, a.shape[-1])
    if relayout == "interleave":
        in_tile, put = blocked_tile, blocked
    else:
        in_tile, put = flat_tile, lambda a: a
    vec = pl.BlockSpec((1, d), lambda i: (0, 0))
    resident = lambda r, c, blk=0: pl.BlockSpec((None, r, c), lambda i: (layer, blk, 0), pipeline_mode=pl.Buffered(1))
    args, specs = [put(x)], [in_tile(d)]
    if mix is not None:
        ya, yb, w_out, g_mix = mix
        da = ya.shape[1]
        assert yb.shape[1] == da and w_out.shape[1:] == (2 * da, d)
        args += [put(ya), put(yb), w_out, w_out, row(g_mix)]
        specs += [in_tile(da), in_tile(da), resident(da, d, 0), resident(da, d, 1), vec]
    args += [row(g_pre), w_gate, w_up, w_down, _HALF_STEP * row(g_post)]
    in_hbm = pl.BlockSpec(memory_space=pl.ANY)
    specs += [vec, in_hbm, in_hbm, in_hbm, vec]
    if relayout == "deinterleave":
        out_tile, out_dims = blocked_tile(d), (batch, _NP, seq // _NP, d)
    else:
        out_tile, out_dims = flat_tile(d), (m, d)
    out_shape = [jax.ShapeDtypeStruct(out_dims, jnp.float32)]
    out_specs = [out_tile]
    if g_next is not None:
        args.append(row(g_next))
        specs.append(vec)
        out_shape.append(jax.ShapeDtypeStruct(out_dims, jnp.bfloat16))
        out_specs.append(out_tile)
    n_chunks = -(-dff // chunk)
    scratch = [pltpu.VMEM((d, dff), w_gate.dtype), pltpu.VMEM((d, dff), w_up.dtype), pltpu.VMEM((dff, d), w_down.dtype),
               pltpu.SemaphoreType.DMA((n_chunks, 3))]
    if relayout:
        scratch.append(pltpu.VMEM((d // _V7X_LANES, tm, _V7X_LANES), jnp.float32))
    outs = pl.pallas_call(
        functools.partial(_ffn_kernel, layer=layer, chunk=chunk, mix_in=mix is not None,
                          norm_out=g_next is not None, relayout=relayout),
        grid=(m // tm,),
        in_specs=specs,
        out_specs=out_specs,
        out_shape=out_shape,
        scratch_shapes=scratch,
        compiler_params=_cparams(("arbitrary",)),
        name="ffn",
    )(*args)
    outs = [o.reshape(m, d) for o in outs]
    return outs if g_next is not None else outs[0]


def _gmlp_kernel(h_ref, w_ref, lng_ref, lnb_ref, ws_ref, bs_ref, o_ref, *, jb, da):
    rows = _CHUNK // _NP
    d = h_ref.shape[-1]
    h = h_ref[...].reshape(_NP * jb * rows, d)
    gelu = lambda p: 0.5 * p * (1.0 + lax.erf(p * _SQRT_HALF))
    u = gelu(jnp.dot(h, w_ref[:, :da], preferred_element_type=jnp.float32))
    v = gelu(jnp.dot(h, w_ref[:, da:], preferred_element_type=jnp.float32))
    mu = jnp.mean(v, axis=-1, keepdims=True)
    vc = v - mu
    var = jnp.mean(vc * vc, axis=-1, keepdims=True)
    vn = (vc * lax.rsqrt(var + _LN_EPS) * lng_ref[...] + lnb_ref[...]).astype(jnp.bfloat16)
    lane = lax.broadcasted_iota(jnp.int32, (_CHUNK, _V7X_LANES), 1)
    first_head = lane < _HEAD_DIM_A
    n_pairs = da // _V7X_LANES
    for j in range(jb):
        starts = [(n2 * jb + j) * rows for n2 in range(_NP)]
        vchunk = jnp.concatenate([vn[s:s + rows] for s in starts], axis=0)
        mixed = []
        for q in range(n_pairs):
            r = jnp.dot(ws_ref[q], vchunk[:, q * _V7X_LANES:(q + 1) * _V7X_LANES],
                        preferred_element_type=jnp.float32)
            mixed.append(jnp.where(first_head, r[:_CHUNK], r[_CHUNK:]))
        mixed = jnp.concatenate(mixed, axis=1) + bs_ref[...]
        for n2 in range(_NP):
            s = starts[n2]
            o_ref[n2, j * rows:(j + 1) * rows] = (u[s:s + rows] * mixed[n2 * rows:(n2 + 1) * rows]).astype(o_ref.dtype)


def _gmlp(h, w_in, layer, ln_g, ln_b, w_s, b_s, *, batch, seq, jb):
    m, d = h.shape
    da = ln_g.shape[0]
    rows = _CHUNK // _NP
    n_chunks = seq // _CHUNK
    assert rows == _V7X_BF16_SUBLANES
    n1 = seq // _NP
    h3 = h.reshape(batch * _NP, n1, d)
    tau = np.arange(_CHUNK)
    perm = np.zeros((_CHUNK, _CHUNK), np.float32)
    perm[(tau % _NP) * rows + tau // _NP, tau] = 1.0
    exact = lax.Precision.HIGHEST
    ws_p = jnp.einsum("pm,hmc,qc->hpq", perm, w_s, perm, precision=exact)
    ws_pairs = ws_p.reshape(_HEADS_A // 2, 2 * _CHUNK, _CHUNK).astype(jnp.bfloat16)
    bs_p = jnp.einsum("hm,pm->hp", b_s, perm, precision=exact)
    bs_full = jnp.repeat(bs_p.T, _HEAD_DIM_A, axis=1)
    out = pl.pallas_call(
        functools.partial(_gmlp_kernel, jb=jb, da=da),
        grid=(batch, n_chunks // jb),
        in_specs=[
            pl.BlockSpec((_NP, jb * rows, d), lambda b, j: (b, j, 0)),
            pl.BlockSpec((None, d, 2 * da), lambda b, j: (layer, 0, 0)),
            pl.BlockSpec((1, da), lambda b, j: (0, 0)),
            pl.BlockSpec((1, da), lambda b, j: (0, 0)),
            pl.BlockSpec((_HEADS_A // 2, 2 * _CHUNK, _CHUNK), lambda b, j: (0, 0, 0)),
            pl.BlockSpec((_CHUNK, da), lambda b, j: (0, 0)),
        ],
        out_specs=pl.BlockSpec((_NP, jb * rows, da), lambda b, j: (b, j, 0)),
        out_shape=jax.ShapeDtypeStruct((batch * _NP, n1, da), jnp.bfloat16),
        compiler_params=_cparams(("arbitrary", "arbitrary")),
        name="gmlp",
    )(h3, w_in, ln_g.reshape(1, da), ln_b.reshape(1, da), ws_pairs, bs_full)
    return out.reshape(m, da)


def _hyproj_kernel(h_ref, *refs, n1):
    n_proj = _ORDER + 1
    w_refs, (cw_ref, cb_ref, o_ref) = refs[:n_proj], refs[n_proj:]
    c = o_ref.shape[-1]
    h = h_ref[0]
    sub = cw_ref.shape[1]
    tiled = lambda a: a.reshape(n1 // sub, sub, c)
    for j in range(n_proj):
        p = jnp.dot(h, w_refs[j][...], preferred_element_type=jnp.float32)
        blocks = [p[k * n1:(k + 1) * n1] for k in range(_NP)]
        row = lax.broadcasted_iota(jnp.int32, blocks[0].shape, 0)
        before_first = jnp.where(row == 0, 0.0, pltpu.roll(blocks[_NP - 1], 1, axis=0))
        after_last = jnp.where(row == n1 - 1, 0.0, pltpu.roll(blocks[0], n1 - 1, axis=0))
        cols = slice(j * c, (j + 1) * c)
        w0, w1, w2 = cw_ref[0, :, cols], cw_ref[1, :, cols], cw_ref[2, :, cols]
        bias = cb_ref[:, cols]
        for k in range(_NP):
            prev = blocks[k - 1] if k > 0 else before_first
            nxt = blocks[k + 1] if k < _NP - 1 else after_last
            y = ((bias + tiled(prev) * w0) + tiled(blocks[k]) * w1) + tiled(nxt) * w2
            o_ref[j, k * n1:(k + 1) * n1] = y.reshape(n1, c).astype(o_ref.dtype)


def _hyproj(h, w_in, layer, conv_w, conv_b, *, batch, seq):
    m, d = h.shape
    n_proj = _ORDER + 1
    c = conv_w.shape[1] // n_proj
    first = (w_in.shape[2] - n_proj * c) // c
    w_spec = lambda j: pl.BlockSpec((None, d, c), lambda b: (layer, 0, first + j), pipeline_mode=pl.Buffered(1))
    sub = _V7X_F32_SUBLANES
    taps = jnp.broadcast_to(conv_w[:, None, :], (_SHORT_K, sub, n_proj * c))
    bias = jnp.broadcast_to(conv_b[None, :], (sub, n_proj * c))
    return pl.pallas_call(
        functools.partial(_hyproj_kernel, n1=seq // _NP),
        grid=(batch,),
        in_specs=[pl.BlockSpec((1, seq, d), lambda b: (b, 0, 0))] + [w_spec(j) for j in range(n_proj)] + [
            pl.BlockSpec((_SHORT_K, sub, n_proj * c), lambda b: (0, 0, 0)),
            pl.BlockSpec((sub, n_proj * c), lambda b: (0, 0)),
        ],
        out_specs=pl.BlockSpec((n_proj, seq, c), lambda b: (0, b, 0)),
        out_shape=jax.ShapeDtypeStruct((n_proj, m, c), jnp.bfloat16),
        compiler_params=_cparams(("arbitrary",)),
        name="hyproj",
    )(h.reshape(batch, seq, d), *([w_in] * n_proj), taps, bias)


@functools.lru_cache(maxsize=None)
def _dft_tables(seq):
    n1 = seq // _NP
    n_fft = 2 * seq
    k1 = np.arange(_HALF)[None, :, None]
    t = _NP * np.arange(n1)[None, None, :] + np.arange(_NP)[:, None, None]
    theta = ((k1 * t) % n_fft) * (2.0 * math.pi / n_fft)
    valid = k1 <= n1
    cos = np.where(valid, np.cos(theta), 0.0)
    sin = np.where(valid, np.sin(theta), 0.0)
    fwd = np.concatenate([cos, -sin], axis=1)
    weight = np.where(k1 == 0, 1.0, 2.0) / n_fft
    inv_re = (weight * cos)[:, :n1]
    inv_im = (-weight * sin)[:, :n1]
    inv_im[:, 0, :] = ((-1.0) ** np.arange(n1))[None, :] / n_fft
    inv = np.concatenate([inv_re, inv_im], axis=1).transpose(0, 2, 1)
    return fwd.astype(np.float32), np.ascontiguousarray(inv).astype(np.float32)


def _fft4(c0, c2_sum, c2_diff, c1, c3):
    (d0r, d0i), (d2r, d2i) = c2_sum, c2_diff
    d1r, d1i = c1[0] + c3[0], c1[1] + c3[1]
    er, ei = c1[0] - c3[0], c1[1] - c3[1]
    return [(d0r + d1r, d0i + d1i), (d2r + ei, d2i - er), (d0r - d1r, d0i - d1i), (d2r - ei, d2i + er)]


def _fft8(re, im):
    s = [(re[j] + re[j + 4], im[j] + im[j + 4]) for j in range(4)]
    d = [(re[j] - re[j + 4], im[j] - im[j + 4]) for j in range(4)]
    even = _fft4(s[0], (s[0][0] + s[2][0], s[0][1] + s[2][1]), (s[0][0] - s[2][0], s[0][1] - s[2][1]), s[1], s[3])
    o1 = ((d[1][0] + d[1][1]) * _SQRT_HALF, (d[1][1] - d[1][0]) * _SQRT_HALF)
    o3 = ((d[3][1] - d[3][0]) * _SQRT_HALF, (d[3][0] + d[3][1]) * -_SQRT_HALF)
    odd = _fft4(d[0], (d[0][0] + d[2][1], d[0][1] - d[2][0]), (d[0][0] - d[2][1], d[0][1] + d[2][0]), o1, o3)
    out = [None] * 8
    for k in range(4):
        out[2 * k], out[2 * k + 1] = even[k], odd[k]
    return [o[0] for o in out], [o[1] for o in out]


def _ifft8(re, im):
    o_im, o_re = _fft8(im, re)
    return o_re, o_im


def _filter_kernel(w1_ref, b1_ref, w2_ref, b2_ref, w3_ref, b3_ref, fr_ref, wof_ref, wob_ref, dl_ref,
                   fwd_ref, o_ref, h_scr, af_scr, ab_scr, *, seq):
    n1 = seq // _NP
    fw = _FILTER_WIDTH
    hp = lax.Precision.HIGHEST

    @pl.when(pl.program_id(1) == 0)
    def _():
        nb = _FILTER_BANDS
        r = lax.broadcasted_iota(jnp.int32, (nb, seq), 1)
        pos = (_NP * (r % n1) + r // n1).astype(jnp.float32)
        band_idx = lax.broadcasted_iota(jnp.int32, (nb, seq), 0).astype(jnp.float32)
        band = 1e-4 + band_idx * ((nb - 1 - 1e-4) / (nb - 1))
        ang = (2.0 * math.pi / seq) * pos * band
        first_row = lax.broadcasted_iota(jnp.int32, (fw - 2 * nb, seq), 0) == 0
        tail = jnp.where(first_row, jnp.concatenate([pos] * ((fw - 2 * nb) // nb), axis=0) * (1.0 / (seq - 1)), 0.0)
        z = jnp.concatenate([jnp.cos(ang), -jnp.sin(ang), tail], axis=0)
        freq = fr_ref[0]
        h = jnp.sin(freq * (jnp.dot(w1_ref[0], z, precision=hp, preferred_element_type=jnp.float32) + b1_ref[0]))
        h = jnp.sin(freq * (jnp.dot(w2_ref[0], h, precision=hp, preferred_element_type=jnp.float32) + b2_ref[0]))
        h = jnp.sin(freq * (jnp.dot(w3_ref[0], h, precision=hp, preferred_element_type=jnp.float32) + b3_ref[0]))
        h_scr[...] = h.T

    r = lax.broadcasted_iota(jnp.int32, (seq, 1), 0)
    pos = (_NP * (r % n1) + r // n1).astype(jnp.float32)
    h = h_scr[...].astype(jnp.bfloat16)
    decay = jnp.exp(-(pos * (1.0 / (seq - 1))) * dl_ref[...])
    hf = jnp.dot(h, wof_ref[0].astype(jnp.bfloat16), preferred_element_type=jnp.float32) * decay
    hb = jnp.dot(h, wob_ref[0].astype(jnp.bfloat16), preferred_element_type=jnp.float32) * decay
    hb = jnp.where(pos == 0.0, 0.0, hb)
    hf = hf.astype(jnp.bfloat16)
    hb = hb.astype(jnp.bfloat16)
    for n2 in range(_NP):
        af_scr[n2] = jnp.dot(fwd_ref[n2], hf[n2 * n1:(n2 + 1) * n1], preferred_element_type=jnp.float32)
        ab_scr[n2] = jnp.dot(fwd_ref[n2], hb[n2 * n1:(n2 + 1) * n1], preferred_element_type=jnp.float32)

    rc = _V7X_BF16_SUBLANES

    def body(i, carry):
        r0 = pl.multiple_of(i * rc, rc)
        back = [(_NP - n2) % _NP for n2 in range(_NP)]
        kr, ki = _fft8([af_scr[n2, pl.ds(r0, rc)] + ab_scr[back[n2], pl.ds(r0, rc)] for n2 in range(_NP)],
                       [af_scr[n2, pl.ds(_HALF + r0, rc)] - ab_scr[back[n2], pl.ds(_HALF + r0, rc)] for n2 in range(_NP)])
        for k2 in range(_NP):
            o_ref[0, 0, k2, 0, pl.ds(r0, rc)] = kr[k2].astype(o_ref.dtype)
            o_ref[0, 0, k2, 1, pl.ds(r0, rc)] = ki[k2].astype(o_ref.dtype)
        return carry

    lax.fori_loop(0, _HALF // rc, body, 0)


def _filter_spectra(w1, b1, w2, b2, w3, b3, freq, w_out, fwd_tab, *, seq, ct):
    n_layers = w1.shape[0]
    c = w_out.shape[2] // (2 * _ORDER)
    n_ct = c // ct
    fw = _FILTER_WIDTH
    w1r = jnp.concatenate([w1[:, 1:], w1[:, :1]], axis=1)
    w1t = jnp.pad(w1r, ((0, 0), (0, fw - w1.shape[1]), (0, 0))).transpose(0, 2, 1)
    col = lambda a: a.reshape(n_layers, fw, 1)
    deltas = jnp.abs(jnp.linspace(_MIN_DECAY, _MAX_DECAY, c, dtype=jnp.float32)).reshape(1, c)
    lmap3 = lambda l, s: (l, 0, 0)
    wo_spec = lambda direction: pl.BlockSpec(
        (1, fw, ct), lambda l, s: (l, 0, (2 * (s // n_ct) + direction) * n_ct + s % n_ct))
    return pl.pallas_call(
        functools.partial(_filter_kernel, seq=seq),
        grid=(n_layers, _ORDER * n_ct),
        in_specs=[
            pl.BlockSpec((1, fw, fw), lmap3), pl.BlockSpec((1, fw, 1), lmap3),
            pl.BlockSpec((1, fw, fw), lmap3), pl.BlockSpec((1, fw, 1), lmap3),
            pl.BlockSpec((1, fw, fw), lmap3), pl.BlockSpec((1, fw, 1), lmap3),
            pl.BlockSpec((1, fw, 1), lmap3),
            wo_spec(0), wo_spec(1),
            pl.BlockSpec((1, ct), lambda l, s: (0, s % n_ct)),
            pl.BlockSpec((_NP, 2 * _HALF, seq // _NP), lambda l, s: (0, 0, 0)),
        ],
        out_specs=pl.BlockSpec((1, 1, _NP, 2, _HALF, ct), lambda l, s: (l, s // n_ct, 0, 0, 0, s % n_ct)),
        out_shape=jax.ShapeDtypeStruct((n_layers, _ORDER, _NP, 2, _HALF, c), _STAGE_TWO_DTYPE),
        scratch_shapes=[pltpu.VMEM((seq, fw), jnp.float32),
                        pltpu.VMEM((_NP, 2 * _HALF, ct), jnp.float32),
                        pltpu.VMEM((_NP, 2 * _HALF, ct), jnp.float32)],
        compiler_params=_cparams(("arbitrary", "arbitrary")),
        name="filter_spectra",
    )(w1t, col(b1), w2.transpose(0, 2, 1), col(b2), w3.transpose(0, 2, 1), col(b3), col(freq),
      w_out, w_out, deltas, fwd_tab)


def _hyena_kernel(v_ref, x1_ref, x2_ref, kf_ref, skip_ref, fwd_ref, inv_ref, o_ref, a_scr, c_scr, z_scr, *, seq):
    n1 = seq // _NP
    rc = _V7X_BF16_SUBLANES
    ct = o_ref.shape[-1]

    def conv_order(order, z_in_ref, gate_ref, z_out_ref):
        for n2 in range(_NP):
            zb = z_in_ref[0, n2 * n1:(n2 + 1) * n1].astype(jnp.bfloat16)
            a_scr[n2] = jnp.dot(fwd_ref[n2], zb, preferred_element_type=jnp.float32).astype(a_scr.dtype)

        def spectrum_rows(r0, ls):
            xr, xi = _fft8([a_scr[n2, pl.ds(r0, rc), ls] for n2 in range(_NP)],
                           [a_scr[n2, pl.ds(_HALF + r0, rc), ls] for n2 in range(_NP)])
            yr, yi = [], []
            for k2 in range(_NP):
                kr = kf_ref[order, k2, 0, pl.ds(r0, rc), ls]
                ki = kf_ref[order, k2, 1, pl.ds(r0, rc), ls]
                yr.append(xr[k2] * kr - xi[k2] * ki)
                yi.append(xr[k2] * ki + xi[k2] * kr)
            return _ifft8(yr, yi)

        def body(i, carry):
            r0 = pl.multiple_of(i * rc, rc)
            for l0 in range(0, ct, _V7X_LANES):
                ls = slice(l0, l0 + _V7X_LANES)
                cr, ci = spectrum_rows(r0, ls)
                for t2 in range(_NP):
                    c_scr[t2, pl.ds(r0, rc), ls] = cr[t2].astype(jnp.bfloat16)
                    c_scr[t2, pl.ds(n1 + r0, rc), ls] = ci[t2].astype(jnp.bfloat16)
            return carry

        first_row = lax.broadcasted_iota(jnp.int32, (rc, _V7X_LANES), 0) == 0
        for l0 in range(0, ct, _V7X_LANES):
            ls = slice(l0, l0 + _V7X_LANES)
            nr, ni = spectrum_rows(n1, ls)
            cr, ci = spectrum_rows(0, ls)
            for t2 in range(_NP):
                ang = math.pi * t2 / _NP
                nyq = nr[t2] * math.cos(ang) - ni[t2] * math.sin(ang)
                c_scr[t2, 0:rc, ls] = cr[t2].astype(jnp.bfloat16)
                c_scr[t2, n1:n1 + rc, ls] = jnp.where(first_row, nyq, ci[t2]).astype(jnp.bfloat16)
        lax.fori_loop(1, n1 // rc, body, 0)

        skip = skip_ref[order:order + 1]
        for t2 in range(_NP):
            rows = slice(t2 * n1, (t2 + 1) * n1)
            y = jnp.dot(inv_ref[t2], c_scr[t2], preferred_element_type=jnp.float32)
            z = z_in_ref[0, rows].astype(jnp.float32)
            gate = gate_ref[0, rows].astype(jnp.float32)
            z_out_ref[0, rows] = (gate * (y + z * skip)).astype(z_out_ref.dtype)

    conv_order(0, v_ref, x1_ref, z_scr)
    conv_order(1, z_scr, x2_ref, o_ref)


def _hyena(proj, kf_all, layer, skip, fwd_tab, inv_tab, *, batch, seq, ct):
    n_proj, m, c = proj.shape
    n1 = seq // _NP
    proj4 = proj.reshape(n_proj, batch, seq, c)
    pspec = lambda j: pl.BlockSpec((None, 1, seq, ct), lambda t, b, j=j: (j, b, 0, t))
    resident = dict(pipeline_mode=pl.Buffered(1))
    out = pl.pallas_call(
        functools.partial(_hyena_kernel, seq=seq),
        grid=(c // ct, batch),
        in_specs=[
            pspec(0), pspec(1), pspec(2),
            pl.BlockSpec((None, _ORDER, _NP, 2, _HALF, ct), lambda t, b: (layer, 0, 0, 0, 0, t), **resident),
            pl.BlockSpec((_ORDER, ct), lambda t, b: (0, t)),
            pl.BlockSpec((_NP, 2 * _HALF, n1), lambda t, b: (0, 0, 0), **resident),
            pl.BlockSpec((_NP, n1, 2 * n1), lambda t, b: (0, 0, 0), **resident),
        ],
        out_specs=pl.BlockSpec((1, seq, ct), lambda t, b: (b, 0, t)),
        out_shape=jax.ShapeDtypeStruct((batch, seq, c), jnp.bfloat16),
        scratch_shapes=[pltpu.VMEM((_NP, 2 * _HALF, ct), _STAGE_TWO_DTYPE),
                        pltpu.VMEM((_NP, 2 * n1, ct), jnp.bfloat16),
                        pltpu.VMEM((1, seq, ct), jnp.float32)],
        compiler_params=_cparams(("arbitrary", "arbitrary")),
        name="hyena",
    )(proj4, proj4, proj4, kf_all, skip, fwd_tab, inv_tab)
    return out.reshape(m, c)


def kernel(x, ffn1_pre_g, ffn1_w_gate, ffn1_w_up, ffn1_w_down, ffn1_post_g, mix_pre_g, mix_w_in, gmlp_ln_g, gmlp_ln_b, gmlp_w_s, gmlp_b_s, hy_conv_w, hy_conv_b, hy_filt_w1, hy_filt_b1, hy_filt_w2, hy_filt_b2, hy_filt_w3, hy_filt_b3, hy_filt_freq, hy_filt_w_out, hy_skip, mix_w_out, mix_post_g, ffn2_pre_g, ffn2_w_gate, ffn2_w_up, ffn2_w_down, ffn2_post_g):
    batch, seq, d = x.shape
    depth = ffn1_pre_g.shape[0]
    da = gmlp_ln_g.shape[1]
    assert seq % (_NP * _CHUNK) == 0 and seq // _NP + 1 <= _HALF
    m = batch * seq
    ffn_tiles = dict(batch=batch, tm=512, chunk=_V7X_MXU_DIM)
    ct = _V7X_MXU_DIM
    xp = x.reshape(m, d)

    fwd_np, inv_np = _dft_tables(seq)
    fwd_bf16 = jnp.asarray(fwd_np).astype(jnp.bfloat16)
    inv_bf16 = jnp.asarray(inv_np).astype(jnp.bfloat16)
    kf_all = _filter_spectra(hy_filt_w1, hy_filt_b1, hy_filt_w2, hy_filt_b2, hy_filt_w3, hy_filt_b3,
                             hy_filt_freq, hy_filt_w_out, fwd_bf16, seq=seq, ct=ct)

    ffn1_w = (ffn1_w_gate, ffn1_w_up, ffn1_w_down)
    ffn2_w = (ffn2_w_gate, ffn2_w_up, ffn2_w_down)
    w_in = mix_w_in
    w_out = mix_w_out.astype(jnp.bfloat16)

    for l in range(depth):
        xp, h = _ffn(xp, l, ffn1_pre_g, *ffn1_w, ffn1_post_g, g_next=mix_pre_g,
                     relayout="deinterleave" if l == 0 else None, **ffn_tiles)
        ya = _gmlp(h, w_in, l, gmlp_ln_g[l], gmlp_ln_b[l], gmlp_w_s[l], gmlp_b_s[l], batch=batch, seq=seq, jb=seq // _CHUNK)
        proj = _hyproj(h, w_in, l, hy_conv_w[l], hy_conv_b[l], batch=batch, seq=seq)
        yb = _hyena(proj, kf_all, l, hy_skip[l], fwd_bf16, inv_bf16, batch=batch, seq=seq, ct=proj.shape[-1])
        xp = _ffn(xp, l, ffn2_pre_g, *ffn2_w, ffn2_post_g, mix=(ya, yb, w_out, mix_post_g),
                  relayout="interleave" if l == depth - 1 else None, **ffn_tiles)

    return xp.reshape(batch, seq, d)
```
